```python
import math
import jax, jax.numpy as jnp
from jax import lax
import numpy as np

D_MODEL = 1024
BATCH = 4
SEQ = 8192
DEPTH = 1
DEC_BATCH = 8
DEC_SEQ = 16
PAST_LEN = 1024

CHUNK = 64
Q_BLK = 128
FOX_HEADS = 8
FOX_HD = 64
FOX_W = FOX_HEADS * FOX_HD
ML_HEADS = 4
ML_DK = 128
ML_DV = 128
ML_W = ML_HEADS * ML_DV
QK_W = 2 * ML_HEADS * ML_DK
MIX_W = FOX_W + ML_W
CONV_W = 4
N_EXPERTS = 32
TOP_K = 4
D_FF = D_MODEL
SWIGLU_LIMIT = 7.0
SWIGLU_ALPHA = 1.702
EPS = 1e-6
NEG = -1e30

O_FQ = 0
O_FK = O_FQ + FOX_W
O_FV = O_FK + FOX_W
O_FF = O_FV + FOX_W
O_MQ = O_FF + FOX_HEADS
O_MK = O_MQ + ML_HEADS * ML_DK
O_MV = O_MK + ML_HEADS * ML_DK
O_MO = O_MV + ML_W
O_MI = O_MO + ML_W
O_MF = O_MI + ML_HEADS
P_IN = O_MF + ML_HEADS

kernel_name = 'fox_mlstm_moe_stream_step'


def rmsnorm(x, g):
    xf = x.astype(jnp.float32)
    xf = xf * lax.rsqrt(jnp.mean(xf * xf, axis=-1, keepdims=True) + EPS)
    return xf.astype(x.dtype) * g.astype(x.dtype)


def head_rmsnorm(h, g):
    hf = h.astype(jnp.float32)
    hf = hf * lax.rsqrt(jnp.mean(hf * hf, axis=-1, keepdims=True) + EPS)
    return hf.reshape(h.shape[0], h.shape[1], -1) * g.astype(jnp.float32)


def causal_conv_silu(u, hist, w, b):
    L = u.shape[1]
    up = jnp.concatenate([hist.astype(u.dtype), u], axis=1)
    y = b.astype(u.dtype)
    for j in range(CONV_W):
        y = y + w[j].astype(u.dtype) * up[:, j:j + L]
    return jax.nn.silu(y), up[:, up.shape[1] - (CONV_W - 1):]


def project(xn, w_in, b_fox_f, conv_w, conv_b, b_ml_i, b_ml_f, conv_hist):
    B, L, _ = xn.shape
    f32 = jnp.float32
    z = xn @ w_in.astype(xn.dtype)
    fq = z[..., O_FQ:O_FK].reshape(B, L, FOX_HEADS, FOX_HD)
    fk = z[..., O_FK:O_FV].reshape(B, L, FOX_HEADS, FOX_HD)
    fv = z[..., O_FV:O_FF].reshape(B, L, FOX_HEADS, FOX_HD)
    flogf = jax.nn.log_sigmoid(z[..., O_FF:O_MQ].astype(f32) + b_fox_f.astype(f32))
    qk, conv_new = causal_conv_silu(z[..., O_MQ:O_MV], conv_hist, conv_w, conv_b)
    mq = qk[..., :QK_W // 2].reshape(B, L, ML_HEADS, ML_DK)
    mk = qk[..., QK_W // 2:].reshape(B, L, ML_HEADS, ML_DK) * (ML_DK ** -0.5)
    mv = z[..., O_MV:O_MO].reshape(B, L, ML_HEADS, ML_DV)
    mo = jax.nn.sigmoid(z[..., O_MO:O_MI])
    mi = z[..., O_MI:O_MF].astype(f32) + b_ml_i.astype(f32)
    mlogf = jax.nn.log_sigmoid(z[..., O_MF:P_IN].astype(f32) + b_ml_f.astype(f32))
    return (fq, fk, fv, flogf), (mq, mk, mv, mi, mlogf), mo, conv_new


def fox_attend(q, k, v, c_q, c_k, q_pos, k_pos):
    s = jnp.einsum('bqhd,bkhd->bhqk', q, k).astype(jnp.float32) * (FOX_HD ** -0.5)
    s = s + (jnp.swapaxes(c_q, 1, 2)[..., :, None] - jnp.swapaxes(c_k, 1, 2)[..., None, :])
    s = jnp.where((k_pos[None, :] <= q_pos[:, None])[None, None], s, NEG)
    p = jax.nn.softmax(s, axis=-1)
    return jnp.einsum('bhqk,bkhd->bqhd', p.astype(v.dtype), v)


def fox_prompt(q, k, v, logf):
    B, S = q.shape[0], q.shape[1]
    c = jnp.cumsum(logf, axis=1)
    pos = jnp.arange(S)

    def blk(i):
        s0 = i * Q_BLK
        qb = lax.dynamic_slice_in_dim(q, s0, Q_BLK, axis=1)
        cb = lax.dynamic_slice_in_dim(c, s0, Q_BLK, axis=1)
        return fox_attend(qb, k, v, cb, c, s0 + jnp.arange(Q_BLK), pos)

    out = lax.map(blk, jnp.arange(S // Q_BLK))
    return jnp.moveaxis(out, 0, 1).reshape(B, S, FOX_HEADS, FOX_HD)


def fox_sample(q, k, v, logf, ck, cv, clogf):
    P, L = ck.shape[1], q.shape[1]
    k_all = jnp.concatenate([ck.astype(k.dtype), k], axis=1)
    v_all = jnp.concatenate([cv.astype(v.dtype), v], axis=1)
    c = jnp.cumsum(jnp.concatenate([clogf.astype(jnp.float32), logf], axis=1), axis=1)
    return fox_attend(q, k_all, v_all, c[:, P:], c, P + jnp.arange(L), jnp.arange(P + L))


def mlstm_chunk(state, inp):
    f32 = jnp.float32
    C, n, m = state
    q, k, v, itil, logf = inp
    q, k, v = q.astype(f32), k.astype(f32), v.astype(f32)
    L = q.shape[1]
    b = jnp.cumsum(logf.astype(f32), axis=1)
    causal = jnp.tril(jnp.ones((L, L), dtype=bool))
    lw = b[:, :, None, :] - b[:, None, :, :] + itil.astype(f32)[:, None, :, :]
    lw = jnp.where(causal[None, :, :, None], lw, -jnp.inf)
    li = m[:, None, :] + b
    m_t = jnp.maximum(li, jnp.max(lw, axis=2))
    dmat = jnp.exp(lw - m_t[:, :, None, :])
    a = jnp.exp(li - m_t)
    s = jnp.einsum('bthd,bshd->btsh', q, k) * dmat
    num = a[..., None] * jnp.einsum('bhvd,bthd->bthv', C, q) + jnp.einsum('btsh,bshv->bthv', s, v)
    den = a * jnp.einsum('bhd,bthd->bth', n, q) + jnp.sum(s, axis=2)
    h = num / jnp.maximum(jnp.abs(den), jnp.exp(-m_t))[..., None]
    m_new = m_t[:, -1]
    w_last = dmat[:, -1]
    a_last = a[:, -1]
    C_new = a_last[..., None, None] * C + jnp.einsum('bsh,bshv,bshd->bhvd', w_last, v, k)
    n_new = a_last[..., None] * n + jnp.einsum('bsh,bshd->bhd', w_last, k)
    return (C_new, n_new, m_new), h


def mlstm_prompt(q, k, v, itil, logf):
    B, S = q.shape[0], q.shape[1]
    nc = S // CHUNK

    def ch(a):
        return jnp.moveaxis(a.reshape(B, nc, CHUNK, *a.shape[2:]), 1, 0)

    f32 = jnp.float32
    init = (jnp.zeros((B, ML_HEADS, ML_DV, ML_DK), f32),
            jnp.zeros((B, ML_HEADS, ML_DK), f32),
            jnp.zeros((B, ML_HEADS), f32))
    st, h = lax.scan(mlstm_chunk, init, (ch(q), ch(k), ch(v), ch(itil), ch(logf)))
    return jnp.moveaxis(h, 0, 1).reshape(B, S, ML_HEADS, ML_DV), st


def merge(fox_h, ml_h, mo, g_fox, g_ml, w_out, dtype):
    a = head_rmsnorm(fox_h, g_fox)
    m = head_rmsnorm(ml_h, g_ml) * mo.astype(jnp.float32)
    cat = jnp.concatenate([a, m], axis=-1).astype(dtype)
    return cat @ w_out.astype(dtype)


def clamped_swiglu(gu):
    gate = jnp.minimum(gu[:, :D_FF], SWIGLU_LIMIT)
    up = jnp.clip(gu[:, D_FF:], -SWIGLU_LIMIT, SWIGLU_LIMIT)
    return (up + 1) * (gate * jax.nn.sigmoid(SWIGLU_ALPHA * gate))


def moe(x, w_router, b_router, w_gate_up, b_gate_up, w_down, b_down):
    f32 = jnp.float32
    B, L, D = x.shape
    t = x.reshape(B * L, D)
    logits = (t @ w_router.astype(t.dtype)).astype(f32) + b_router.astype(f32)
    top_v, top_i = lax.top_k(logits, TOP_K)
    gates = jax.nn.softmax(top_v, axis=-1)
    comb = jnp.einsum('tk,tke->te', gates, jax.nn.one_hot(top_i, N_EXPERTS, dtype=f32))
    out = jnp.zeros((B * L, D), f32)
    for e in range(N_EXPERTS):
        gu = t @ w_gate_up[e].astype(t.dtype) + b_gate_up[e].astype(t.dtype)
        ye = clamped_swiglu(gu) @ w_down[e].astype(t.dtype) + b_down[e].astype(t.dtype)
        out = out + comb[:, e:e + 1] * ye.astype(f32)
    return out.astype(x.dtype).reshape(B, L, D)


def setup_inputs(seed: int = 0) -> dict:
    key = jax.random.key(seed)
    ks = jax.random.split(key, 28)
    f32 = jnp.float32

    def nrm(k, shape, scale=1.0):
        return jax.random.normal(k, shape, f32) * scale

    def gain(k, shape):
        return 1.0 + nrm(k, shape, 0.02)

    def unif(k, shape, lo, hi):
        return jax.random.uniform(k, shape, f32, lo, hi)

    return {
        'x_prompt': nrm(ks[0], (BATCH, SEQ, D_MODEL)),
        'x_sample': nrm(ks[1], (DEC_BATCH, DEC_SEQ, D_MODEL)),
        'cache_fox_k': nrm(ks[2], (DEPTH, DEC_BATCH, PAST_LEN, FOX_HEADS, FOX_HD)),
        'cache_fox_v': nrm(ks[3], (DEPTH, DEC_BATCH, PAST_LEN, FOX_HEADS, FOX_HD)),
        'cache_fox_logf': jax.nn.log_sigmoid(2.5 + nrm(ks[4], (DEPTH, DEC_BATCH, PAST_LEN, FOX_HEADS))),
        'state_mlstm_C': nrm(ks[5], (DEPTH, DEC_BATCH, ML_HEADS, ML_DV, ML_DK), 0.1),
        'state_mlstm_n': nrm(ks[6], (DEPTH, DEC_BATCH, ML_HEADS, ML_DK), 0.1),
        'state_mlstm_m': nrm(ks[7], (DEPTH, DEC_BATCH, ML_HEADS)),
        'state_mlstm_conv': nrm(ks[8], (DEPTH, DEC_BATCH, CONV_W - 1, QK_W)),
        'norm_mix_g': gain(ks[9], (DEPTH, D_MODEL)),
        'w_in': nrm(ks[10], (DEPTH, D_MODEL, P_IN), D_MODEL ** -0.5),
        'b_fox_f': unif(ks[11], (DEPTH, FOX_HEADS), 1.0, 4.0),
        'conv_w': nrm(ks[12], (DEPTH, CONV_W, QK_W), CONV_W ** -0.5),
        'conv_b': nrm(ks[13], (DEPTH, QK_W), 0.02),
        'b_ml_i': nrm(ks[14], (DEPTH, ML_HEADS), 0.1),
        'b_ml_f': unif(ks[15], (DEPTH, ML_HEADS), 3.0, 6.0),
        'g_fox': gain(ks[16], (DEPTH, FOX_W)),
        'g_ml': gain(ks[17], (DEPTH, ML_W)),
        'w_out': nrm(ks[18], (DEPTH, MIX_W, D_MODEL), MIX_W ** -0.5),
        'norm_ffn_g': gain(ks[19], (DEPTH, D_MODEL)),
        'w_router': nrm(ks[20], (DEPTH, D_MODEL, N_EXPERTS), D_MODEL ** -0.5),
        'b_router': nrm(ks[21], (DEPTH, N_EXPERTS), 0.01),
        'w_gate_up': nrm(ks[22], (DEPTH, N_EXPERTS, D_MODEL, 2 * D_FF), D_MODEL ** -0.5),
        'b_gate_up': nrm(ks[23], (DEPTH, N_EXPERTS, 2 * D_FF), 0.01),
        'w_down': nrm(ks[24], (DEPTH, N_EXPERTS, D_FF, D_MODEL), D_FF ** -0.5),
        'b_down': nrm(ks[25], (DEPTH, N_EXPERTS, D_MODEL), 0.01),
        'norm_final_g': gain(ks[26], (D_MODEL,)),
    }


def reference(x_prompt, x_sample, cache_fox_k, cache_fox_v, cache_fox_logf,
              state_mlstm_C, state_mlstm_n, state_mlstm_m, state_mlstm_conv,
              norm_mix_g, w_in, b_fox_f, conv_w, conv_b, b_ml_i, b_ml_f,
              g_fox, g_ml, w_out, norm_ffn_g, w_router, b_router,
              w_gate_up, b_gate_up, w_down, b_down, norm_final_g):
    f32 = jnp.float32
    yp, ys = x_prompt, x_sample
    Bp = x_prompt.shape[0]
    p_st, s_st = [], []
    for l in range(DEPTH):
        xn = rmsnorm(yp, norm_mix_g[l])
        hist0 = jnp.zeros((Bp, CONV_W - 1, QK_W), xn.dtype)
        fox_in, ml_in, mo, conv_new = project(xn, w_in[l], b_fox_f[l], conv_w[l], conv_b[l],
                                              b_ml_i[l], b_ml_f[l], hist0)
        fox_h = fox_prompt(*fox_in)
        ml_h, (C_new, n_new, m_new) = mlstm_prompt(*ml_in)
        yp = yp + merge(fox_h, ml_h, mo, g_fox[l], g_ml[l], w_out[l], yp.dtype).astype(yp.dtype)
        yp = yp + moe(rmsnorm(yp, norm_ffn_g[l]), w_router[l], b_router[l],
                      w_gate_up[l], b_gate_up[l], w_down[l], b_down[l])
        p_st.append((fox_in[1], fox_in[2], fox_in[3], C_new, n_new, m_new, conv_new))

        xn = rmsnorm(ys, norm_mix_g[l])
        fox_in, ml_in, mo, conv_new = project(xn, w_in[l], b_fox_f[l], conv_w[l], conv_b[l],
                                              b_ml_i[l], b_ml_f[l], state_mlstm_conv[l])
        fox_h = fox_sample(*fox_in, cache_fox_k[l], cache_fox_v[l], cache_fox_logf[l])
        (C_new, n_new, m_new), ml_h = mlstm_chunk(
            (state_mlstm_C[l].astype(f32), state_mlstm_n[l].astype(f32), state_mlstm_m[l].astype(f32)),
            ml_in)
        ys = ys + merge(fox_h, ml_h, mo, g_fox[l], g_ml[l], w_out[l], ys.dtype).astype(ys.dtype)
        ys = ys + moe(rmsnorm(ys, norm_ffn_g[l]), w_router[l], b_router[l],
                      w_gate_up[l], b_gate_up[l], w_down[l], b_down[l])
        s_st.append((fox_in[1], fox_in[2], fox_in[3], C_new, n_new, m_new, conv_new))

    y_prompt = rmsnorm(yp, norm_final_g)
    y_sample = rmsnorm(ys, norm_final_g)
    p_fox_k, p_fox_v, p_fox_logf, p_C, p_n, p_m, p_conv = (jnp.stack(a) for a in zip(*p_st))
    s_fox_k, s_fox_v, s_fox_logf, s_C, s_n, s_m, s_conv = (jnp.stack(a) for a in zip(*s_st))
    return (y_prompt, y_sample, p_fox_k, p_fox_v, p_fox_logf, p_C, p_n, p_m, p_conv,
            s_fox_k, s_fox_v, s_fox_logf, s_C, s_n, s_m, s_conv)
```

```python
import functools

import jax
import jax.numpy as jnp
from jax import lax
from jax.experimental import pallas as pl
from jax.experimental.pallas import tpu as pltpu

F32 = jnp.float32
BF16 = jnp.bfloat16

FOX_HEADS = 8
FOX_HD = 64
FOX_W = FOX_HEADS * FOX_HD
ML_HEADS = 4
ML_DK = 128
ML_DV = 128
ML_W = ML_HEADS * ML_DV
QK_W = 2 * ML_HEADS * ML_DK
CONV_W = 4
CHUNK = 64
TOP_K = 4
SWIGLU_LIMIT = 7.0
SWIGLU_ALPHA = 1.702
EPS = 1e-6
NEG = -1e30

O_FQ = 0
O_FK = O_FQ + FOX_W
O_FV = O_FK + FOX_W
O_FF = O_FV + FOX_W
O_MQ = O_FF + FOX_HEADS
O_MK = O_MQ + ML_HEADS * ML_DK
O_MV = O_MK + ML_HEADS * ML_DK
O_MO = O_MV + ML_W
O_MI = O_MO + ML_W
O_MF = O_MI + ML_HEADS
P_IN = O_MF + ML_HEADS

B_FQ, B_FK, B_FV, B_QK, B_MV, B_MO, B_END = 0, 512, 1024, 1536, 2560, 3072, 3584
S_FF, S_MI, S_MF, S_END = 0, 8, 12, 16

V7X_LANES = 128
V7X_SUBLANES = 8
V7X_VMEM_LIMIT_BYTES = 56 * 1024 * 1024

PROJ_ROWS = 512
FOX_BLOCK = 512
MERGE_ROWS = 512
MOE_ROWS = 512
FFN_ROWS = 512


def _params(*semantics):
    return pltpu.CompilerParams(dimension_semantics=semantics, vmem_limit_bytes=V7X_VMEM_LIMIT_BYTES)


def _dot(a, b):
    return jnp.dot(a, b, preferred_element_type=F32)


def _dot_nt(a, b):
    return lax.dot_general(a, b, (((1,), (1,)), ((), ())), preferred_element_type=F32)


def _dot_tn(a, b):
    return lax.dot_general(a, b, (((0,), (0,)), ((), ())), preferred_element_type=F32)


def _sigmoid(x):
    return 1.0 / (1.0 + jnp.exp(-x))


def _log_sigmoid(x):
    return jnp.minimum(x, 0.0) - jnp.log1p(jnp.exp(-jnp.abs(x)))


def _proj_kernel(x_ref, g_ref, wbig_ref, wsm_ref, bsm_ref, cw_ref, cb_ref, hist_ref,
                 fq_ref, fk32_ref, fv32_ref, fk16_ref, fv16_ref, sm_ref, mq_ref, mk_ref, mv_ref, mo_ref,
                 cnew_ref, halo_ref):
    tm = x_ref.shape[1]

    @pl.when(pl.program_id(1) == 0)
    def _():
        halo_ref[...] = hist_ref[0]

    x = x_ref[0]
    xn = (x * lax.rsqrt(jnp.mean(x * x, axis=-1, keepdims=True) + EPS)) * g_ref[...]
    xb = xn.astype(BF16)
    z = _dot(xb, wbig_ref[...])

    fq_ref[0] = (z[:, B_FQ:B_FK] * (FOX_HD ** -0.5)).astype(BF16)
    fk = z[:, B_FK:B_FV]
    fv = z[:, B_FV:B_QK]
    fk32_ref[0] = fk
    fv32_ref[0] = fv
    fk16_ref[0] = fk.astype(BF16)
    fv16_ref[0] = fv.astype(BF16)
    mv_ref[0] = z[:, B_MV:B_MO].astype(BF16)
    mo_ref[0] = _sigmoid(z[:, B_MO:B_END]).astype(BF16)

    zs = _dot(xb, wsm_ref[...]) + bsm_ref[...]
    lane = lax.broadcasted_iota(jnp.int32, zs.shape, 1)
    is_forget = (lane < S_MI) | ((lane >= S_MF) & (lane < S_END))
    sm_ref[0] = jnp.where(is_forget, _log_sigmoid(zs), zs)

    u = z[:, B_QK:B_MV]
    up = jnp.concatenate([halo_ref[...], u], axis=0)
    n = tm + V7X_SUBLANES
    first = V7X_SUBLANES - (CONV_W - 1)
    y = cb_ref[...] + cw_ref[0:1, :] * pltpu.roll(up, n - first, axis=0)[:tm]
    for j in range(1, CONV_W - 1):
        y = y + cw_ref[j:j + 1, :] * pltpu.roll(up, n - (first + j), axis=0)[:tm]
    y = y + cw_ref[CONV_W - 1:CONV_W, :] * u
    qk = y * _sigmoid(y)
    mq_ref[0] = qk[:, :QK_W // 2].astype(BF16)
    mk_ref[0] = (qk[:, QK_W // 2:] * (ML_DK ** -0.5)).astype(BF16)

    halo_ref[...] = u[tm - V7X_SUBLANES:, :]
    cnew_ref[0] = halo_ref[first:, :]


def _project(x, hist8, g, wbig, wsm, bsm, cw, cb, tm):
    bsz, seq, d = x.shape
    grid = (bsz, seq // tm)
    row = lambda c: pl.BlockSpec((1, tm, c), lambda b, s: (b, s, 0))
    const = lambda shape: pl.BlockSpec(shape, lambda b, s: (0,) * len(shape))
    outs = [
        (FOX_W, BF16), (FOX_W, F32), (FOX_W, F32), (FOX_W, BF16), (FOX_W, BF16), (V7X_LANES, F32),
        (ML_W, BF16), (ML_W, BF16), (ML_W, BF16), (ML_W, BF16),
    ]
    out_shape = [jax.ShapeDtypeStruct((bsz, seq, c), dt) for c, dt in outs]
    out_specs = [row(c) for c, _ in outs]
    out_shape.append(jax.ShapeDtypeStruct((bsz, CONV_W - 1, QK_W), F32))
    out_specs.append(pl.BlockSpec((1, CONV_W - 1, QK_W), lambda b, s: (b, 0, 0)))
    return pl.pallas_call(
        _proj_kernel,
        grid=grid,
        in_specs=[row(d), const((1, d)), const(wbig.shape), const(wsm.shape), const((1, V7X_LANES)),
                  const((CONV_W, QK_W)), const((1, QK_W)),
                  pl.BlockSpec((1, V7X_SUBLANES, QK_W), lambda b, s: (b, 0, 0))],
        out_specs=out_specs,
        out_shape=out_shape,
        scratch_shapes=[pltpu.VMEM((V7X_SUBLANES, QK_W), F32)],
        compiler_params=_params("arbitrary", "arbitrary"),
        name="proj",
    )(x, g, wbig, wsm, bsm, cw, cb, hist8)


def _seg_scan(x, seg, op, ident):
    idx = lax.broadcasted_iota(jnp.int32, x.shape, 1) % seg
    sh = 1
    while sh < seg:
        x = op(x, jnp.where(idx >= sh, pltpu.roll(x, sh, axis=1), ident))
        sh *= 2
    return x


def _cumsum_kernel(x_ref, o_ref):
    x = x_ref[...]
    o_ref[...] = _seg_scan(x, x.shape[1], jnp.add, 0.0)


def _row_cumsum(x):
    return pl.pallas_call(
        _cumsum_kernel, out_shape=jax.ShapeDtypeStruct(x.shape, F32), name="fox_cumsum",
        compiler_params=pltpu.CompilerParams(vmem_limit_bytes=V7X_VMEM_LIMIT_BYTES),
    )(x)


def _ml_gate_kernel(logf_ref, i_ref, b_ref, g_ref, gmax_ref, *, seg):
    b = _seg_scan(logf_ref[...], seg, jnp.add, 0.0)
    g = i_ref[...] - b
    b_ref[...] = b
    g_ref[...] = g
    gmax_ref[...] = _seg_scan(g, seg, jnp.maximum, -jnp.inf)


def _ml_gates(logf_rows, i_rows, seg):
    sds = jax.ShapeDtypeStruct(logf_rows.shape, F32)
    return pl.pallas_call(
        functools.partial(_ml_gate_kernel, seg=seg), out_shape=[sds, sds, sds], name="mlstm_gates",
        compiler_params=pltpu.CompilerParams(vmem_limit_bytes=V7X_VMEM_LIMIT_BYTES),
    )(logf_rows, i_rows)


def _fox_kernel(q_ref, k_ref, v_ref, ck_ref, g_ref, o_ref, *, tq, tk, q_offset):
    i = pl.program_id(2)
    q = q_ref[0]
    lane = lax.broadcasted_iota(jnp.int32, q.shape, 1)
    first_head = lane < FOX_HD
    qh = (jnp.where(first_head, q, jnp.zeros_like(q)), jnp.where(first_head, jnp.zeros_like(q), q))

    q_lo = q_offset + i * tq
    n_full = (q_lo + 1) // tk
    n_total = (q_lo + tq + tk - 1) // tk

    def block(j, carry, masked):
        start = pl.multiple_of(j * tk, tk)
        kb = k_ref[0, pl.ds(start, tk), :]
        vb = v_ref[0, pl.ds(start, tk), :]
        out = []
        for hh in range(2):
            m, l, acc = carry[hh]
            s = _dot_nt(qh[hh], kb) - ck_ref[0, 0, hh:hh + 1, pl.ds(start, tk)]
            if masked:
                qpos = q_lo + lax.broadcasted_iota(jnp.int32, s.shape, 0)
                kpos = start + lax.broadcasted_iota(jnp.int32, s.shape, 1)
                s = jnp.where(kpos <= qpos, s, NEG)
            m_new = jnp.maximum(m, jnp.max(s, axis=-1, keepdims=True))
            alpha = jnp.exp(m - m_new)
            p = jnp.exp(s - m_new)
            l = alpha * l + jnp.sum(p, axis=-1, keepdims=True)
            acc = alpha * acc + _dot(p.astype(BF16), vb)
            out.append((m_new, l, acc))
        return tuple(out)

    init_head = (jnp.full((tq, 1), NEG, F32), jnp.zeros((tq, 1), F32), jnp.zeros((tq, V7X_LANES), F32))
    carry = lax.fori_loop(0, n_full, functools.partial(block, masked=False), (init_head, init_head))
    carry = lax.fori_loop(n_full, n_total, functools.partial(block, masked=True), carry)

    out = jnp.zeros((tq, V7X_LANES), F32)
    for hh in range(2):
        _, l, acc = carry[hh]
        own = first_head if hh == 0 else jnp.logical_not(first_head)
        o = jnp.where(own, acc / l, 0.0)
        out = out + o * lax.rsqrt(jnp.sum(o * o, axis=-1, keepdims=True) * (1.0 / FOX_HD) + EPS)
    o_ref[0] = (out * g_ref[...]).astype(BF16)


def _fox(q, k, v, ck, g_fox, q_offset, tq, tk):
    bsz, lq, _ = q.shape
    lk = k.shape[1]
    pairs = FOX_HEADS // 2
    grid = (bsz, pairs, lq // tq)
    kv_spec = pl.BlockSpec((1, lk, V7X_LANES), lambda b, p, i: (b, 0, p))
    q_spec = pl.BlockSpec((1, tq, V7X_LANES), lambda b, p, i: (b, i, p))
    return pl.pallas_call(
        functools.partial(_fox_kernel, tq=tq, tk=tk, q_offset=q_offset),
        grid=grid,
        in_specs=[q_spec, kv_spec, kv_spec,
                  pl.BlockSpec((1, 1, 2, lk), lambda b, p, i: (b, p, 0, 0)),
                  pl.BlockSpec((1, V7X_LANES), lambda b, p, i: (0, p))],
        out_specs=q_spec,
        out_shape=jax.ShapeDtypeStruct((bsz, lq, FOX_W), BF16),
        compiler_params=_params("arbitrary", "arbitrary", "arbitrary"),
        name="fox_attention",
    )(q, k, v, ck, g_fox)


def _mlstm_kernel(q_ref, k_ref, v_ref, mo_ref, col_ref, row_ref, c0_ref, n0_ref, m0_ref, g_ref,
                  o_ref, c_ref, n_ref, m_ref):
    bsz, L, _ = q_ref.shape

    @pl.when(pl.program_id(0) == 0)
    def _():
        c_ref[...] = c0_ref[...]
        n_ref[...] = n0_ref[...]
        m_ref[...] = m0_ref[...]

    causal = (lax.broadcasted_iota(jnp.int32, (L, L), 1) <= lax.broadcasted_iota(jnp.int32, (L, L), 0))

    def per_batch(b, _):
        col = col_ref[b]
        bcol = col[:, 0:ML_HEADS]
        gmax = col[:, ML_HEADS:2 * ML_HEADS]
        gcol = col[:, 2 * ML_HEADS:3 * ML_HEADS]
        m = m_ref[pl.ds(b, 1), :]
        u = jnp.maximum(m, gmax)
        a = jnp.exp(m - u)
        mt = bcol + u
        em = jnp.exp(-mt)
        u_last = u[L - 1:L, :]
        a_last = a[L - 1:L, :]
        wcol = jnp.exp(gcol - u_last)
        m_ref[pl.ds(b, 1), :] = mt[L - 1:L, :]

        for h in range(ML_HEADS):
            cs = slice(h * ML_DK, (h + 1) * ML_DK)
            qh = q_ref[b, :, cs]
            kh = k_ref[b, :, cs]
            vh = v_ref[b, :, cs]
            grow = row_ref[0, pl.ds(b * ML_HEADS + h, 1), :]
            decay = jnp.exp(jnp.where(causal, grow - u[:, h:h + 1], NEG))
            s = _dot_nt(qh, kh) * decay
            c_old = c_ref[b, h]
            n_old = n_ref[b, h:h + 1, :]
            a_h = a[:, h:h + 1]
            num = a_h * _dot_nt(qh, c_old.astype(BF16)) + _dot(s.astype(BF16), vh)
            den = (a_h * jnp.sum(qh.astype(F32) * n_old, axis=-1, keepdims=True)
                   + jnp.sum(s, axis=-1, keepdims=True))
            hv = num / jnp.maximum(jnp.abs(den), em[:, h:h + 1])
            hn = hv * lax.rsqrt(jnp.mean(hv * hv, axis=-1, keepdims=True) + EPS)
            o_ref[b, :, cs] = (hn * g_ref[:, cs] * mo_ref[b, :, cs].astype(F32)).astype(BF16)

            w_h = wcol[:, h:h + 1]
            vw = (vh.astype(F32) * w_h).astype(BF16)
            c_ref[b, h] = a_last[:, h:h + 1] * c_old + _dot_tn(vw, kh)
            n_ref[b, h:h + 1, :] = (a_last[:, h:h + 1] * n_old
                                    + jnp.sum(kh.astype(F32) * w_h, axis=0, keepdims=True))
        return 0

    lax.fori_loop(0, bsz, per_batch, 0)


def _mlstm(mq, mk, mv, mo, colpack, rowpack, c0, n0, m0, g_ml, L):
    bsz, seq, _ = mq.shape
    nc = seq // L
    chunk = pl.BlockSpec((bsz, L, ML_W), lambda c: (0, c, 0))
    full = lambda shape: pl.BlockSpec(shape, lambda c: (0,) * len(shape))
    return pl.pallas_call(
        _mlstm_kernel,
        grid=(nc,),
        in_specs=[chunk, chunk, chunk, chunk,
                  pl.BlockSpec((bsz, L, colpack.shape[-1]), lambda c: (0, c, 0)),
                  pl.BlockSpec((1, bsz * ML_HEADS, L), lambda c: (c, 0, 0)),
                  full(c0.shape), full(n0.shape), full(m0.shape), full((1, ML_W))],
        out_specs=[chunk, full(c0.shape), full(n0.shape), full(m0.shape)],
        out_shape=[jax.ShapeDtypeStruct((bsz, seq, ML_W), BF16),
                   jax.ShapeDtypeStruct(c0.shape, F32),
                   jax.ShapeDtypeStruct(n0.shape, F32),
                   jax.ShapeDtypeStruct(m0.shape, F32)],
        compiler_params=_params("arbitrary"),
        name="mlstm",
    )(mq, mk, mv, mo, colpack, rowpack, c0, n0, m0, g_ml)


def _merge_kernel(ca_ref, cm_ref, x_ref, wa_ref, wm_ref, g_ref, wrh_ref, wrl_ref, br_ref,
                  y1_ref, xn_ref, ridx_ref, rgate_ref, cnt_ref):
    tm = x_ref.shape[0]

    @pl.when(pl.program_id(0) == 0)
    def _():
        cnt_ref[...] = jnp.zeros_like(cnt_ref)

    y1 = x_ref[...] + (_dot(ca_ref[...], wa_ref[...]) + _dot(cm_ref[...], wm_ref[...]))
    y1_ref[...] = y1
    xn = (y1 * lax.rsqrt(jnp.mean(y1 * y1, axis=-1, keepdims=True) + EPS)) * g_ref[...]
    xn_ref[...] = xn

    xh = xn.astype(BF16)
    xl = (xn - xh.astype(F32)).astype(BF16)
    logits = (_dot(xh, wrh_ref[...]) + (_dot(xl, wrh_ref[...]) + _dot(xh, wrl_ref[...]))) + br_ref[...]

    lane = lax.broadcasted_iota(jnp.int32, logits.shape, 1)
    vals, sel = [], []
    work = logits
    for _ in range(TOP_K):
        mx = jnp.max(work, axis=-1, keepdims=True)
        idx = jnp.min(jnp.where(work == mx, lane, V7X_LANES), axis=-1, keepdims=True)
        vals.append(mx)
        sel.append(idx)
        work = jnp.where(lane == idx, -jnp.inf, work)
    ex = [jnp.exp(v - vals[0]) for v in vals]
    tot = ex[0] + ex[1] + ex[2] + ex[3]

    onehot = [(lane == idx) for idx in sel]
    picked = jnp.where(onehot[0] | onehot[1] | onehot[2] | onehot[3], 1.0, 0.0)
    earlier = (lax.broadcasted_iota(jnp.int32, (tm, tm), 1) < lax.broadcasted_iota(jnp.int32, (tm, tm), 0))
    before = _dot(jnp.where(earlier, 1.0, 0.0).astype(BF16), picked.astype(BF16)) + cnt_ref[...]
    cnt_ref[...] = cnt_ref[...] + jnp.sum(picked, axis=0, keepdims=True)

    ridx = jnp.zeros(logits.shape, jnp.int32)
    rgate = jnp.zeros(logits.shape, F32)
    for k in range(TOP_K):
        rank = jnp.sum(jnp.where(onehot[k], before, 0.0), axis=-1, keepdims=True).astype(jnp.int32)
        ridx = jnp.where(lane == k, sel[k], ridx)
        ridx = jnp.where(lane == TOP_K + k, rank, ridx)
        rgate = jnp.where(lane == k, ex[k] / tot, rgate)
    ridx_ref[...] = ridx
    rgate_ref[...] = rgate


def _merge(cat_a, cat_m, x, w_a, w_m, g_ffn, wr_hi, wr_lo, br, tm):
    t, d = x.shape
    row = lambda c: pl.BlockSpec((tm, c), lambda i: (i, 0))
    const = lambda shape: pl.BlockSpec(shape, lambda i: (0,) * len(shape))
    return pl.pallas_call(
        _merge_kernel,
        grid=(t // tm,),
        in_specs=[row(FOX_W), row(ML_W), row(d), const(w_a.shape), const(w_m.shape), const((1, d)),
                  const(wr_hi.shape), const(wr_lo.shape), const((1, V7X_LANES))],
        out_specs=[row(d), row(d), row(V7X_LANES), row(V7X_LANES), const((1, V7X_LANES))],
        out_shape=[jax.ShapeDtypeStruct((t, d), F32), jax.ShapeDtypeStruct((t, d), F32),
                   jax.ShapeDtypeStruct((t, V7X_LANES), jnp.int32),
                   jax.ShapeDtypeStruct((t, V7X_LANES), F32),
                   jax.ShapeDtypeStruct((1, V7X_LANES), F32)],
        compiler_params=_params("arbitrary"),
        name="merge_router",
    )(cat_a, cat_m, x, w_a, w_m, g_ffn, wr_hi, wr_lo, br)


def _row_copy(src_ref, src_row, dst_ref, dst_row, sem):
    return pltpu.make_async_copy(src_ref.at[pl.ds(src_row, 1)], dst_ref.at[pl.ds(dst_row, 1)], sem)


def _scatter_kernel(pos_ref, x_ref, xs_in_ref, xs_ref, sem):
    del xs_in_ref
    tm = x_ref.shape[0]

    def issue(r, _):
        for k in range(TOP_K):
            _row_copy(x_ref, r, xs_ref, pos_ref[0, 0, TOP_K * r + k], sem).start()
        return 0

    def drain(r, _):
        for k in range(TOP_K):
            _row_copy(x_ref, 0, xs_ref, 0, sem).wait()
        return 0

    lax.fori_loop(0, tm, issue, 0)
    lax.fori_loop(0, tm, drain, 0)


def _scatter(pos3, x, xs, tm):
    t, d = x.shape
    return pl.pallas_call(
        _scatter_kernel,
        grid=(t // tm,),
        in_specs=[pl.BlockSpec((1, 1, TOP_K * tm), lambda i: (i, 0, 0), memory_space=pltpu.SMEM),
                  pl.BlockSpec((tm, d), lambda i: (i, 0)),
                  pl.BlockSpec(memory_space=pl.ANY)],
        out_specs=pl.BlockSpec(memory_space=pl.ANY),
        out_shape=jax.ShapeDtypeStruct(xs.shape, xs.dtype),
        scratch_shapes=[pltpu.SemaphoreType.DMA(())],
        input_output_aliases={2: 0},
        compiler_params=_params("arbitrary"),
        name="moe_scatter",
    )(pos3, x, xs)


def _ffn_kernel(te_ref, nu_ref, x_ref, wgu_ref, bgu_ref, wd_ref, bd_ref, y_ref, wgu16_ref, wd16_ref):
    i = pl.program_id(0)
    d_ff = wd_ref.shape[1]

    @pl.when((i == 0) | (te_ref[i] != te_ref[jnp.maximum(i - 1, 0)]))
    def _():
        wgu16_ref[...] = wgu_ref[0].astype(BF16)
        wd16_ref[...] = wd_ref[0].astype(BF16)

    @pl.when(i < nu_ref[0])
    def _():
        gu = _dot(x_ref[...].astype(BF16), wgu16_ref[...]) + bgu_ref[0]
        gate = jnp.minimum(gu[:, :d_ff], SWIGLU_LIMIT)
        up = jnp.clip(gu[:, d_ff:], -SWIGLU_LIMIT, SWIGLU_LIMIT)
        h = (up + 1.0) * (gate * _sigmoid(SWIGLU_ALPHA * gate))
        y_ref[...] = _dot(h.astype(BF16), wd16_ref[...]) + bd_ref[0]

    @pl.when(i >= nu_ref[0])
    def _():
        y_ref[...] = jnp.zeros_like(y_ref)


def _ffn(tile_expert, n_used, xs, w_gu, b_gu, w_d, b_d, tm):
    r, d = xs.shape
    d_ff = w_d.shape[1]
    grid_spec = pltpu.PrefetchScalarGridSpec(
        num_scalar_prefetch=2,
        grid=(r // tm,),
        in_specs=[pl.BlockSpec((tm, d), lambda i, te, nu: (i, 0)),
                  pl.BlockSpec((1, d, 2 * d_ff), lambda i, te, nu: (te[i], 0, 0)),
                  pl.BlockSpec((1, 1, 2 * d_ff), lambda i, te, nu: (te[i], 0, 0)),
                  pl.BlockSpec((1, d_ff, d), lambda i, te, nu: (te[i], 0, 0)),
                  pl.BlockSpec((1, 1, d), lambda i, te, nu: (te[i], 0, 0))],
        out_specs=pl.BlockSpec((tm, d), lambda i, te, nu: (i, 0)),
        scratch_shapes=[pltpu.VMEM((d, 2 * d_ff), BF16), pltpu.VMEM((d_ff, d), BF16)],
    )
    return pl.pallas_call(
        _ffn_kernel,
        grid_spec=grid_spec,
        out_shape=jax.ShapeDtypeStruct((r, d), F32),
        compiler_params=_params("arbitrary"),
        name="moe_ffn",
    )(tile_expert, n_used, xs, w_gu, b_gu, w_d, b_d)


def _combine_kernel(pos_ref, y1_ref, gate_ref, gfin_ref, ys_ref, o_ref, buf_ref, sem):
    tm = y1_ref.shape[0]

    def issue(r, _):
        for k in range(TOP_K):
            _row_copy(ys_ref, pos_ref[0, 0, TOP_K * r + k], buf_ref.at[k], r, sem).start()
        return 0

    def drain(r, _):
        for k in range(TOP_K):
            _row_copy(ys_ref, 0, buf_ref.at[k], 0, sem).wait()
        return 0

    lax.fori_loop(0, tm, issue, 0)
    lax.fori_loop(0, tm, drain, 0)

    gate = gate_ref[...]
    moe = gate[:, 0:1] * buf_ref[0]
    for k in range(1, TOP_K):
        moe = moe + gate[:, k:k + 1] * buf_ref[k]
    y = y1_ref[...] + moe
    o_ref[...] = (y * lax.rsqrt(jnp.mean(y * y, axis=-1, keepdims=True) + EPS)) * gfin_ref[...]


def _combine(pos3, y1, rgate, g_final, ys, tm):
    t, d = y1.shape
    return pl.pallas_call(
        _combine_kernel,
        grid=(t // tm,),
        in_specs=[pl.BlockSpec((1, 1, TOP_K * tm), lambda i: (i, 0, 0), memory_space=pltpu.SMEM),
                  pl.BlockSpec((tm, d), lambda i: (i, 0)),
                  pl.BlockSpec((tm, V7X_LANES), lambda i: (i, 0)),
                  pl.BlockSpec((1, d), lambda i: (0, 0)),
                  pl.BlockSpec(memory_space=pl.ANY)],
        out_specs=pl.BlockSpec((tm, d), lambda i: (i, 0)),
        out_shape=jax.ShapeDtypeStruct((t, d), F32),
        scratch_shapes=[pltpu.VMEM((TOP_K, tm, d), F32), pltpu.SemaphoreType.DMA(())],
        compiler_params=_params("arbitrary"),
        name="moe_combine",
    )(pos3, y1, rgate, g_final, ys)


def _rows_to_lanes(a):
    bsz, L, h = a.shape
    return jnp.transpose(a, (0, 2, 1)).reshape(bsz * h, L)


def _mixer(x, hist, fox_cache, ml_state, w, tm, tq, tk, chunk):
    bsz, seq, _ = x.shape
    hist8 = jnp.pad(hist, ((0, 0), (V7X_SUBLANES - (CONV_W - 1), 0), (0, 0)))
    (fq, fk32, fv32, fk16, fv16, small, mq, mk, mv, mo, conv_new) = _project(
        x, hist8, w["g_mix"], w["wbig"], w["wsm"], w["bsm"], w["conv_w"], w["conv_b"], tm)
    flogf = small[..., S_FF:S_MI]
    mi = small[..., S_MI:S_MF]
    mlogf = small[..., S_MF:S_END]

    if fox_cache is None:
        q_offset = 0
        logf_all, k_all, v_all = flogf, fk16, fv16
    else:
        ck_c, cv_c, clogf_c = fox_cache
        q_offset = ck_c.shape[1]
        logf_all = jnp.concatenate([clogf_c, flogf], axis=1)
        k_all = jnp.concatenate([ck_c.reshape(bsz, q_offset, FOX_W).astype(BF16), fk16], axis=1)
        v_all = jnp.concatenate([cv_c.reshape(bsz, q_offset, FOX_W).astype(BF16), fv16], axis=1)
    lk = k_all.shape[1]
    lk_pad = -(-lk // tk) * tk
    logf_rows = jnp.pad(_rows_to_lanes(logf_all), ((0, 0), (0, lk_pad - lk)))
    k_all = jnp.pad(k_all, ((0, 0), (0, lk_pad - lk), (0, 0)))
    v_all = jnp.pad(v_all, ((0, 0), (0, lk_pad - lk), (0, 0)))
    ck = _row_cumsum(logf_rows).reshape(bsz, FOX_HEADS // 2, 2, lk_pad)
    cat_a = _fox(fq, k_all, v_all, ck, w["g_fox"], q_offset, tq, tk)

    b_r, g_r, gmax_r = _ml_gates(_rows_to_lanes(mlogf), _rows_to_lanes(mi), chunk)
    nc = seq // chunk
    rowpack = jnp.transpose(g_r.reshape(bsz * ML_HEADS, nc, chunk), (1, 0, 2))
    colpack = jnp.transpose(jnp.stack([b_r, gmax_r, g_r]).reshape(3, bsz, ML_HEADS, seq), (1, 3, 0, 2))
    colpack = colpack.reshape(bsz, seq, 3 * ML_HEADS)
    c0, n0, m0 = ml_state
    cat_m, c_new, n_new, m_new = _mlstm(mq, mk, mv, mo, colpack, rowpack, c0, n0, m0, w["g_ml"], chunk)

    states = (fk32.reshape(bsz, seq, FOX_HEADS, FOX_HD), fv32.reshape(bsz, seq, FOX_HEADS, FOX_HD), flogf,
              c_new, n_new, m_new, conv_new)
    return cat_a, cat_m, states


def kernel(x_prompt, x_sample, cache_fox_k, cache_fox_v, cache_fox_logf, state_mlstm_C, state_mlstm_n,
           state_mlstm_m, state_mlstm_conv, norm_mix_g, w_in, b_fox_f, conv_w, conv_b, b_ml_i, b_ml_f,
           g_fox, g_ml, w_out, norm_ffn_g, w_router, b_router, w_gate_up, b_gate_up, w_down, b_down,
           norm_final_g):
    depth = w_in.shape[0]
    bp, sp, d = x_prompt.shape
    bs, ss, _ = x_sample.shape
    n_exp = w_router.shape[-1]
    d_ff = w_down.shape[2]
    yp, ys = x_prompt, x_sample
    p_st, s_st = [], []

    for l in range(depth):
        wl = w_in[l]
        w = {
            "g_mix": norm_mix_g[l][None, :],
            "wbig": jnp.concatenate([wl[:, O_FQ:O_FF], wl[:, O_MQ:O_MI]], axis=1).astype(BF16),
            "wsm": jnp.pad(jnp.concatenate([wl[:, O_FF:O_MQ], wl[:, O_MI:P_IN]], axis=1),
                           ((0, 0), (0, V7X_LANES - S_END))).astype(BF16),
            "bsm": jnp.pad(jnp.concatenate([b_fox_f[l], b_ml_i[l], b_ml_f[l]]), (0, V7X_LANES - S_END))[None, :],
            "conv_w": conv_w[l],
            "conv_b": conv_b[l][None, :],
            "g_fox": g_fox[l][None, :],
            "g_ml": g_ml[l][None, :],
        }
        w_a = w_out[l][:FOX_W].astype(BF16)
        w_m = w_out[l][FOX_W:].astype(BF16)
        g_ffn = norm_ffn_g[l][None, :]
        wr = jnp.pad(w_router[l], ((0, 0), (0, V7X_LANES - n_exp)))
        wr_hi = wr.astype(BF16)
        wr_lo = (wr - wr_hi.astype(F32)).astype(BF16)
        br = jnp.pad(b_router[l], (0, V7X_LANES - n_exp), constant_values=NEG)[None, :]

        zeros_state = (jnp.zeros((bp, ML_HEADS, ML_DV, ML_DK), F32), jnp.zeros((bp, ML_HEADS, ML_DK), F32),
                       jnp.zeros((bp, ML_HEADS), F32))
        cat_a_p, cat_m_p, st_p = _mixer(yp, jnp.zeros((bp, CONV_W - 1, QK_W), F32), None, zeros_state, w,
                                        min(PROJ_ROWS, sp), min(FOX_BLOCK, sp), min(FOX_BLOCK, sp), CHUNK)
        p_st.append(st_p)
        past = cache_fox_k.shape[2]
        tk_s = -(-(past + ss) // V7X_LANES) * V7X_LANES
        cat_a_s, cat_m_s, st_s = _mixer(
            ys, state_mlstm_conv[l], (cache_fox_k[l], cache_fox_v[l], cache_fox_logf[l]),
            (state_mlstm_C[l].astype(F32), state_mlstm_n[l].astype(F32), state_mlstm_m[l].astype(F32)),
            w, ss, ss, tk_s, ss)
        s_st.append(st_s)

        tp, ts = bp * sp, bs * ss
        y1_p, xn_p, ridx_p, rgate_p, cnt_p = _merge(
            cat_a_p.reshape(tp, FOX_W), cat_m_p.reshape(tp, ML_W), yp.reshape(tp, d), w_a, w_m, g_ffn,
            wr_hi, wr_lo, br, min(MERGE_ROWS, tp))
        y1_s, xn_s, ridx_s, rgate_s, cnt_s = _merge(
            cat_a_s.reshape(ts, FOX_W), cat_m_s.reshape(ts, ML_W), ys.reshape(ts, d), w_a, w_m, g_ffn,
            wr_hi, wr_lo, br, min(MERGE_ROWS, ts))

        cnt_p = cnt_p[0, :n_exp].astype(jnp.int32)
        cnt_s = cnt_s[0, :n_exp].astype(jnp.int32)
        seg_rows = -(-(cnt_p + cnt_s) // FFN_ROWS) * FFN_ROWS
        seg_end = jnp.cumsum(seg_rows)
        seg_start = seg_end - seg_rows
        n_rows = -(-((tp + ts) * TOP_K + n_exp * (FFN_ROWS - 1)) // FFN_ROWS) * FFN_ROWS
        n_tiles = n_rows // FFN_ROWS
        tile_expert = jnp.minimum(
            jnp.searchsorted(seg_end, jnp.arange(n_tiles, dtype=jnp.int32) * FFN_ROWS, side="right"),
            n_exp - 1).astype(jnp.int32)
        n_used = (seg_end[-1:] // FFN_ROWS).astype(jnp.int32)
        e_p, rank_p = ridx_p[:, :TOP_K], ridx_p[:, TOP_K:2 * TOP_K]
        e_s, rank_s = ridx_s[:, :TOP_K], ridx_s[:, TOP_K:2 * TOP_K]
        pos_p = seg_start[e_p] + rank_p
        pos_s = seg_start[e_s] + cnt_p[e_s] + rank_s
        tm_p, tm_s = min(MOE_ROWS, tp), min(MOE_ROWS, ts)
        pos3_p = pos_p.reshape(tp // tm_p, 1, TOP_K * tm_p)
        pos3_s = pos_s.reshape(ts // tm_s, 1, TOP_K * tm_s)

        xs = jnp.zeros((n_rows, d), F32)
        xs = _scatter(pos3_p, xn_p, xs, tm_p)
        xs = _scatter(pos3_s, xn_s, xs, tm_s)
        ysort = _ffn(tile_expert, n_used, xs, w_gate_up[l], b_gate_up[l].reshape(n_exp, 1, 2 * d_ff),
                     w_down[l], b_down[l].reshape(n_exp, 1, d), FFN_ROWS)

        last = l == depth - 1
        g_fin = norm_final_g[None, :] if last else None
        assert last, "per-layer combine without the final norm is not needed for depth 1"
        yp = _combine(pos3_p, y1_p, rgate_p, g_fin, ysort, tm_p).reshape(bp, sp, d)
        ys = _combine(pos3_s, y1_s, rgate_s, g_fin, ysort, tm_s).reshape(bs, ss, d)

    p_out = tuple(jnp.stack(a) for a in zip(*p_st))
    s_out = tuple(jnp.stack(a) for a in zip(*s_st))
    return (yp, ys) + p_out + s_out
```

```python
import functools
import math

import numpy as np

import jax
import jax.numpy as jnp
from jax import lax
from jax.experimental import pallas as pl
from jax.experimental.pallas import tpu as pltpu

F32 = jnp.float32
BF16 = jnp.bfloat16

FOX_HEADS = 8
FOX_HD = 64
FOX_W = FOX_HEADS * FOX_HD
ML_HEADS = 4
ML_DK = 128
ML_DV = 128
ML_W = ML_HEADS * ML_DV
QK_W = 2 * ML_HEADS * ML_DK
CONV_W = 4
CHUNK = 64
TOP_K = 4
SWIGLU_LIMIT = 7.0
SWIGLU_ALPHA = 1.702
EPS = 1e-6
NEG = -1e30
LOG2E = math.log2(math.e)

O_FQ = 0
O_FK = O_FQ + FOX_W
O_FV = O_FK + FOX_W
O_FF = O_FV + FOX_W
O_MQ = O_FF + FOX_HEADS
O_MK = O_MQ + ML_HEADS * ML_DK
O_MV = O_MK + ML_HEADS * ML_DK
O_MO = O_MV + ML_W
O_MI = O_MO + ML_W
O_MF = O_MI + ML_HEADS
P_IN = O_MF + ML_HEADS

B_FQ, B_FK, B_FV, B_QK, B_MV, B_MO, B_END = 0, 512, 1024, 1536, 2560, 3072, 3584
S_FF, S_MI, S_MF, S_END = 0, 8, 12, 16
BIAS_TERMS = 3

V7X_LANES = 128
V7X_SUBLANES = 8
V7X_VMEM_LIMIT_BYTES = 56 * 1024 * 1024

PROJ_ROWS = 512
FOX_BLOCK = 512
FOX_QBLOCK = 1024
FOX_SUB = 512
MERGE_ROWS = 512
MOE_ROWS = 512
FFN_ROWS = 512


def _params(*semantics):
    return pltpu.CompilerParams(dimension_semantics=semantics, vmem_limit_bytes=V7X_VMEM_LIMIT_BYTES)


def _dot(a, b):
    return jnp.dot(a, b, preferred_element_type=F32)


def _dot_nt(a, b):
    return lax.dot_general(a, b, (((1,), (1,)), ((), ())), preferred_element_type=F32)


def _dot_tn(a, b):
    return lax.dot_general(a, b, (((0,), (0,)), ((), ())), preferred_element_type=F32)


def _sigmoid(x):
    return 1.0 / (1.0 + jnp.exp(-x))


def _log_sigmoid(x):
    return jnp.minimum(x, 0.0) - jnp.log1p(jnp.exp(-jnp.abs(x)))


def _split3(x):
    t1 = x.astype(BF16)
    r1 = x - t1.astype(F32)
    t2 = r1.astype(BF16)
    t3 = (r1 - t2.astype(F32)).astype(BF16)
    return t1, t2, t3


def _lower_tri(n):
    r = lax.broadcasted_iota(jnp.int32, (n, n), 0)
    c = lax.broadcasted_iota(jnp.int32, (n, n), 1)
    return c <= r


def _proj_kernel(x_ref, g_ref, wbig_ref, wsm_ref, bsm_ref, cw_ref, cb_ref, hist_ref,
                 fq_ref, fk32_ref, fv32_ref, fk16_ref, fv16_ref, sm_ref, flogf_ref, mq_ref, mk_ref, mv_ref,
                 mo_ref, cnew_ref, halo_ref):
    tm = x_ref.shape[1]

    @pl.when(pl.program_id(1) == 0)
    def _():
        halo_ref[...] = hist_ref[0]

    x = x_ref[0]
    xn = (x * lax.rsqrt(jnp.mean(x * x, axis=-1, keepdims=True) + EPS)) * g_ref[...]
    xb = xn.astype(BF16)
    z = _dot(xb, wbig_ref[...])

    fq_ref[0] = (z[:, B_FQ:B_FK] * (FOX_HD ** -0.5 * LOG2E)).astype(BF16)
    fk = z[:, B_FK:B_FV]
    fv = z[:, B_FV:B_QK]
    fk32_ref[0] = fk
    fv32_ref[0] = fv
    fk16_ref[0] = fk.astype(BF16)
    fv16_ref[0] = fv.astype(BF16)
    mv_ref[0] = z[:, B_MV:B_MO].astype(BF16)
    mo_ref[0] = _sigmoid(z[:, B_MO:B_END]).astype(BF16)

    zs = _dot(xb, wsm_ref[...]) + bsm_ref[...]
    lane = lax.broadcasted_iota(jnp.int32, zs.shape, 1)
    is_forget = (lane < S_MI) | ((lane >= S_MF) & (lane < S_END))
    sm = jnp.where(is_forget, _log_sigmoid(zs), zs)
    sm_ref[0] = sm
    flogf_ref[0] = sm[:, S_FF:S_MI]

    u = z[:, B_QK:B_MV]
    up = jnp.concatenate([halo_ref[...], u], axis=0)
    n = tm + V7X_SUBLANES
    first = V7X_SUBLANES - (CONV_W - 1)
    y = cb_ref[...] + cw_ref[0:1, :] * pltpu.roll(up, n - first, axis=0)[:tm]
    for j in range(1, CONV_W - 1):
        y = y + cw_ref[j:j + 1, :] * pltpu.roll(up, n - (first + j), axis=0)[:tm]
    y = y + cw_ref[CONV_W - 1:CONV_W, :] * u
    qk = y * _sigmoid(y)
    mq_ref[0] = qk[:, :QK_W // 2].astype(BF16)
    mk_ref[0] = (qk[:, QK_W // 2:] * (ML_DK ** -0.5)).astype(BF16)

    halo_ref[...] = u[tm - V7X_SUBLANES:, :]
    cnew_ref[0] = halo_ref[first:, :]


def _project(x, hist8, g, wbig, wsm, bsm, cw, cb, tm):
    bsz, seq, d = x.shape
    grid = (bsz, seq // tm)
    row = lambda c: pl.BlockSpec((1, tm, c), lambda b, s: (b, s, 0))
    const = lambda shape: pl.BlockSpec(shape, lambda b, s: (0,) * len(shape))
    outs = [
        (FOX_W, BF16), (FOX_W, F32), (FOX_W, F32), (FOX_W, BF16), (FOX_W, BF16), (V7X_LANES, F32),
        (FOX_HEADS, F32), (ML_W, BF16), (ML_W, BF16), (ML_W, BF16), (ML_W, BF16),
    ]
    out_shape = [jax.ShapeDtypeStruct((bsz, seq, c), dt) for c, dt in outs]
    out_specs = [row(c) for c, _ in outs]
    out_shape.append(jax.ShapeDtypeStruct((bsz, CONV_W - 1, QK_W), F32))
    out_specs.append(pl.BlockSpec((1, CONV_W - 1, QK_W), lambda b, s: (b, 0, 0)))
    return pl.pallas_call(
        _proj_kernel,
        grid=grid,
        in_specs=[row(d), const((1, d)), const(wbig.shape), const(wsm.shape), const((1, V7X_LANES)),
                  const((CONV_W, QK_W)), const((1, QK_W)),
                  pl.BlockSpec((1, V7X_SUBLANES, QK_W), lambda b, s: (b, 0, 0))],
        out_specs=out_specs,
        out_shape=out_shape,
        scratch_shapes=[pltpu.VMEM((V7X_SUBLANES, QK_W), F32)],
        compiler_params=_params("arbitrary", "arbitrary"),
        name="proj",
    )(x, g, wbig, wsm, bsm, cw, cb, hist8)


def _bias_lane(head, term):
    return (head // 2) * V7X_LANES + (FOX_HD if head % 2 == 0 else 0) + term


def _bias_placement():
    place = np.zeros((BIAS_TERMS, V7X_LANES, FOX_W), np.float32)
    for h in range(FOX_HEADS):
        for t in range(BIAS_TERMS):
            place[t, S_FF + h, _bias_lane(h, t)] = 1.0
    return jnp.asarray(place, BF16)


def _fox_prep_kernel(lf_ref, k_ref, v_ref, place_ref, ka0_ref, ka1_ref, va0_ref, va1_ref, carry_ref):
    tm = lf_ref.shape[1]

    @pl.when(pl.program_id(1) == 0)
    def _():
        carry_ref[...] = jnp.zeros_like(carry_ref)

    tri = jnp.where(_lower_tri(tm), 1.0, 0.0).astype(BF16)
    t1, t2, t3 = _split3(lf_ref[0])
    c = carry_ref[...] + (_dot(tri, t1) + (_dot(tri, t2) + _dot(tri, t3)))
    carry_ref[...] = c[tm - 1:tm, :]
    b1, b2, b3 = _split3(c * (-LOG2E))
    kb = _dot(b1, place_ref[0]) + (_dot(b2, place_ref[1]) + _dot(b3, place_ref[2]))

    k = k_ref[0].astype(F32)
    v = v_ref[0].astype(F32)
    first_head = lax.broadcasted_iota(jnp.int32, k.shape, 1) % V7X_LANES < FOX_HD
    ka0_ref[0] = jnp.where(first_head, k, kb).astype(BF16)
    ka1_ref[0] = jnp.where(first_head, kb, k).astype(BF16)
    va0_ref[0] = jnp.where(first_head, v, 1.0).astype(BF16)
    va1_ref[0] = jnp.where(first_head, 1.0, v).astype(BF16)


def _fox_prep(lf, k, v, tm):
    bsz, lk, _ = lf.shape
    place = _bias_placement()
    wide = pl.BlockSpec((1, tm, FOX_W), lambda b, s: (b, s, 0))
    sds = jax.ShapeDtypeStruct((bsz, lk, FOX_W), BF16)
    return pl.pallas_call(
        _fox_prep_kernel,
        grid=(bsz, lk // tm),
        in_specs=[pl.BlockSpec((1, tm, V7X_LANES), lambda b, s: (b, s, 0)), wide, wide,
                  pl.BlockSpec(place.shape, lambda b, s: (0, 0, 0))],
        out_specs=[wide, wide, wide, wide],
        out_shape=[sds, sds, sds, sds],
        scratch_shapes=[pltpu.VMEM((1, V7X_LANES), F32)],
        compiler_params=_params("arbitrary", "arbitrary"),
        name="fox_prep",
    )(lf, k, v, place)


def _fox_kernel(q_ref, ka0_ref, ka1_ref, va0_ref, va1_ref, g_ref, o_ref, *, tq, tk, sub, q_offset):
    ts = min(tq, sub)
    n_sub = tq // ts
    diag_blocks = max(ts // tk, 1)
    k_refs = (ka0_ref, ka1_ref)
    v_refs = (va0_ref, va1_ref)
    lane = lax.broadcasted_iota(jnp.int32, (ts, V7X_LANES), 1)
    first_head = lane < FOX_HD
    q_lo = q_offset + pl.program_id(2) * tq
    n_common = (q_lo + 1) // tk

    def chain_q(r, hh):
        q = q_ref[0, r * ts:(r + 1) * ts, :].astype(F32)
        if hh == 0:
            return jnp.where(first_head, q, jnp.where(lane < FOX_HD + BIAS_TERMS, 1.0, 0.0)).astype(BF16)
        return jnp.where(first_head, jnp.where(lane < BIAS_TERMS, 1.0, 0.0), q).astype(BF16)

    chains = [(r, hh) for r in range(n_sub) for hh in range(2)]
    qs = [chain_q(r, hh) for r, hh in chains]

    def step(state, c, j, masked):
        r, hh = chains[c]
        m, acc = state
        start = pl.multiple_of(j * tk, tk)
        s = _dot_nt(qs[c], k_refs[hh][0, pl.ds(start, tk), :])
        if masked:
            qpos = q_lo + r * ts + lax.broadcasted_iota(jnp.int32, s.shape, 0)
            kpos = start + lax.broadcasted_iota(jnp.int32, s.shape, 1)
            s = jnp.where(kpos <= qpos, s, NEG)
        m_new = jnp.maximum(m, jnp.max(s, axis=-1, keepdims=True))
        p = jnp.exp2(s - m_new).astype(BF16)
        acc = jnp.exp2(m - m_new) * acc + _dot(p, v_refs[hh][0, pl.ds(start, tk), :])
        return m_new, acc

    def common(j, states):
        return tuple(step(st, c, j, False) for c, st in enumerate(states))

    def common_pair(j2, states):
        return common(2 * j2 + 1, common(2 * j2, states))

    init = (jnp.full((ts, 1), NEG, F32), jnp.zeros((ts, V7X_LANES), F32))
    states = lax.fori_loop(0, n_common // 2, common_pair, (init,) * len(chains))
    states = list(lax.fori_loop(2 * (n_common // 2), n_common, common, states))
    for c, (r, _) in enumerate(chains):
        for e in range(r * diag_blocks):
            states[c] = step(states[c], c, n_common + e, False)
        for e in range(diag_blocks):
            states[c] = step(states[c], c, n_common + r * diag_blocks + e, True)

    for r in range(n_sub):
        out = jnp.zeros((ts, V7X_LANES), F32)
        for hh in range(2):
            acc = states[chains.index((r, hh))][1]
            own = first_head if hh == 0 else jnp.logical_not(first_head)
            denom_lane = FOX_HD if hh == 0 else 0
            o = jnp.where(own, acc / acc[:, denom_lane:denom_lane + 1], 0.0)
            out = out + o * lax.rsqrt(jnp.sum(o * o, axis=-1, keepdims=True) * (1.0 / FOX_HD) + EPS)
        o_ref[0, r * ts:(r + 1) * ts, :] = (out * g_ref[...]).astype(BF16)


def _fox(q, ka0, ka1, va0, va1, g_fox, q_offset, tq, tk):
    bsz, lq, _ = q.shape
    lk = ka0.shape[1]
    assert (q_offset % tk == 0 and tq % tk == 0) or (lq == tq <= tk and lk == tk), (q_offset, tq, tk, lq, lk)
    pairs = FOX_HEADS // 2
    grid = (bsz, pairs, lq // tq)
    kv_spec = pl.BlockSpec((1, lk, V7X_LANES), lambda b, p, i: (b, 0, p))
    q_spec = pl.BlockSpec((1, tq, V7X_LANES), lambda b, p, i: (b, i, p))
    return pl.pallas_call(
        functools.partial(_fox_kernel, tq=tq, tk=tk, sub=FOX_SUB, q_offset=q_offset),
        grid=grid,
        in_specs=[q_spec, kv_spec, kv_spec, kv_spec, kv_spec,
                  pl.BlockSpec((1, V7X_LANES), lambda b, p, i: (0, p))],
        out_specs=q_spec,
        out_shape=jax.ShapeDtypeStruct((bsz, lq, FOX_W), BF16),
        compiler_params=_params("arbitrary", "arbitrary", "arbitrary"),
        name="fox_attention",
    )(q, ka0, ka1, va0, va1, g_fox)


def _mlstm_kernel(q_ref, k_ref, v_ref, mo_ref, sm_ref, c0_ref, n0_ref, m0_ref, g_ref,
                  o_ref, c_ref, n_ref, m_ref):
    bsz, L, _ = q_ref.shape

    @pl.when(pl.program_id(0) == 0)
    def _():
        c_ref[...] = c0_ref[...]
        n_ref[...] = n0_ref[...]
        m_ref[...] = m0_ref[...]

    causal = _lower_tri(L)
    tri = jnp.where(causal, 1.0, 0.0).astype(BF16)
    eye = jnp.where(lax.broadcasted_iota(jnp.int32, (L, L), 0) == lax.broadcasted_iota(jnp.int32, (L, L), 1),
                    1.0, 0.0).astype(BF16)
    rows = lax.broadcasted_iota(jnp.int32, (L, V7X_LANES), 0)
    lanes = lax.broadcasted_iota(jnp.int32, (L, V7X_LANES), 1)
    gate_lanes = (lanes >= S_MI) & (lanes < S_END)

    def per_batch(b, _):
        sm = jnp.where(gate_lanes, sm_ref[b], 0.0)
        t1, t2, t3 = _split3(sm)
        bcum = _dot(tri, t1) + (_dot(tri, t2) + _dot(tri, t3))
        bcum = pltpu.roll(bcum, V7X_LANES - (S_MF - S_MI), axis=1)
        g = sm - bcum
        gmax = g
        sh = 1
        while sh < L:
            gmax = jnp.maximum(gmax, jnp.where(rows >= sh, pltpu.roll(gmax, sh, axis=0), -jnp.inf))
            sh *= 2
        m = m_ref[pl.ds(b, 1), :]
        u = jnp.maximum(m, gmax)
        a = jnp.exp(m - u)
        mt = bcum + u
        em = jnp.exp(-mt)
        u_last = u[L - 1:L, :]
        a_last = a[L - 1:L, :]
        wcol = jnp.exp(g - u_last)
        m_ref[pl.ds(b, 1), :] = mt[L - 1:L, :]
        g1, g2, g3 = _split3(g)
        g_rows = _dot_tn(g1, eye) + (_dot_tn(g2, eye) + _dot_tn(g3, eye))

        for h in range(ML_HEADS):
            cs = slice(h * ML_DK, (h + 1) * ML_DK)
            gl = S_MI + h
            qh = q_ref[b, :, cs]
            kh = k_ref[b, :, cs]
            vh = v_ref[b, :, cs]
            decay = jnp.exp(jnp.where(causal, g_rows[gl:gl + 1, :] - u[:, gl:gl + 1], NEG))
            s = _dot_nt(qh, kh) * decay
            c_old = c_ref[b, h]
            n_old = n_ref[b, h:h + 1, :]
            a_h = a[:, gl:gl + 1]
            num = a_h * _dot_nt(qh, c_old.astype(BF16)) + _dot(s.astype(BF16), vh)
            den = (a_h * jnp.sum(qh.astype(F32) * n_old, axis=-1, keepdims=True)
                   + jnp.sum(s, axis=-1, keepdims=True))
            hv = num / jnp.maximum(jnp.abs(den), em[:, gl:gl + 1])
            hn = hv * lax.rsqrt(jnp.mean(hv * hv, axis=-1, keepdims=True) + EPS)
            o_ref[b, :, cs] = (hn * g_ref[:, cs] * mo_ref[b, :, cs].astype(F32)).astype(BF16)

            w_h = wcol[:, gl:gl + 1]
            vw = (vh.astype(F32) * w_h).astype(BF16)
            c_ref[b, h] = a_last[:, gl:gl + 1] * c_old + _dot_tn(vw, kh)
            n_ref[b, h:h + 1, :] = (a_last[:, gl:gl + 1] * n_old
                                    + jnp.sum(kh.astype(F32) * w_h, axis=0, keepdims=True))
        return 0

    lax.fori_loop(0, bsz, per_batch, 0)


def _mlstm(mq, mk, mv, mo, small, c0, n0, m0, g_ml, L):
    bsz, seq, _ = mq.shape
    chunk = pl.BlockSpec((bsz, L, ML_W), lambda c: (0, c, 0))
    full = lambda shape: pl.BlockSpec(shape, lambda c: (0,) * len(shape))
    return pl.pallas_call(
        _mlstm_kernel,
        grid=(seq // L,),
        in_specs=[chunk, chunk, chunk, chunk,
                  pl.BlockSpec((bsz, L, V7X_LANES), lambda c: (0, c, 0)),
                  full(c0.shape), full(n0.shape), full(m0.shape), full((1, ML_W))],
        out_specs=[chunk, full(c0.shape), full(n0.shape), full(m0.shape)],
        out_shape=[jax.ShapeDtypeStruct((bsz, seq, ML_W), BF16),
                   jax.ShapeDtypeStruct(c0.shape, F32),
                   jax.ShapeDtypeStruct(n0.shape, F32),
                   jax.ShapeDtypeStruct(m0.shape, F32)],
        compiler_params=_params("arbitrary"),
        name="mlstm",
    )(mq, mk, mv, mo, small, c0, n0, m0, g_ml)


def _merge_kernel(ca_ref, cm_ref, x_ref, wa_ref, wm_ref, g_ref, wrh_ref, wrl_ref, br_ref,
                  y1_ref, xn_ref, ridx_ref, rgate_ref, cnt_ref):
    tm = x_ref.shape[0]

    @pl.when(pl.program_id(0) == 0)
    def _():
        cnt_ref[...] = jnp.zeros_like(cnt_ref)

    y1 = x_ref[...] + (_dot(ca_ref[...], wa_ref[...]) + _dot(cm_ref[...], wm_ref[...]))
    y1_ref[...] = y1
    xn = (y1 * lax.rsqrt(jnp.mean(y1 * y1, axis=-1, keepdims=True) + EPS)) * g_ref[...]
    xn_ref[...] = xn

    xh = xn.astype(BF16)
    xl = (xn - xh.astype(F32)).astype(BF16)
    logits = (_dot(xh, wrh_ref[...]) + (_dot(xl, wrh_ref[...]) + _dot(xh, wrl_ref[...]))) + br_ref[...]

    lane = lax.broadcasted_iota(jnp.int32, logits.shape, 1)
    vals, sel = [], []
    work = logits
    for _ in range(TOP_K):
        mx = jnp.max(work, axis=-1, keepdims=True)
        idx = jnp.min(jnp.where(work == mx, lane, V7X_LANES), axis=-1, keepdims=True)
        vals.append(mx)
        sel.append(idx)
        work = jnp.where(lane == idx, -jnp.inf, work)
    ex = [jnp.exp(v - vals[0]) for v in vals]
    tot = ex[0] + ex[1] + ex[2] + ex[3]

    onehot = [(lane == idx) for idx in sel]
    picked = jnp.where(onehot[0] | onehot[1] | onehot[2] | onehot[3], 1.0, 0.0)
    earlier = (lax.broadcasted_iota(jnp.int32, (tm, tm), 1) < lax.broadcasted_iota(jnp.int32, (tm, tm), 0))
    before = _dot(jnp.where(earlier, 1.0, 0.0).astype(BF16), picked.astype(BF16)) + cnt_ref[...]
    cnt_ref[...] = cnt_ref[...] + jnp.sum(picked, axis=0, keepdims=True)

    ridx = jnp.zeros(logits.shape, jnp.int32)
    rgate = jnp.zeros(logits.shape, F32)
    for k in range(TOP_K):
        rank = jnp.sum(jnp.where(onehot[k], before, 0.0), axis=-1, keepdims=True).astype(jnp.int32)
        ridx = jnp.where(lane == k, sel[k], ridx)
        ridx = jnp.where(lane == TOP_K + k, rank, ridx)
        rgate = jnp.where(lane == k, ex[k] / tot, rgate)
    ridx_ref[...] = ridx
    rgate_ref[...] = rgate


def _merge(cat_a, cat_m, x, w_a, w_m, g_ffn, wr_hi, wr_lo, br, tm):
    t, d = x.shape
    row = lambda c: pl.BlockSpec((tm, c), lambda i: (i, 0))
    const = lambda shape: pl.BlockSpec(shape, lambda i: (0,) * len(shape))
    return pl.pallas_call(
        _merge_kernel,
        grid=(t // tm,),
        in_specs=[row(FOX_W), row(ML_W), row(d), const(w_a.shape), const(w_m.shape), const((1, d)),
                  const(wr_hi.shape), const(wr_lo.shape), const((1, V7X_LANES))],
        out_specs=[row(d), row(d), row(V7X_LANES), row(V7X_LANES), const((1, V7X_LANES))],
        out_shape=[jax.ShapeDtypeStruct((t, d), F32), jax.ShapeDtypeStruct((t, d), F32),
                   jax.ShapeDtypeStruct((t, V7X_LANES), jnp.int32),
                   jax.ShapeDtypeStruct((t, V7X_LANES), F32),
                   jax.ShapeDtypeStruct((1, V7X_LANES), F32)],
        compiler_params=_params("arbitrary"),
        name="merge_router",
    )(cat_a, cat_m, x, w_a, w_m, g_ffn, wr_hi, wr_lo, br)


def _row_copy(src_ref, src_row, dst_ref, dst_row, sem):
    return pltpu.make_async_copy(src_ref.at[pl.ds(src_row, 1)], dst_ref.at[pl.ds(dst_row, 1)], sem)


def _scatter_kernel(pos_ref, x_ref, xs_in_ref, xs_ref, sem):
    del xs_in_ref
    tm = x_ref.shape[0]

    def issue(r, _):
        for k in range(TOP_K):
            _row_copy(x_ref, r, xs_ref, pos_ref[0, 0, TOP_K * r + k], sem).start(priority=k % 2)
        return 0

    def drain(r, _):
        for k in range(TOP_K):
            _row_copy(x_ref, 0, xs_ref, 0, sem).wait()
        return 0

    lax.fori_loop(0, tm, issue, 0)
    lax.fori_loop(0, tm, drain, 0)


def _scatter(pos3, x, xs, tm):
    t, d = x.shape
    return pl.pallas_call(
        _scatter_kernel,
        grid=(t // tm,),
        in_specs=[pl.BlockSpec((1, 1, TOP_K * tm), lambda i: (i, 0, 0), memory_space=pltpu.SMEM),
                  pl.BlockSpec((tm, d), lambda i: (i, 0)),
                  pl.BlockSpec(memory_space=pl.ANY)],
        out_specs=pl.BlockSpec(memory_space=pl.ANY),
        out_shape=jax.ShapeDtypeStruct(xs.shape, xs.dtype),
        scratch_shapes=[pltpu.SemaphoreType.DMA(())],
        input_output_aliases={2: 0},
        compiler_params=_params("arbitrary"),
        name="moe_scatter",
    )(pos3, x, xs)


def _ffn_kernel(te_ref, nu_ref, x_ref, wgu_ref, bgu_ref, wd_ref, bd_ref, y_ref, wgu16_ref, wd16_ref):
    i = pl.program_id(0)
    d_ff = wd_ref.shape[1]

    @pl.when((i == 0) | (te_ref[i] != te_ref[jnp.maximum(i - 1, 0)]))
    def _():
        wgu16_ref[...] = wgu_ref[0].astype(BF16)
        wd16_ref[...] = wd_ref[0].astype(BF16)

    @pl.when(i < nu_ref[0])
    def _():
        gu = _dot(x_ref[...].astype(BF16), wgu16_ref[...]) + bgu_ref[0]
        gate = jnp.minimum(gu[:, :d_ff], SWIGLU_LIMIT)
        up = jnp.clip(gu[:, d_ff:], -SWIGLU_LIMIT, SWIGLU_LIMIT)
        h = (up + 1.0) * (gate * _sigmoid(SWIGLU_ALPHA * gate))
        y_ref[...] = _dot(h.astype(BF16), wd16_ref[...]) + bd_ref[0]

    @pl.when(i >= nu_ref[0])
    def _():
        y_ref[...] = jnp.zeros_like(y_ref)


def _ffn(tile_expert, n_used, xs, w_gu, b_gu, w_d, b_d, tm):
    r, d = xs.shape
    d_ff = w_d.shape[1]
    grid_spec = pltpu.PrefetchScalarGridSpec(
        num_scalar_prefetch=2,
        grid=(r // tm,),
        in_specs=[pl.BlockSpec((tm, d), lambda i, te, nu: (i, 0)),
                  pl.BlockSpec((1, d, 2 * d_ff), lambda i, te, nu: (te[i], 0, 0)),
                  pl.BlockSpec((1, 1, 2 * d_ff), lambda i, te, nu: (te[i], 0, 0)),
                  pl.BlockSpec((1, d_ff, d), lambda i, te, nu: (te[i], 0, 0)),
                  pl.BlockSpec((1, 1, d), lambda i, te, nu: (te[i], 0, 0))],
        out_specs=pl.BlockSpec((tm, d), lambda i, te, nu: (i, 0)),
        scratch_shapes=[pltpu.VMEM((d, 2 * d_ff), BF16), pltpu.VMEM((d_ff, d), BF16)],
    )
    return pl.pallas_call(
        _ffn_kernel,
        grid_spec=grid_spec,
        out_shape=jax.ShapeDtypeStruct((r, d), F32),
        compiler_params=_params("arbitrary"),
        name="moe_ffn",
    )(tile_expert, n_used, xs, w_gu, b_gu, w_d, b_d)


def _combine_kernel(pos_ref, y1_ref, gate_ref, gfin_ref, ys_ref, o_ref, buf_ref, sem):
    tm = y1_ref.shape[0]

    def issue(r, _):
        for k in range(TOP_K):
            _row_copy(ys_ref, pos_ref[0, 0, TOP_K * r + k], buf_ref.at[k], r, sem).start(priority=k % 2)
        return 0

    def drain(r, _):
        for k in range(TOP_K):
            _row_copy(ys_ref, 0, buf_ref.at[k], 0, sem).wait()
        return 0

    lax.fori_loop(0, tm, issue, 0)
    lax.fori_loop(0, tm, drain, 0)

    gate = gate_ref[...]
    moe = gate[:, 0:1] * buf_ref[0]
    for k in range(1, TOP_K):
        moe = moe + gate[:, k:k + 1] * buf_ref[k]
    y = y1_ref[...] + moe
    o_ref[...] = (y * lax.rsqrt(jnp.mean(y * y, axis=-1, keepdims=True) + EPS)) * gfin_ref[...]


def _combine(pos3, y1, rgate, g_final, ys, tm):
    t, d = y1.shape
    return pl.pallas_call(
        _combine_kernel,
        grid=(t // tm,),
        in_specs=[pl.BlockSpec((1, 1, TOP_K * tm), lambda i: (i, 0, 0), memory_space=pltpu.SMEM),
                  pl.BlockSpec((tm, d), lambda i: (i, 0)),
                  pl.BlockSpec((tm, V7X_LANES), lambda i: (i, 0)),
                  pl.BlockSpec((1, d), lambda i: (0, 0)),
                  pl.BlockSpec(memory_space=pl.ANY)],
        out_specs=pl.BlockSpec((tm, d), lambda i: (i, 0)),
        out_shape=jax.ShapeDtypeStruct((t, d), F32),
        scratch_shapes=[pltpu.VMEM((TOP_K, tm, d), F32), pltpu.SemaphoreType.DMA(())],
        compiler_params=_params("arbitrary"),
        name="moe_combine",
    )(pos3, y1, rgate, g_final, ys)


def _mixer(x, hist, fox_cache, ml_state, w, tm, tq, tk, chunk):
    bsz, seq, _ = x.shape
    hist8 = jnp.pad(hist, ((0, 0), (V7X_SUBLANES - (CONV_W - 1), 0), (0, 0)))
    (fq, fk32, fv32, fk16, fv16, small, flogf, mq, mk, mv, mo, conv_new) = _project(
        x, hist8, w["g_mix"], w["wbig"], w["wsm"], w["bsm"], w["conv_w"], w["conv_b"], tm)

    if fox_cache is None:
        q_offset = 0
        lf_all, k_all, v_all = small, fk16, fv16
    else:
        ck_c, cv_c, clogf_c = fox_cache
        q_offset = ck_c.shape[1]
        clogf_c = jnp.pad(clogf_c, ((0, 0), (0, 0), (S_FF, V7X_LANES - S_FF - FOX_HEADS)))
        lf_all = jnp.concatenate([clogf_c, small], axis=1)
        k_all = jnp.concatenate([ck_c.reshape(bsz, q_offset, FOX_W).astype(BF16), fk16], axis=1)
        v_all = jnp.concatenate([cv_c.reshape(bsz, q_offset, FOX_W).astype(BF16), fv16], axis=1)
    lk = k_all.shape[1]
    lk_pad = -(-lk // tk) * tk
    pad_rows = ((0, 0), (0, lk_pad - lk), (0, 0))
    lf_all, k_all, v_all = jnp.pad(lf_all, pad_rows), jnp.pad(k_all, pad_rows), jnp.pad(v_all, pad_rows)
    ka0, ka1, va0, va1 = _fox_prep(lf_all, k_all, v_all, math.gcd(tk, FOX_BLOCK))
    cat_a = _fox(fq, ka0, ka1, va0, va1, w["g_fox"], q_offset, tq, tk)

    c0, n0, m0 = ml_state
    m0 = jnp.pad(m0, ((0, 0), (S_MI, V7X_LANES - S_MI - ML_HEADS)))
    cat_m, c_new, n_new, m_new = _mlstm(mq, mk, mv, mo, small, c0, n0, m0, w["g_ml"], chunk)
    m_new = m_new[:, S_MI:S_MI + ML_HEADS]

    states = (fk32.reshape(bsz, seq, FOX_HEADS, FOX_HD), fv32.reshape(bsz, seq, FOX_HEADS, FOX_HD), flogf,
              c_new, n_new, m_new, conv_new)
    return cat_a, cat_m, states


def kernel(x_prompt, x_sample, cache_fox_k, cache_fox_v, cache_fox_logf, state_mlstm_C, state_mlstm_n,
           state_mlstm_m, state_mlstm_conv, norm_mix_g, w_in, b_fox_f, conv_w, conv_b, b_ml_i, b_ml_f,
           g_fox, g_ml, w_out, norm_ffn_g, w_router, b_router, w_gate_up, b_gate_up, w_down, b_down,
           norm_final_g):
    depth = w_in.shape[0]
    assert depth == 1, "the final norm is fused into the last layer's combine; only depth 1 is wired up"
    bp, sp, d = x_prompt.shape
    bs, ss, _ = x_sample.shape
    n_exp = w_router.shape[-1]
    d_ff = w_down.shape[2]
    yp, ys = x_prompt, x_sample
    p_st, s_st = [], []

    for l in range(depth):
        wl = w_in[l]
        w = {
            "g_mix": norm_mix_g[l][None, :],
            "wbig": jnp.concatenate([wl[:, O_FQ:O_FF], wl[:, O_MQ:O_MI]], axis=1).astype(BF16),
            "wsm": jnp.pad(jnp.concatenate([wl[:, O_FF:O_MQ], wl[:, O_MI:P_IN]], axis=1),
                           ((0, 0), (0, V7X_LANES - S_END))).astype(BF16),
            "bsm": jnp.pad(jnp.concatenate([b_fox_f[l], b_ml_i[l], b_ml_f[l]]), (0, V7X_LANES - S_END))[None, :],
            "conv_w": conv_w[l],
            "conv_b": conv_b[l][None, :],
            "g_fox": g_fox[l][None, :],
            "g_ml": g_ml[l][None, :],
        }
        w_a = w_out[l][:FOX_W].astype(BF16)
        w_m = w_out[l][FOX_W:].astype(BF16)
        g_ffn = norm_ffn_g[l][None, :]
        wr = jnp.pad(w_router[l], ((0, 0), (0, V7X_LANES - n_exp)))
        wr_hi = wr.astype(BF16)
        wr_lo = (wr - wr_hi.astype(F32)).astype(BF16)
        br = jnp.pad(b_router[l], (0, V7X_LANES - n_exp), constant_values=NEG)[None, :]

        zeros_state = (jnp.zeros((bp, ML_HEADS, ML_DV, ML_DK), F32), jnp.zeros((bp, ML_HEADS, ML_DK), F32),
                       jnp.zeros((bp, ML_HEADS), F32))
        cat_a_p, cat_m_p, st_p = _mixer(yp, jnp.zeros((bp, CONV_W - 1, QK_W), F32), None, zeros_state, w,
                                        min(PROJ_ROWS, sp), min(FOX_QBLOCK, sp), min(FOX_BLOCK, sp), CHUNK)
        p_st.append(st_p)
        past = cache_fox_k.shape[2]
        tk_s = -(-(past + ss) // V7X_LANES) * V7X_LANES
        cat_a_s, cat_m_s, st_s = _mixer(
            ys, state_mlstm_conv[l], (cache_fox_k[l], cache_fox_v[l], cache_fox_logf[l]),
            (state_mlstm_C[l].astype(F32), state_mlstm_n[l].astype(F32), state_mlstm_m[l].astype(F32)),
            w, ss, ss, tk_s, ss)
        s_st.append(st_s)

        tp, ts = bp * sp, bs * ss
        y1_p, xn_p, ridx_p, rgate_p, cnt_p = _merge(
            cat_a_p.reshape(tp, FOX_W), cat_m_p.reshape(tp, ML_W), yp.reshape(tp, d), w_a, w_m, g_ffn,
            wr_hi, wr_lo, br, min(MERGE_ROWS, tp))
        y1_s, xn_s, ridx_s, rgate_s, cnt_s = _merge(
            cat_a_s.reshape(ts, FOX_W), cat_m_s.reshape(ts, ML_W), ys.reshape(ts, d), w_a, w_m, g_ffn,
            wr_hi, wr_lo, br, min(MERGE_ROWS, ts))

        cnt_p = cnt_p[0, :n_exp].astype(jnp.int32)
        cnt_s = cnt_s[0, :n_exp].astype(jnp.int32)
        seg_rows = -(-(cnt_p + cnt_s) // FFN_ROWS) * FFN_ROWS
        seg_end = jnp.cumsum(seg_rows)
        seg_start = seg_end - seg_rows
        n_rows = -(-((tp + ts) * TOP_K + n_exp * (FFN_ROWS - 1)) // FFN_ROWS) * FFN_ROWS
        n_tiles = n_rows // FFN_ROWS
        tile_row0 = jnp.arange(n_tiles, dtype=jnp.int32) * FFN_ROWS
        tile_expert = jnp.minimum(jnp.sum((seg_end[None, :] <= tile_row0[:, None]).astype(jnp.int32), axis=1),
                                  n_exp - 1)
        n_used = (seg_end[-1:] // FFN_ROWS).astype(jnp.int32)
        e_p, rank_p = ridx_p[:, :TOP_K], ridx_p[:, TOP_K:2 * TOP_K]
        e_s, rank_s = ridx_s[:, :TOP_K], ridx_s[:, TOP_K:2 * TOP_K]
        pos_p = seg_start[e_p] + rank_p
        pos_s = seg_start[e_s] + cnt_p[e_s] + rank_s
        tm_p, tm_s = min(MOE_ROWS, tp), min(MOE_ROWS, ts)
        pos3_p = pos_p.reshape(tp // tm_p, 1, TOP_K * tm_p)
        pos3_s = pos_s.reshape(ts // tm_s, 1, TOP_K * tm_s)

        xs = jnp.zeros((n_rows, d), F32)
        xs = _scatter(pos3_p, xn_p, xs, tm_p)
        xs = _scatter(pos3_s, xn_s, xs, tm_s)
        ysort = _ffn(tile_expert, n_used, xs, w_gate_up[l], b_gate_up[l].reshape(n_exp, 1, 2 * d_ff),
                     w_down[l], b_down[l].reshape(n_exp, 1, d), FFN_ROWS)

        g_fin = norm_final_g[None, :]
        yp = _combine(pos3_p, y1_p, rgate_p, g_fin, ysort, tm_p).reshape(bp, sp, d)
        ys = _combine(pos3_s, y1_s, rgate_s, g_fin, ysort, tm_s).reshape(bs, ss, d)

    p_out = tuple(jnp.stack(a) for a in zip(*p_st))
    s_out = tuple(jnp.stack(a) for a in zip(*s_st))
    return (yp, ys) + p_out + s_out
```

```python
import functools
import math

import numpy as np

import jax
import jax.numpy as jnp
from jax import lax
from jax.experimental import pallas as pl
from jax.experimental.pallas import tpu as pltpu

F32 = jnp.float32
BF16 = jnp.bfloat16

FOX_HEADS = 8
FOX_HD = 64
FOX_W = FOX_HEADS * FOX_HD
ML_HEADS = 4
ML_DK = 128
ML_DV = 128
ML_W = ML_HEADS * ML_DV
QK_W = 2 * ML_HEADS * ML_DK
CONV_W = 4
CHUNK = 64
TOP_K = 4
SWIGLU_LIMIT = 7.0
SWIGLU_ALPHA = 1.702
EPS = 1e-6
NEG = -1e30
LOG2E = math.log2(math.e)

O_FQ = 0
O_FK = O_FQ + FOX_W
O_FV = O_FK + FOX_W
O_FF = O_FV + FOX_W
O_MQ = O_FF + FOX_HEADS
O_MK = O_MQ + ML_HEADS * ML_DK
O_MV = O_MK + ML_HEADS * ML_DK
O_MO = O_MV + ML_W
O_MI = O_MO + ML_W
O_MF = O_MI + ML_HEADS
P_IN = O_MF + ML_HEADS

B_FQ, B_FK, B_FV, B_QK, B_MV, B_MO, B_END = 0, 512, 1024, 1536, 2560, 3072, 3584
S_FF, S_MI, S_MF, S_END = 0, 8, 12, 16
BIAS_TERMS = 3

V7X_LANES = 128
V7X_SUBLANES = 8
V7X_VMEM_LIMIT_BYTES = 56 * 1024 * 1024

PROJ_ROWS = 512
FOX_BLOCK = 512
FOX_QBLOCK = 1024
FOX_SUB = 512
MERGE_ROWS = 512
MOE_ROWS = 512
FFN_ROWS = 512
DMA_UNROLL = 8


def _params(*semantics):
    return pltpu.CompilerParams(dimension_semantics=semantics, vmem_limit_bytes=V7X_VMEM_LIMIT_BYTES)


def _dot(a, b):
    return jnp.dot(a, b, preferred_element_type=F32)


def _dot_nt(a, b):
    return lax.dot_general(a, b, (((1,), (1,)), ((), ())), preferred_element_type=F32)


def _dot_tn(a, b):
    return lax.dot_general(a, b, (((0,), (0,)), ((), ())), preferred_element_type=F32)


def _sigmoid(x):
    return 1.0 / (1.0 + jnp.exp(-x))


def _log_sigmoid(x):
    return jnp.minimum(x, 0.0) - jnp.log1p(jnp.exp(-jnp.abs(x)))


def _split3(x):
    t1 = x.astype(BF16)
    r1 = x - t1.astype(F32)
    t2 = r1.astype(BF16)
    t3 = (r1 - t2.astype(F32)).astype(BF16)
    return t1, t2, t3


def _lower_tri(n):
    r = lax.broadcasted_iota(jnp.int32, (n, n), 0)
    c = lax.broadcasted_iota(jnp.int32, (n, n), 1)
    return c <= r


def _proj_kernel(x_ref, g_ref, wbig_ref, wsm_ref, bsm_ref, cw_ref, cb_ref, hist_ref,
                 fq_ref, fk32_ref, fv32_ref, fk16_ref, fv16_ref, sm_ref, flogf_ref, mq_ref, mk_ref, mv_ref,
                 mo_ref, cnew_ref, halo_ref):
    tm = x_ref.shape[1]

    @pl.when(pl.program_id(1) == 0)
    def _():
        halo_ref[...] = hist_ref[0]

    x = x_ref[0]
    xn = (x * lax.rsqrt(jnp.mean(x * x, axis=-1, keepdims=True) + EPS)) * g_ref[...]
    xb = xn.astype(BF16)
    z = _dot(xb, wbig_ref[...])

    fq_ref[0] = (z[:, B_FQ:B_FK] * (FOX_HD ** -0.5 * LOG2E)).astype(BF16)
    fk = z[:, B_FK:B_FV]
    fv = z[:, B_FV:B_QK]
    fk32_ref[0] = fk
    fv32_ref[0] = fv
    fk16_ref[0] = fk.astype(BF16)
    fv16_ref[0] = fv.astype(BF16)
    mv_ref[0] = z[:, B_MV:B_MO].astype(BF16)
    mo_ref[0] = _sigmoid(z[:, B_MO:B_END]).astype(BF16)

    zs = _dot(xb, wsm_ref[...]) + bsm_ref[...]
    lane = lax.broadcasted_iota(jnp.int32, zs.shape, 1)
    is_forget = (lane < S_MI) | ((lane >= S_MF) & (lane < S_END))
    sm = jnp.where(is_forget, _log_sigmoid(zs), zs)
    sm_ref[0] = sm
    flogf_ref[0] = sm[:, S_FF:S_MI]

    u = z[:, B_QK:B_MV]
    up = jnp.concatenate([halo_ref[...], u], axis=0)
    n = tm + V7X_SUBLANES
    first = V7X_SUBLANES - (CONV_W - 1)
    y = cb_ref[...] + cw_ref[0:1, :] * pltpu.roll(up, n - first, axis=0)[:tm]
    for j in range(1, CONV_W - 1):
        y = y + cw_ref[j:j + 1, :] * pltpu.roll(up, n - (first + j), axis=0)[:tm]
    y = y + cw_ref[CONV_W - 1:CONV_W, :] * u
    qk = y * _sigmoid(y)
    mq_ref[0] = qk[:, :QK_W // 2].astype(BF16)
    mk_ref[0] = (qk[:, QK_W // 2:] * (ML_DK ** -0.5)).astype(BF16)

    halo_ref[...] = u[tm - V7X_SUBLANES:, :]
    cnew_ref[0] = halo_ref[first:, :]


def _project(x, hist8, g, wbig, wsm, bsm, cw, cb, tm):
    bsz, seq, d = x.shape
    grid = (bsz, seq // tm)
    row = lambda c: pl.BlockSpec((1, tm, c), lambda b, s: (b, s, 0))
    const = lambda shape: pl.BlockSpec(shape, lambda b, s: (0,) * len(shape))
    outs = [
        (FOX_W, BF16), (FOX_W, F32), (FOX_W, F32), (FOX_W, BF16), (FOX_W, BF16), (V7X_LANES, F32),
        (FOX_HEADS, F32), (ML_W, BF16), (ML_W, BF16), (ML_W, BF16), (ML_W, BF16),
    ]
    out_shape = [jax.ShapeDtypeStruct((bsz, seq, c), dt) for c, dt in outs]
    out_specs = [row(c) for c, _ in outs]
    out_shape.append(jax.ShapeDtypeStruct((bsz, CONV_W - 1, QK_W), F32))
    out_specs.append(pl.BlockSpec((1, CONV_W - 1, QK_W), lambda b, s: (b, 0, 0)))
    return pl.pallas_call(
        _proj_kernel,
        grid=grid,
        in_specs=[row(d), const((1, d)), const(wbig.shape), const(wsm.shape), const((1, V7X_LANES)),
                  const((CONV_W, QK_W)), const((1, QK_W)),
                  pl.BlockSpec((1, V7X_SUBLANES, QK_W), lambda b, s: (b, 0, 0))],
        out_specs=out_specs,
        out_shape=out_shape,
        scratch_shapes=[pltpu.VMEM((V7X_SUBLANES, QK_W), F32)],
        compiler_params=_params("arbitrary", "arbitrary"),
        name="proj",
    )(x, g, wbig, wsm, bsm, cw, cb, hist8)


def _bias_lane(head, term):
    return (head // 2) * V7X_LANES + (FOX_HD if head % 2 == 0 else 0) + term


def _bias_placement():
    place = np.zeros((BIAS_TERMS, V7X_LANES, FOX_W), np.float32)
    for h in range(FOX_HEADS):
        for t in range(BIAS_TERMS):
            place[t, S_FF + h, _bias_lane(h, t)] = 1.0
    return jnp.asarray(place, BF16)


def _fox_prep_kernel(lf_ref, k_ref, v_ref, place_ref, ka0_ref, ka1_ref, va0_ref, va1_ref, carry_ref):
    tm = lf_ref.shape[1]

    @pl.when(pl.program_id(1) == 0)
    def _():
        carry_ref[...] = jnp.zeros_like(carry_ref)

    tri = jnp.where(_lower_tri(tm), 1.0, 0.0).astype(BF16)
    t1, t2, t3 = _split3(lf_ref[0])
    c = carry_ref[...] + (_dot(tri, t1) + (_dot(tri, t2) + _dot(tri, t3)))
    carry_ref[...] = c[tm - 1:tm, :]
    b1, b2, b3 = _split3(c * (-LOG2E))
    kb = _dot(b1, place_ref[0]) + (_dot(b2, place_ref[1]) + _dot(b3, place_ref[2]))

    k = k_ref[0].astype(F32)
    v = v_ref[0].astype(F32)
    first_head = lax.broadcasted_iota(jnp.int32, k.shape, 1) % V7X_LANES < FOX_HD
    ka0_ref[0] = jnp.where(first_head, k, kb).astype(BF16)
    ka1_ref[0] = jnp.where(first_head, kb, k).astype(BF16)
    va0_ref[0] = jnp.where(first_head, v, 1.0).astype(BF16)
    va1_ref[0] = jnp.where(first_head, 1.0, v).astype(BF16)


def _fox_prep(lf, k, v, tm):
    bsz, lk, _ = lf.shape
    place = _bias_placement()
    wide = pl.BlockSpec((1, tm, FOX_W), lambda b, s: (b, s, 0))
    sds = jax.ShapeDtypeStruct((bsz, lk, FOX_W), BF16)
    return pl.pallas_call(
        _fox_prep_kernel,
        grid=(bsz, lk // tm),
        in_specs=[pl.BlockSpec((1, tm, V7X_LANES), lambda b, s: (b, s, 0)), wide, wide,
                  pl.BlockSpec(place.shape, lambda b, s: (0, 0, 0))],
        out_specs=[wide, wide, wide, wide],
        out_shape=[sds, sds, sds, sds],
        scratch_shapes=[pltpu.VMEM((1, V7X_LANES), F32)],
        compiler_params=_params("arbitrary", "arbitrary"),
        name="fox_prep",
    )(lf, k, v, place)


def _fox_kernel(q_ref, ka0_ref, ka1_ref, va0_ref, va1_ref, g_ref, o_ref, *, tq, tk, sub, q_offset):
    ts = min(tq, sub)
    n_sub = tq // ts
    diag_blocks = max(ts // tk, 1)
    k_refs = (ka0_ref, ka1_ref)
    v_refs = (va0_ref, va1_ref)
    lane = lax.broadcasted_iota(jnp.int32, (ts, V7X_LANES), 1)
    first_head = lane < FOX_HD
    q_lo = q_offset + pl.program_id(2) * tq
    n_common = (q_lo + 1) // tk

    def chain_q(r, hh):
        q = q_ref[0, r * ts:(r + 1) * ts, :].astype(F32)
        if hh == 0:
            return jnp.where(first_head, q, jnp.where(lane < FOX_HD + BIAS_TERMS, 1.0, 0.0)).astype(BF16)
        return jnp.where(first_head, jnp.where(lane < BIAS_TERMS, 1.0, 0.0), q).astype(BF16)

    chains = [(r, hh) for r in range(n_sub) for hh in range(2)]
    qs = [chain_q(r, hh) for r, hh in chains]

    def step(state, c, j, masked):
        r, hh = chains[c]
        m, acc = state
        start = pl.multiple_of(j * tk, tk)
        s = _dot_nt(qs[c], k_refs[hh][0, pl.ds(start, tk), :])
        if masked:
            qpos = q_lo + r * ts + lax.broadcasted_iota(jnp.int32, s.shape, 0)
            kpos = start + lax.broadcasted_iota(jnp.int32, s.shape, 1)
            s = jnp.where(kpos <= qpos, s, NEG)
        m_new = jnp.maximum(m, jnp.max(s, axis=-1, keepdims=True))
        p = jnp.exp2(s - m_new).astype(BF16)
        acc = jnp.exp2(m - m_new) * acc + _dot(p, v_refs[hh][0, pl.ds(start, tk), :])
        return m_new, acc

    def common(j, states):
        return tuple(step(st, c, j, False) for c, st in enumerate(states))

    def common_pair(j2, states):
        return common(2 * j2 + 1, common(2 * j2, states))

    init = (jnp.full((ts, 1), NEG, F32), jnp.zeros((ts, V7X_LANES), F32))
    states = lax.fori_loop(0, n_common // 2, common_pair, (init,) * len(chains))
    states = list(lax.fori_loop(2 * (n_common // 2), n_common, common, states))
    for c, (r, _) in enumerate(chains):
        for e in range(r * diag_blocks):
            states[c] = step(states[c], c, n_common + e, False)
        for e in range(diag_blocks):
            states[c] = step(states[c], c, n_common + r * diag_blocks + e, True)

    for r in range(n_sub):
        out = jnp.zeros((ts, V7X_LANES), F32)
        for hh in range(2):
            acc = states[chains.index((r, hh))][1]
            own = first_head if hh == 0 else jnp.logical_not(first_head)
            denom_lane = FOX_HD if hh == 0 else 0
            o = jnp.where(own, acc / acc[:, denom_lane:denom_lane + 1], 0.0)
            out = out + o * lax.rsqrt(jnp.sum(o * o, axis=-1, keepdims=True) * (1.0 / FOX_HD) + EPS)
        o_ref[0, r * ts:(r + 1) * ts, :] = (out * g_ref[...]).astype(BF16)


def _fox(q, ka0, ka1, va0, va1, g_fox, q_offset, tq, tk):
    bsz, lq, _ = q.shape
    lk = ka0.shape[1]
    assert (q_offset % tk == 0 and tq % tk == 0) or (lq == tq <= tk and lk == tk), (q_offset, tq, tk, lq, lk)
    pairs = FOX_HEADS // 2
    grid = (bsz, pairs, lq // tq)
    kv_spec = pl.BlockSpec((1, lk, V7X_LANES), lambda b, p, i: (b, 0, p))
    q_spec = pl.BlockSpec((1, tq, V7X_LANES), lambda b, p, i: (b, i, p))
    return pl.pallas_call(
        functools.partial(_fox_kernel, tq=tq, tk=tk, sub=FOX_SUB, q_offset=q_offset),
        grid=grid,
        in_specs=[q_spec, kv_spec, kv_spec, kv_spec, kv_spec,
                  pl.BlockSpec((1, V7X_LANES), lambda b, p, i: (0, p))],
        out_specs=q_spec,
        out_shape=jax.ShapeDtypeStruct((bsz, lq, FOX_W), BF16),
        compiler_params=_params("arbitrary", "arbitrary", "arbitrary"),
        name="fox_attention",
    )(q, ka0, ka1, va0, va1, g_fox)


def _mlstm_kernel(q_ref, k_ref, v_ref, mo_ref, sm_ref, c0_ref, n0_ref, m0_ref, g_ref,
                  o_ref, c_ref, n_ref, m_ref):
    bsz, L, _ = q_ref.shape

    @pl.when(pl.program_id(0) == 0)
    def _():
        c_ref[...] = c0_ref[...]
        n_ref[...] = n0_ref[...]
        m_ref[...] = m0_ref[...]

    causal = _lower_tri(L)
    tri = jnp.where(causal, 1.0, 0.0).astype(BF16)
    eye = jnp.where(lax.broadcasted_iota(jnp.int32, (L, L), 0) == lax.broadcasted_iota(jnp.int32, (L, L), 1),
                    1.0, 0.0).astype(BF16)
    rows = lax.broadcasted_iota(jnp.int32, (L, V7X_LANES), 0)
    lanes = lax.broadcasted_iota(jnp.int32, (L, V7X_LANES), 1)
    gate_lanes = (lanes >= S_MI) & (lanes < S_END)

    def per_batch(b, _):
        sm = jnp.where(gate_lanes, sm_ref[b], 0.0)
        t1, t2, t3 = _split3(sm)
        bcum = _dot(tri, t1) + (_dot(tri, t2) + _dot(tri, t3))
        bcum = pltpu.roll(bcum, V7X_LANES - (S_MF - S_MI), axis=1)
        g = sm - bcum
        gmax = g
        sh = 1
        while sh < L:
            gmax = jnp.maximum(gmax, jnp.where(rows >= sh, pltpu.roll(gmax, sh, axis=0), -jnp.inf))
            sh *= 2
        m = m_ref[pl.ds(b, 1), :]
        u = jnp.maximum(m, gmax)
        a = jnp.exp(m - u)
        mt = bcum + u
        em = jnp.exp(-mt)
        u_last = u[L - 1:L, :]
        a_last = a[L - 1:L, :]
        wcol = jnp.exp(g - u_last)
        m_ref[pl.ds(b, 1), :] = mt[L - 1:L, :]
        g1, g2, g3 = _split3(g)
        g_rows = _dot_tn(g1, eye) + (_dot_tn(g2, eye) + _dot_tn(g3, eye))

        for h in range(ML_HEADS):
            cs = slice(h * ML_DK, (h + 1) * ML_DK)
            gl = S_MI + h
            qh = q_ref[b, :, cs]
            kh = k_ref[b, :, cs]
            vh = v_ref[b, :, cs]
            decay = jnp.exp(jnp.where(causal, g_rows[gl:gl + 1, :] - u[:, gl:gl + 1], NEG))
            s = _dot_nt(qh, kh) * decay
            c_old = c_ref[b, h]
            n_old = n_ref[b, h:h + 1, :]
            a_h = a[:, gl:gl + 1]
            num = a_h * _dot_nt(qh, c_old.astype(BF16)) + _dot(s.astype(BF16), vh)
            den = (a_h * jnp.sum(qh.astype(F32) * n_old, axis=-1, keepdims=True)
                   + jnp.sum(s, axis=-1, keepdims=True))
            hv = num / jnp.maximum(jnp.abs(den), em[:, gl:gl + 1])
            hn = hv * lax.rsqrt(jnp.mean(hv * hv, axis=-1, keepdims=True) + EPS)
            o_ref[b, :, cs] = (hn * g_ref[:, cs] * mo_ref[b, :, cs].astype(F32)).astype(BF16)

            w_h = wcol[:, gl:gl + 1]
            vw = (vh.astype(F32) * w_h).astype(BF16)
            c_ref[b, h] = a_last[:, gl:gl + 1] * c_old + _dot_tn(vw, kh)
            n_ref[b, h:h + 1, :] = (a_last[:, gl:gl + 1] * n_old
                                    + jnp.sum(kh.astype(F32) * w_h, axis=0, keepdims=True))
        return 0

    for b in range(bsz):
        per_batch(b, 0)


def _mlstm(mq, mk, mv, mo, small, c0, n0, m0, g_ml, L):
    bsz, seq, _ = mq.shape
    chunk = pl.BlockSpec((bsz, L, ML_W), lambda c: (0, c, 0))
    full = lambda shape: pl.BlockSpec(shape, lambda c: (0,) * len(shape))
    return pl.pallas_call(
        _mlstm_kernel,
        grid=(seq // L,),
        in_specs=[chunk, chunk, chunk, chunk,
                  pl.BlockSpec((bsz, L, V7X_LANES), lambda c: (0, c, 0)),
                  full(c0.shape), full(n0.shape), full(m0.shape), full((1, ML_W))],
        out_specs=[chunk, full(c0.shape), full(n0.shape), full(m0.shape)],
        out_shape=[jax.ShapeDtypeStruct((bsz, seq, ML_W), BF16),
                   jax.ShapeDtypeStruct(c0.shape, F32),
                   jax.ShapeDtypeStruct(n0.shape, F32),
                   jax.ShapeDtypeStruct(m0.shape, F32)],
        compiler_params=_params("arbitrary"),
        name="mlstm",
    )(mq, mk, mv, mo, small, c0, n0, m0, g_ml)


def _merge_kernel(ca_ref, cm_ref, x_ref, wa_ref, wm_ref, g_ref, wrh_ref, wrl_ref, br_ref,
                  y1_ref, xn_ref, ridx_ref, rgate_ref, cnt_ref):
    tm = x_ref.shape[0]

    @pl.when(pl.program_id(0) == 0)
    def _():
        cnt_ref[...] = jnp.zeros_like(cnt_ref)

    y1 = x_ref[...] + (_dot(ca_ref[...], wa_ref[...]) + _dot(cm_ref[...], wm_ref[...]))
    y1_ref[...] = y1
    xn = (y1 * lax.rsqrt(jnp.mean(y1 * y1, axis=-1, keepdims=True) + EPS)) * g_ref[...]
    xn_ref[...] = xn.reshape(xn_ref.shape)

    xh = xn.astype(BF16)
    xl = (xn - xh.astype(F32)).astype(BF16)
    logits = (_dot(xh, wrh_ref[...]) + (_dot(xl, wrh_ref[...]) + _dot(xh, wrl_ref[...]))) + br_ref[...]

    lane = lax.broadcasted_iota(jnp.int32, logits.shape, 1)
    vals, sel = [], []
    work = logits
    for _ in range(TOP_K):
        mx = jnp.max(work, axis=-1, keepdims=True)
        idx = jnp.min(jnp.where(work == mx, lane, V7X_LANES), axis=-1, keepdims=True)
        vals.append(mx)
        sel.append(idx)
        work = jnp.where(lane == idx, -jnp.inf, work)
    ex = [jnp.exp(v - vals[0]) for v in vals]
    tot = ex[0] + ex[1] + ex[2] + ex[3]

    onehot = [(lane == idx) for idx in sel]
    picked = jnp.where(onehot[0] | onehot[1] | onehot[2] | onehot[3], 1.0, 0.0)
    earlier = (lax.broadcasted_iota(jnp.int32, (tm, tm), 1) < lax.broadcasted_iota(jnp.int32, (tm, tm), 0))
    before = _dot(jnp.where(earlier, 1.0, 0.0).astype(BF16), picked.astype(BF16)) + cnt_ref[...]
    cnt_ref[...] = cnt_ref[...] + jnp.sum(picked, axis=0, keepdims=True)

    ridx = jnp.zeros(logits.shape, jnp.int32)
    rgate = jnp.zeros(logits.shape, F32)
    for k in range(TOP_K):
        rank = jnp.sum(jnp.where(onehot[k], before, 0.0), axis=-1, keepdims=True).astype(jnp.int32)
        ridx = jnp.where(lane == k, sel[k], ridx)
        ridx = jnp.where(lane == TOP_K + k, rank, ridx)
        rgate = jnp.where(lane == k, ex[k] / tot, rgate)
    ridx_ref[...] = ridx
    rgate_ref[...] = rgate


def _merge(cat_a, cat_m, x, w_a, w_m, g_ffn, wr_hi, wr_lo, br, tm):
    t, d = x.shape
    row = lambda c: pl.BlockSpec((tm, c), lambda i: (i, 0))
    const = lambda shape: pl.BlockSpec(shape, lambda i: (0,) * len(shape))
    return pl.pallas_call(
        _merge_kernel,
        grid=(t // tm,),
        in_specs=[row(FOX_W), row(ML_W), row(d), const(w_a.shape), const(w_m.shape), const((1, d)),
                  const(wr_hi.shape), const(wr_lo.shape), const((1, V7X_LANES))],
        out_specs=[row(d), pl.BlockSpec((tm,) + _row_tile(d), lambda i: (i, 0, 0)), row(V7X_LANES),
                   row(V7X_LANES), const((1, V7X_LANES))],
        out_shape=[jax.ShapeDtypeStruct((t, d), F32), jax.ShapeDtypeStruct((t,) + _row_tile(d), F32),
                   jax.ShapeDtypeStruct((t, V7X_LANES), jnp.int32),
                   jax.ShapeDtypeStruct((t, V7X_LANES), F32),
                   jax.ShapeDtypeStruct((1, V7X_LANES), F32)],
        compiler_params=_params("arbitrary"),
        name="merge_router",
    )(cat_a, cat_m, x, w_a, w_m, g_ffn, wr_hi, wr_lo, br)


def _row_tile(d):
    return (d // V7X_LANES, V7X_LANES)


def _row_copy(src_ref, src_row, dst_ref, dst_row, sem):
    return pltpu.make_async_copy(src_ref.at[pl.ds(src_row, 1)], dst_ref.at[pl.ds(dst_row, 1)], sem)


def _scatter_kernel(pos_ref, x_ref, xs_in_ref, xs_ref, sem):
    del xs_in_ref
    tm = x_ref.shape[0]

    def issue(g, _):
        for u in range(DMA_UNROLL):
            r = g * DMA_UNROLL + u
            for k in range(TOP_K):
                _row_copy(x_ref, r, xs_ref, pos_ref[0, 0, TOP_K * r + k], sem).start(priority=k % 2)
        return 0

    def drain(g, _):
        for _ in range(DMA_UNROLL * TOP_K):
            _row_copy(x_ref, 0, xs_ref, 0, sem).wait()
        return 0

    lax.fori_loop(0, tm // DMA_UNROLL, issue, 0)
    lax.fori_loop(0, tm // DMA_UNROLL, drain, 0)


def _scatter(pos3, x, xs, tm):
    t, sub, lanes = x.shape
    return pl.pallas_call(
        _scatter_kernel,
        grid=(t // tm,),
        in_specs=[pl.BlockSpec((1, 1, TOP_K * tm), lambda i: (i, 0, 0), memory_space=pltpu.SMEM),
                  pl.BlockSpec((tm, sub, lanes), lambda i: (i, 0, 0)),
                  pl.BlockSpec(memory_space=pl.ANY)],
        out_specs=pl.BlockSpec(memory_space=pl.ANY),
        out_shape=jax.ShapeDtypeStruct(xs.shape, xs.dtype),
        scratch_shapes=[pltpu.SemaphoreType.DMA(())],
        input_output_aliases={2: 0},
        compiler_params=_params("arbitrary"),
        name="moe_scatter",
    )(pos3, x, xs)


def _ffn_kernel(te_ref, nu_ref, x_ref, wgu_ref, bgu_ref, wd_ref, bd_ref, y_ref, wgu16_ref, wd16_ref):
    i = pl.program_id(0)
    d_ff = wd_ref.shape[1]

    @pl.when((i == 0) | (te_ref[i] != te_ref[jnp.maximum(i - 1, 0)]))
    def _():
        wgu16_ref[...] = wgu_ref[0].astype(BF16)
        wd16_ref[...] = wd_ref[0].astype(BF16)

    @pl.when(i < nu_ref[0])
    def _():
        x = x_ref[...].reshape(x_ref.shape[0], -1)
        gu = _dot(x.astype(BF16), wgu16_ref[...]) + bgu_ref[0]
        gate = jnp.minimum(gu[:, :d_ff], SWIGLU_LIMIT)
        up = jnp.clip(gu[:, d_ff:], -SWIGLU_LIMIT, SWIGLU_LIMIT)
        h = (up + 1.0) * (gate * _sigmoid(SWIGLU_ALPHA * gate))
        y = _dot(h.astype(BF16), wd16_ref[...]) + bd_ref[0]
        y_ref[...] = y.reshape(y_ref.shape)

    @pl.when(i >= nu_ref[0])
    def _():
        y_ref[...] = jnp.zeros_like(y_ref)


def _ffn(tile_expert, n_used, xs, w_gu, b_gu, w_d, b_d, tm):
    r = xs.shape[0]
    d_ff, d = w_d.shape[1:]
    tiles = pl.BlockSpec((tm,) + _row_tile(d), lambda i, te, nu: (i, 0, 0))
    grid_spec = pltpu.PrefetchScalarGridSpec(
        num_scalar_prefetch=2,
        grid=(r // tm,),
        in_specs=[tiles,
                  pl.BlockSpec((1, d, 2 * d_ff), lambda i, te, nu: (te[i], 0, 0)),
                  pl.BlockSpec((1, 1, 2 * d_ff), lambda i, te, nu: (te[i], 0, 0)),
                  pl.BlockSpec((1, d_ff, d), lambda i, te, nu: (te[i], 0, 0)),
                  pl.BlockSpec((1, 1, d), lambda i, te, nu: (te[i], 0, 0))],
        out_specs=tiles,
        scratch_shapes=[pltpu.VMEM((d, 2 * d_ff), BF16), pltpu.VMEM((d_ff, d), BF16)],
    )
    return pl.pallas_call(
        _ffn_kernel,
        grid_spec=grid_spec,
        out_shape=jax.ShapeDtypeStruct(xs.shape, F32),
        compiler_params=_params("arbitrary"),
        name="moe_ffn",
    )(tile_expert, n_used, xs, w_gu, b_gu, w_d, b_d)


def _combine_kernel(pos_ref, y1_ref, gate_ref, gfin_ref, ys_ref, o_ref, buf_ref, sem):
    tm = y1_ref.shape[0]

    def issue(g, _):
        for u in range(DMA_UNROLL):
            r = g * DMA_UNROLL + u
            for k in range(TOP_K):
                _row_copy(ys_ref, pos_ref[0, 0, TOP_K * r + k], buf_ref.at[k], r, sem).start(priority=k % 2)
        return 0

    def drain(g, _):
        for _ in range(DMA_UNROLL * TOP_K):
            _row_copy(ys_ref, 0, buf_ref.at[0], 0, sem).wait()
        return 0

    lax.fori_loop(0, tm // DMA_UNROLL, issue, 0)
    lax.fori_loop(0, tm // DMA_UNROLL, drain, 0)

    gate = gate_ref[...]
    moe = gate[:, 0:1] * buf_ref[0].reshape(tm, -1)
    for k in range(1, TOP_K):
        moe = moe + gate[:, k:k + 1] * buf_ref[k].reshape(tm, -1)
    y = y1_ref[...] + moe
    o_ref[...] = (y * lax.rsqrt(jnp.mean(y * y, axis=-1, keepdims=True) + EPS)) * gfin_ref[...]


def _combine(pos3, y1, rgate, g_final, ys, tm):
    t, d = y1.shape
    return pl.pallas_call(
        _combine_kernel,
        grid=(t // tm,),
        in_specs=[pl.BlockSpec((1, 1, TOP_K * tm), lambda i: (i, 0, 0), memory_space=pltpu.SMEM),
                  pl.BlockSpec((tm, d), lambda i: (i, 0)),
                  pl.BlockSpec((tm, V7X_LANES), lambda i: (i, 0)),
                  pl.BlockSpec((1, d), lambda i: (0, 0)),
                  pl.BlockSpec(memory_space=pl.ANY)],
        out_specs=pl.BlockSpec((tm, d), lambda i: (i, 0)),
        out_shape=jax.ShapeDtypeStruct((t, d), F32),
        scratch_shapes=[pltpu.VMEM((TOP_K, tm) + _row_tile(d), F32), pltpu.SemaphoreType.DMA(())],
        compiler_params=_params("arbitrary"),
        name="moe_combine",
    )(pos3, y1, rgate, g_final, ys)


def _mixer(x, hist, fox_cache, ml_state, w, tm, tq, tk, chunk):
    bsz, seq, _ = x.shape
    hist8 = jnp.pad(hist, ((0, 0), (V7X_SUBLANES - (CONV_W - 1), 0), (0, 0)))
    (fq, fk32, fv32, fk16, fv16, small, flogf, mq, mk, mv, mo, conv_new) = _project(
        x, hist8, w["g_mix"], w["wbig"], w["wsm"], w["bsm"], w["conv_w"], w["conv_b"], tm)

    if fox_cache is None:
        q_offset = 0
        lf_all, k_all, v_all = small, fk16, fv16
    else:
        ck_c, cv_c, clogf_c = fox_cache
        q_offset = ck_c.shape[1]
        clogf_c = jnp.pad(clogf_c, ((0, 0), (0, 0), (S_FF, V7X_LANES - S_FF - FOX_HEADS)))
        lf_all = jnp.concatenate([clogf_c, small], axis=1)
        k_all = jnp.concatenate([ck_c.reshape(bsz, q_offset, FOX_W).astype(BF16), fk16], axis=1)
        v_all = jnp.concatenate([cv_c.reshape(bsz, q_offset, FOX_W).astype(BF16), fv16], axis=1)
    lk = k_all.shape[1]
    lk_pad = -(-lk // tk) * tk
    pad_rows = ((0, 0), (0, lk_pad - lk), (0, 0))
    lf_all, k_all, v_all = jnp.pad(lf_all, pad_rows), jnp.pad(k_all, pad_rows), jnp.pad(v_all, pad_rows)
    ka0, ka1, va0, va1 = _fox_prep(lf_all, k_all, v_all, math.gcd(tk, FOX_BLOCK))
    cat_a = _fox(fq, ka0, ka1, va0, va1, w["g_fox"], q_offset, tq, tk)

    c0, n0, m0 = ml_state
    m0 = jnp.pad(m0, ((0, 0), (S_MI, V7X_LANES - S_MI - ML_HEADS)))
    cat_m, c_new, n_new, m_new = _mlstm(mq, mk, mv, mo, small, c0, n0, m0, w["g_ml"], chunk)
    m_new = m_new[:, S_MI:S_MI + ML_HEADS]

    states = (fk32.reshape(bsz, seq, FOX_HEADS, FOX_HD), fv32.reshape(bsz, seq, FOX_HEADS, FOX_HD), flogf,
              c_new, n_new, m_new, conv_new)
    return cat_a, cat_m, states


def kernel(x_prompt, x_sample, cache_fox_k, cache_fox_v, cache_fox_logf, state_mlstm_C, state_mlstm_n,
           state_mlstm_m, state_mlstm_conv, norm_mix_g, w_in, b_fox_f, conv_w, conv_b, b_ml_i, b_ml_f,
           g_fox, g_ml, w_out, norm_ffn_g, w_router, b_router, w_gate_up, b_gate_up, w_down, b_down,
           norm_final_g):
    depth = w_in.shape[0]
    assert depth == 1, "the final norm is fused into the last layer's combine; only depth 1 is wired up"
    bp, sp, d = x_prompt.shape
    bs, ss, _ = x_sample.shape
    n_exp = w_router.shape[-1]
    d_ff = w_down.shape[2]
    yp, ys = x_prompt, x_sample
    p_st, s_st = [], []

    for l in range(depth):
        wl = w_in[l]
        w = {
            "g_mix": norm_mix_g[l][None, :],
            "wbig": jnp.concatenate([wl[:, O_FQ:O_FF], wl[:, O_MQ:O_MI]], axis=1).astype(BF16),
            "wsm": jnp.pad(jnp.concatenate([wl[:, O_FF:O_MQ], wl[:, O_MI:P_IN]], axis=1),
                           ((0, 0), (0, V7X_LANES - S_END))).astype(BF16),
            "bsm": jnp.pad(jnp.concatenate([b_fox_f[l], b_ml_i[l], b_ml_f[l]]), (0, V7X_LANES - S_END))[None, :],
            "conv_w": conv_w[l],
            "conv_b": conv_b[l][None, :],
            "g_fox": g_fox[l][None, :],
            "g_ml": g_ml[l][None, :],
        }
        w_a = w_out[l][:FOX_W].astype(BF16)
        w_m = w_out[l][FOX_W:].astype(BF16)
        g_ffn = norm_ffn_g[l][None, :]
        wr = jnp.pad(w_router[l], ((0, 0), (0, V7X_LANES - n_exp)))
        wr_hi = wr.astype(BF16)
        wr_lo = (wr - wr_hi.astype(F32)).astype(BF16)
        br = jnp.pad(b_router[l], (0, V7X_LANES - n_exp), constant_values=NEG)[None, :]

        zeros_state = (jnp.zeros((bp, ML_HEADS, ML_DV, ML_DK), F32), jnp.zeros((bp, ML_HEADS, ML_DK), F32),
                       jnp.zeros((bp, ML_HEADS), F32))
        cat_a_p, cat_m_p, st_p = _mixer(yp, jnp.zeros((bp, CONV_W - 1, QK_W), F32), None, zeros_state, w,
                                        min(PROJ_ROWS, sp), min(FOX_QBLOCK, sp), min(FOX_BLOCK, sp), CHUNK)
        p_st.append(st_p)
        past = cache_fox_k.shape[2]
        tk_s = -(-(past + ss) // V7X_LANES) * V7X_LANES
        cat_a_s, cat_m_s, st_s = _mixer(
            ys, state_mlstm_conv[l], (cache_fox_k[l], cache_fox_v[l], cache_fox_logf[l]),
            (state_mlstm_C[l].astype(F32), state_mlstm_n[l].astype(F32), state_mlstm_m[l].astype(F32)),
            w, ss, ss, tk_s, ss)
        s_st.append(st_s)

        tp, ts = bp * sp, bs * ss
        y1_p, xn_p, ridx_p, rgate_p, cnt_p = _merge(
            cat_a_p.reshape(tp, FOX_W), cat_m_p.reshape(tp, ML_W), yp.reshape(tp, d), w_a, w_m, g_ffn,
            wr_hi, wr_lo, br, min(MERGE_ROWS, tp))
        y1_s, xn_s, ridx_s, rgate_s, cnt_s = _merge(
            cat_a_s.reshape(ts, FOX_W), cat_m_s.reshape(ts, ML_W), ys.reshape(ts, d), w_a, w_m, g_ffn,
            wr_hi, wr_lo, br, min(MERGE_ROWS, ts))

        cnt_p = cnt_p[0, :n_exp].astype(jnp.int32)
        cnt_s = cnt_s[0, :n_exp].astype(jnp.int32)
        seg_rows = -(-(cnt_p + cnt_s) // FFN_ROWS) * FFN_ROWS
        seg_end = jnp.cumsum(seg_rows)
        seg_start = seg_end - seg_rows
        n_rows = -(-((tp + ts) * TOP_K + n_exp * (FFN_ROWS - 1)) // FFN_ROWS) * FFN_ROWS
        n_tiles = n_rows // FFN_ROWS
        tile_row0 = jnp.arange(n_tiles, dtype=jnp.int32) * FFN_ROWS
        tile_expert = jnp.minimum(jnp.sum((seg_end[None, :] <= tile_row0[:, None]).astype(jnp.int32), axis=1),
                                  n_exp - 1)
        n_used = (seg_end[-1:] // FFN_ROWS).astype(jnp.int32)
        e_p, rank_p = ridx_p[:, :TOP_K], ridx_p[:, TOP_K:2 * TOP_K]
        e_s, rank_s = ridx_s[:, :TOP_K], ridx_s[:, TOP_K:2 * TOP_K]
        pos_p = seg_start[e_p] + rank_p
        pos_s = seg_start[e_s] + cnt_p[e_s] + rank_s
        tm_p, tm_s = min(MOE_ROWS, tp), min(MOE_ROWS, ts)
        pos3_p = pos_p.reshape(tp // tm_p, 1, TOP_K * tm_p)
        pos3_s = pos_s.reshape(ts // tm_s, 1, TOP_K * tm_s)

        xs = jnp.zeros((n_rows,) + _row_tile(d), F32)
        xs = _scatter(pos3_p, xn_p, xs, tm_p)
        xs = _scatter(pos3_s, xn_s, xs, tm_s)
        ysort = _ffn(tile_expert, n_used, xs, w_gate_up[l], b_gate_up[l].reshape(n_exp, 1, 2 * d_ff),
                     w_down[l], b_down[l].reshape(n_exp, 1, d), FFN_ROWS)

        g_fin = norm_final_g[None, :]
        yp = _combine(pos3_p, y1_p, rgate_p, g_fin, ysort, tm_p).reshape(bp, sp, d)
        ys = _combine(pos3_s, y1_s, rgate_s, g_fin, ysort, tm_s).reshape(bs, ss, d)

    p_out = tuple(jnp.stack(a) for a in zip(*p_st))
    s_out = tuple(jnp.stack(a) for a in zip(*s_st))
    return (yp, ys) + p_out + s_out
```

```python
import functools
import math

import numpy as np

import jax
import jax.numpy as jnp
from jax import lax
from jax.experimental import pallas as pl
from jax.experimental.pallas import tpu as pltpu

F32 = jnp.float32
BF16 = jnp.bfloat16

FOX_HEADS = 8
FOX_HD = 64
FOX_W = FOX_HEADS * FOX_HD
ML_HEADS = 4
ML_DK = 128
ML_DV = 128
ML_W = ML_HEADS * ML_DV
QK_W = 2 * ML_HEADS * ML_DK
CONV_W = 4
CHUNK = 64
TOP_K = 4
SWIGLU_LIMIT = 7.0
SWIGLU_ALPHA = 1.702
EPS = 1e-6
NEG = -1e30
LOG2E = math.log2(math.e)

O_FQ = 0
O_FK = O_FQ + FOX_W
O_FV = O_FK + FOX_W
O_FF = O_FV + FOX_W
O_MQ = O_FF + FOX_HEADS
O_MK = O_MQ + ML_HEADS * ML_DK
O_MV = O_MK + ML_HEADS * ML_DK
O_MO = O_MV + ML_W
O_MI = O_MO + ML_W
O_MF = O_MI + ML_HEADS
P_IN = O_MF + ML_HEADS

B_FQ, B_FK, B_FV, B_QK, B_MV, B_MO, B_END = 0, 512, 1024, 1536, 2560, 3072, 3584
S_FF, S_MI, S_MF, S_END = 0, 8, 12, 16
BIAS_TERMS = 3

V7X_LANES = 128
V7X_SUBLANES = 8
V7X_VMEM_LIMIT_BYTES = 56 * 1024 * 1024

PROJ_ROWS = 512
FOX_BLOCK = 512
FOX_QBLOCK = 1024
FOX_SUB = 512
MERGE_ROWS = 512
MOE_ROWS = 512
FFN_ROWS = 512
DMA_UNROLL = 8


def _params(*semantics):
    return pltpu.CompilerParams(dimension_semantics=semantics, vmem_limit_bytes=V7X_VMEM_LIMIT_BYTES)


def _dot(a, b):
    return jnp.dot(a, b, preferred_element_type=F32)


def _dot_nt(a, b):
    return lax.dot_general(a, b, (((1,), (1,)), ((), ())), preferred_element_type=F32)


def _mxu_transpose(eye, x):
    return _dot_nt(eye, x)


def _sigmoid(x):
    return 1.0 / (1.0 + jnp.exp(-x))


def _log_sigmoid(x):
    return jnp.minimum(x, 0.0) - jnp.log1p(jnp.exp(-jnp.abs(x)))


def _split3(x):
    t1 = x.astype(BF16)
    r1 = x - t1.astype(F32)
    t2 = r1.astype(BF16)
    t3 = (r1 - t2.astype(F32)).astype(BF16)
    return t1, t2, t3


def _lower_tri(n):
    r = lax.broadcasted_iota(jnp.int32, (n, n), 0)
    c = lax.broadcasted_iota(jnp.int32, (n, n), 1)
    return c <= r


def _proj_kernel(x_ref, g_ref, wbig_ref, wsm_ref, bsm_ref, cw_ref, cb_ref, hist_ref,
                 fq_ref, fk32_ref, fv32_ref, fk16_ref, fv16_ref, sm_ref, flogf_ref, mq_ref, mk_ref, mv_ref,
                 mo_ref, cnew_ref, halo_ref):
    tm = x_ref.shape[1]

    @pl.when(pl.program_id(1) == 0)
    def _():
        halo_ref[...] = hist_ref[0]

    x = x_ref[0]
    xn = (x * lax.rsqrt(jnp.mean(x * x, axis=-1, keepdims=True) + EPS)) * g_ref[...]
    xb = xn.astype(BF16)
    z = _dot(xb, wbig_ref[...])

    fq_ref[0] = (z[:, B_FQ:B_FK] * (FOX_HD ** -0.5 * LOG2E)).astype(BF16)
    fk = z[:, B_FK:B_FV]
    fv = z[:, B_FV:B_QK]
    fk32_ref[0] = fk
    fv32_ref[0] = fv
    fk16_ref[0] = fk.astype(BF16)
    fv16_ref[0] = fv.astype(BF16)
    mv_ref[0] = z[:, B_MV:B_MO].astype(BF16)
    mo_ref[0] = _sigmoid(z[:, B_MO:B_END]).astype(BF16)

    zs = _dot(xb, wsm_ref[...]) + bsm_ref[...]
    lane = lax.broadcasted_iota(jnp.int32, zs.shape, 1)
    is_forget = (lane < S_MI) | ((lane >= S_MF) & (lane < S_END))
    sm = jnp.where(is_forget, _log_sigmoid(zs), zs)
    sm_ref[0] = sm
    flogf_ref[0] = sm[:, S_FF:S_MI]

    u = z[:, B_QK:B_MV]
    up = jnp.concatenate([halo_ref[...], u], axis=0)
    n = tm + V7X_SUBLANES
    first = V7X_SUBLANES - (CONV_W - 1)
    y = cb_ref[...] + cw_ref[0:1, :] * pltpu.roll(up, n - first, axis=0)[:tm]
    for j in range(1, CONV_W - 1):
        y = y + cw_ref[j:j + 1, :] * pltpu.roll(up, n - (first + j), axis=0)[:tm]
    y = y + cw_ref[CONV_W - 1:CONV_W, :] * u
    qk = y * _sigmoid(y)
    mq_ref[0] = qk[:, :QK_W // 2].astype(BF16)
    mk_ref[0] = (qk[:, QK_W // 2:] * (ML_DK ** -0.5)).astype(BF16)

    halo_ref[...] = u[tm - V7X_SUBLANES:, :]
    cnew_ref[0] = halo_ref[first:, :]


def _project(x, hist8, g, wbig, wsm, bsm, cw, cb, tm):
    bsz, seq, d = x.shape
    grid = (bsz, seq // tm)
    row = lambda c: pl.BlockSpec((1, tm, c), lambda b, s: (b, s, 0))
    const = lambda shape: pl.BlockSpec(shape, lambda b, s: (0,) * len(shape))
    outs = [
        (FOX_W, BF16), (FOX_W, F32), (FOX_W, F32), (FOX_W, BF16), (FOX_W, BF16), (V7X_LANES, F32),
        (FOX_HEADS, F32), (ML_W, BF16), (ML_W, BF16), (ML_W, BF16), (ML_W, BF16),
    ]
    out_shape = [jax.ShapeDtypeStruct((bsz, seq, c), dt) for c, dt in outs]
    out_specs = [row(c) for c, _ in outs]
    out_shape.append(jax.ShapeDtypeStruct((bsz, CONV_W - 1, QK_W), F32))
    out_specs.append(pl.BlockSpec((1, CONV_W - 1, QK_W), lambda b, s: (b, 0, 0)))
    return pl.pallas_call(
        _proj_kernel,
        grid=grid,
        in_specs=[row(d), const((1, d)), const(wbig.shape), const(wsm.shape), const((1, V7X_LANES)),
                  const((CONV_W, QK_W)), const((1, QK_W)),
                  pl.BlockSpec((1, V7X_SUBLANES, QK_W), lambda b, s: (b, 0, 0))],
        out_specs=out_specs,
        out_shape=out_shape,
        scratch_shapes=[pltpu.VMEM((V7X_SUBLANES, QK_W), F32)],
        compiler_params=_params("arbitrary", "arbitrary"),
        name="proj",
    )(x, g, wbig, wsm, bsm, cw, cb, hist8)


def _bias_lane(head, term):
    return (head // 2) * V7X_LANES + (FOX_HD if head % 2 == 0 else 0) + term


def _bias_placement():
    place = np.zeros((BIAS_TERMS, V7X_LANES, FOX_W), np.float32)
    for h in range(FOX_HEADS):
        for t in range(BIAS_TERMS):
            place[t, S_FF + h, _bias_lane(h, t)] = 1.0
    return jnp.asarray(place, BF16)


def _fox_prep_kernel(lf_ref, k_ref, v_ref, place_ref, ka0_ref, ka1_ref, va0_ref, va1_ref, carry_ref):
    tm = lf_ref.shape[1]

    @pl.when(pl.program_id(1) == 0)
    def _():
        carry_ref[...] = jnp.zeros_like(carry_ref)

    tri = jnp.where(_lower_tri(tm), 1.0, 0.0).astype(BF16)
    t1, t2, t3 = _split3(lf_ref[0])
    c = carry_ref[...] + (_dot(tri, t1) + (_dot(tri, t2) + _dot(tri, t3)))
    carry_ref[...] = c[tm - 1:tm, :]
    b1, b2, b3 = _split3(c * (-LOG2E))
    kb = _dot(b1, place_ref[0]) + (_dot(b2, place_ref[1]) + _dot(b3, place_ref[2]))

    k = k_ref[0].astype(F32)
    v = v_ref[0].astype(F32)
    first_head = lax.broadcasted_iota(jnp.int32, k.shape, 1) % V7X_LANES < FOX_HD
    ka0_ref[0] = jnp.where(first_head, k, kb).astype(BF16)
    ka1_ref[0] = jnp.where(first_head, kb, k).astype(BF16)
    va0_ref[0] = jnp.where(first_head, v, 1.0).astype(BF16)
    va1_ref[0] = jnp.where(first_head, 1.0, v).astype(BF16)


def _fox_prep(lf, k, v, tm):
    bsz, lk, _ = lf.shape
    place = _bias_placement()
    wide = pl.BlockSpec((1, tm, FOX_W), lambda b, s: (b, s, 0))
    sds = jax.ShapeDtypeStruct((bsz, lk, FOX_W), BF16)
    return pl.pallas_call(
        _fox_prep_kernel,
        grid=(bsz, lk // tm),
        in_specs=[pl.BlockSpec((1, tm, V7X_LANES), lambda b, s: (b, s, 0)), wide, wide,
                  pl.BlockSpec(place.shape, lambda b, s: (0, 0, 0))],
        out_specs=[wide, wide, wide, wide],
        out_shape=[sds, sds, sds, sds],
        scratch_shapes=[pltpu.VMEM((1, V7X_LANES), F32)],
        compiler_params=_params("arbitrary", "arbitrary"),
        name="fox_prep",
    )(lf, k, v, place)


def _fox_kernel(q_ref, ka0_ref, ka1_ref, va0_ref, va1_ref, g_ref, o_ref, *, tq, tk, sub, q_offset):
    ts = min(tq, sub)
    n_sub = tq // ts
    diag_blocks = max(ts // tk, 1)
    k_refs = (ka0_ref, ka1_ref)
    v_refs = (va0_ref, va1_ref)
    lane = lax.broadcasted_iota(jnp.int32, (ts, V7X_LANES), 1)
    first_head = lane < FOX_HD
    q_lo = q_offset + pl.program_id(2) * tq
    n_common = (q_lo + 1) // tk
    single_block = ka0_ref.shape[1] == tk
    even_common = q_offset % (2 * tk) == 0 and tq % (2 * tk) == 0

    def chain_q(r, hh):
        q = q_ref[0, r * ts:(r + 1) * ts, :].astype(F32)
        if hh == 0:
            return jnp.where(first_head, q, jnp.where(lane < FOX_HD + BIAS_TERMS, 1.0, 0.0)).astype(BF16)
        return jnp.where(first_head, jnp.where(lane < BIAS_TERMS, 1.0, 0.0), q).astype(BF16)

    chains = [(r, hh) for r in range(n_sub) for hh in range(2)]
    qs = [chain_q(r, hh) for r, hh in chains]

    def scores(c, j):
        return _dot_nt(qs[c], k_refs[chains[c][1]][0, pl.ds(pl.multiple_of(j * tk, tk), tk), :])

    def update(state, c, j, s, masked):
        r, hh = chains[c]
        m, acc = state
        start = pl.multiple_of(j * tk, tk)
        if masked:
            qpos = q_lo + r * ts + lax.broadcasted_iota(jnp.int32, s.shape, 0)
            kpos = start + lax.broadcasted_iota(jnp.int32, s.shape, 1)
            s = jnp.where(kpos <= qpos, s, NEG)
        m_new = jnp.maximum(m, jnp.max(s, axis=-1, keepdims=True))
        p = jnp.exp2(s - m_new).astype(BF16)
        acc = jnp.exp2(m - m_new) * acc + _dot(p, v_refs[hh][0, pl.ds(start, tk), :])
        return m_new, acc

    def step(state, c, j, masked):
        return update(state, c, j, scores(c, j), masked)

    def common(j, states):
        ss = [scores(c, j) for c in range(len(chains))]
        return tuple(update(st, c, j, ss[c], False) for c, st in enumerate(states))

    def common_pair(j2, states):
        states = list(states)
        s0 = [scores(c, 2 * j2) for c in range(len(chains))]
        s1 = []
        for c in range(len(chains)):
            states[c] = update(states[c], c, 2 * j2, s0[c], False)
            s1.append(scores(c, 2 * j2 + 1))
        return tuple(update(st, c, 2 * j2 + 1, s1[c], False) for c, st in enumerate(states))

    init = (jnp.full((ts, 1), NEG, F32), jnp.zeros((ts, V7X_LANES), F32))
    states = (init,) * len(chains)
    if not single_block:
        states = lax.fori_loop(0, n_common // 2, common_pair, states)
        if not even_common:
            states = lax.fori_loop(2 * (n_common // 2), n_common, common, states)
    states = list(states)
    for c, (r, _) in enumerate(chains):
        for e in range(r * diag_blocks):
            states[c] = step(states[c], c, n_common + e, False)
        for e in range(diag_blocks):
            states[c] = step(states[c], c, n_common + r * diag_blocks + e, True)

    for r in range(n_sub):
        out = jnp.zeros((ts, V7X_LANES), F32)
        for hh in range(2):
            acc = states[chains.index((r, hh))][1]
            own = first_head if hh == 0 else jnp.logical_not(first_head)
            denom_lane = FOX_HD if hh == 0 else 0
            o = jnp.where(own, acc / acc[:, denom_lane:denom_lane + 1], 0.0)
            out = out + o * lax.rsqrt(jnp.sum(o * o, axis=-1, keepdims=True) * (1.0 / FOX_HD) + EPS)
        o_ref[0, r * ts:(r + 1) * ts, :] = (out * g_ref[...]).astype(BF16)


def _fox(q, ka0, ka1, va0, va1, g_fox, q_offset, tq, tk):
    bsz, lq, _ = q.shape
    lk = ka0.shape[1]
    assert (q_offset % tk == 0 and tq % tk == 0) or (lq == tq <= tk and lk == tk), (q_offset, tq, tk, lq, lk)
    pairs = FOX_HEADS // 2
    grid = (bsz, pairs, lq // tq)
    kv_spec = pl.BlockSpec((1, lk, V7X_LANES), lambda b, p, i: (b, 0, p))
    q_spec = pl.BlockSpec((1, tq, V7X_LANES), lambda b, p, i: (b, i, p))
    return pl.pallas_call(
        functools.partial(_fox_kernel, tq=tq, tk=tk, sub=FOX_SUB, q_offset=q_offset),
        grid=grid,
        in_specs=[q_spec, kv_spec, kv_spec, kv_spec, kv_spec,
                  pl.BlockSpec((1, V7X_LANES), lambda b, p, i: (0, p))],
        out_specs=q_spec,
        out_shape=jax.ShapeDtypeStruct((bsz, lq, FOX_W), BF16),
        compiler_params=_params("arbitrary", "arbitrary", "arbitrary"),
        name="fox_attention",
    )(q, ka0, ka1, va0, va1, g_fox)


def _mlstm_kernel(q_ref, k_ref, v_ref, mo_ref, sm_ref, c0_ref, n0_ref, m0_ref, g_ref,
                  o_ref, c_ref, n_ref, m_ref):
    bsz, L, _ = q_ref.shape

    @pl.when(pl.program_id(0) == 0)
    def _():
        c_ref[...] = c0_ref[...]
        n_ref[...] = n0_ref[...]
        m_ref[...] = m0_ref[...]

    causal = _lower_tri(L)
    tri = jnp.where(causal, 1.0, 0.0).astype(BF16)
    eye = jnp.where(lax.broadcasted_iota(jnp.int32, (V7X_LANES, V7X_LANES), 0)
                    == lax.broadcasted_iota(jnp.int32, (V7X_LANES, V7X_LANES), 1), 1.0, 0.0).astype(BF16)
    rows = lax.broadcasted_iota(jnp.int32, (L, V7X_LANES), 0)
    lanes = lax.broadcasted_iota(jnp.int32, (L, V7X_LANES), 1)
    gate_lanes = (lanes >= S_MI) & (lanes < S_END)

    heads = [(b, h) for b in range(bsz) for h in range(ML_HEADS)]
    cols = lambda h: slice(h * ML_DK, (h + 1) * ML_DK)

    qk, qc, n_prev = {}, {}, {}
    for b, h in heads:
        qh = q_ref[b, :, cols(h)]
        qk[b, h] = _dot_nt(qh, k_ref[b, :, cols(h)])
        qc[b, h] = _dot_nt(qh, c_ref[b, h].astype(BF16))

    gates = {}
    for b in range(bsz):
        sm = jnp.where(gate_lanes, sm_ref[b], 0.0)
        t1, t2, t3 = _split3(sm)
        bcum = _dot(tri, t1) + (_dot(tri, t2) + _dot(tri, t3))
        bcum = pltpu.roll(bcum, V7X_LANES - (S_MF - S_MI), axis=1)
        g = sm - bcum
        gmax = g
        sh = 1
        while sh < L:
            gmax = jnp.maximum(gmax, jnp.where(rows >= sh, pltpu.roll(gmax, sh, axis=0), -jnp.inf))
            sh *= 2
        m = m_ref[pl.ds(b, 1), :]
        u = jnp.maximum(m, gmax)
        a = jnp.exp(m - u)
        mt = bcum + u
        m_ref[pl.ds(b, 1), :] = mt[L - 1:L, :]
        g1, g2, g3 = _split3(g)
        g_rows = _mxu_transpose(eye, g1) + (_mxu_transpose(eye, g2) + _mxu_transpose(eye, g3))
        gates[b] = dict(u=u, a=a, em=jnp.exp(-mt), a_last=a[L - 1:L, :], wcol=jnp.exp(g - u[L - 1:L, :]),
                        g_rows=g_rows)

    for b, h in heads:
        gl = S_MI + h
        kh = k_ref[b, :, cols(h)]
        w_h = gates[b]["wcol"][:, gl:gl + 1]
        a_l = gates[b]["a_last"][:, gl:gl + 1]
        vw = (v_ref[b, :, cols(h)].astype(F32) * w_h).astype(BF16)
        n_old = n_ref[b, h:h + 1, :]
        n_prev[b, h] = n_old
        c_ref[b, h] = a_l * c_ref[b, h] + _dot(_mxu_transpose(eye, vw).astype(BF16), kh)
        n_ref[b, h:h + 1, :] = a_l * n_old + jnp.sum(kh.astype(F32) * w_h, axis=0, keepdims=True)

    for b, h in heads:
        gl = S_MI + h
        gb = gates[b]
        decay = jnp.exp(jnp.where(causal, gb["g_rows"][gl:gl + 1, :] - gb["u"][:, gl:gl + 1], NEG))
        s = qk[b, h] * decay
        a_h = gb["a"][:, gl:gl + 1]
        num = a_h * qc[b, h] + _dot(s.astype(BF16), v_ref[b, :, cols(h)])
        den = (a_h * jnp.sum(q_ref[b, :, cols(h)].astype(F32) * n_prev[b, h], axis=-1, keepdims=True)
               + jnp.sum(s, axis=-1, keepdims=True))
        hv = num / jnp.maximum(jnp.abs(den), gb["em"][:, gl:gl + 1])
        hn = hv * lax.rsqrt(jnp.mean(hv * hv, axis=-1, keepdims=True) + EPS)
        o_ref[b, :, cols(h)] = (hn * g_ref[:, cols(h)] * mo_ref[b, :, cols(h)].astype(F32)).astype(BF16)


def _mlstm(mq, mk, mv, mo, small, c0, n0, m0, g_ml, L):
    bsz, seq, _ = mq.shape
    chunk = pl.BlockSpec((bsz, L, ML_W), lambda c: (0, c, 0))
    full = lambda shape: pl.BlockSpec(shape, lambda c: (0,) * len(shape))
    return pl.pallas_call(
        _mlstm_kernel,
        grid=(seq // L,),
        in_specs=[chunk, chunk, chunk, chunk,
                  pl.BlockSpec((bsz, L, V7X_LANES), lambda c: (0, c, 0)),
                  full(c0.shape), full(n0.shape), full(m0.shape), full((1, ML_W))],
        out_specs=[chunk, full(c0.shape), full(n0.shape), full(m0.shape)],
        out_shape=[jax.ShapeDtypeStruct((bsz, seq, ML_W), BF16),
                   jax.ShapeDtypeStruct(c0.shape, F32),
                   jax.ShapeDtypeStruct(n0.shape, F32),
                   jax.ShapeDtypeStruct(m0.shape, F32)],
        compiler_params=_params("arbitrary"),
        name="mlstm",
    )(mq, mk, mv, mo, small, c0, n0, m0, g_ml)


def _merge_kernel(ca_ref, cm_ref, x_ref, wa_ref, wm_ref, g_ref, wrh_ref, wrl_ref, br_ref,
                  y1_ref, xn_ref, ridx_ref, rgate_ref, cnt_ref):
    tm = x_ref.shape[0]

    @pl.when(pl.program_id(0) == 0)
    def _():
        cnt_ref[...] = jnp.zeros_like(cnt_ref)

    y1 = x_ref[...] + (_dot(ca_ref[...], wa_ref[...]) + _dot(cm_ref[...], wm_ref[...]))
    y1_ref[...] = y1
    xn = (y1 * lax.rsqrt(jnp.mean(y1 * y1, axis=-1, keepdims=True) + EPS)) * g_ref[...]
    xn_ref[...] = xn.reshape(xn_ref.shape)

    xh = xn.astype(BF16)
    xl = (xn - xh.astype(F32)).astype(BF16)
    logits = (_dot(xh, wrh_ref[...]) + (_dot(xl, wrh_ref[...]) + _dot(xh, wrl_ref[...]))) + br_ref[...]

    lane = lax.broadcasted_iota(jnp.int32, logits.shape, 1)
    vals, sel = [], []
    work = logits
    for _ in range(TOP_K):
        mx = jnp.max(work, axis=-1, keepdims=True)
        idx = jnp.min(jnp.where(work == mx, lane, V7X_LANES), axis=-1, keepdims=True)
        vals.append(mx)
        sel.append(idx)
        work = jnp.where(lane == idx, -jnp.inf, work)
    ex = [jnp.exp(v - vals[0]) for v in vals]
    tot = ex[0] + ex[1] + ex[2] + ex[3]

    onehot = [(lane == idx) for idx in sel]
    picked = jnp.where(onehot[0] | onehot[1] | onehot[2] | onehot[3], 1.0, 0.0)
    earlier = (lax.broadcasted_iota(jnp.int32, (tm, tm), 1) < lax.broadcasted_iota(jnp.int32, (tm, tm), 0))
    before = _dot(jnp.where(earlier, 1.0, 0.0).astype(BF16), picked.astype(BF16)) + cnt_ref[...]
    cnt_ref[...] = cnt_ref[...] + jnp.sum(picked, axis=0, keepdims=True)

    ridx = jnp.zeros(logits.shape, jnp.int32)
    rgate = jnp.zeros(logits.shape, F32)
    for k in range(TOP_K):
        rank = jnp.sum(jnp.where(onehot[k], before, 0.0), axis=-1, keepdims=True).astype(jnp.int32)
        ridx = jnp.where(lane == k, sel[k], ridx)
        ridx = jnp.where(lane == TOP_K + k, rank, ridx)
        rgate = jnp.where(lane == k, ex[k] / tot, rgate)
    ridx_ref[...] = ridx
    rgate_ref[...] = rgate


def _merge(cat_a, cat_m, x, w_a, w_m, g_ffn, wr_hi, wr_lo, br, tm):
    t, d = x.shape
    row = lambda c: pl.BlockSpec((tm, c), lambda i: (i, 0))
    const = lambda shape: pl.BlockSpec(shape, lambda i: (0,) * len(shape))
    return pl.pallas_call(
        _merge_kernel,
        grid=(t // tm,),
        in_specs=[row(FOX_W), row(ML_W), row(d), const(w_a.shape), const(w_m.shape), const((1, d)),
                  const(wr_hi.shape), const(wr_lo.shape), const((1, V7X_LANES))],
        out_specs=[row(d), pl.BlockSpec((tm,) + _row_tile(d), lambda i: (i, 0, 0)), row(V7X_LANES),
                   row(V7X_LANES), const((1, V7X_LANES))],
        out_shape=[jax.ShapeDtypeStruct((t, d), F32), jax.ShapeDtypeStruct((t,) + _row_tile(d), F32),
                   jax.ShapeDtypeStruct((t, V7X_LANES), jnp.int32),
                   jax.ShapeDtypeStruct((t, V7X_LANES), F32),
                   jax.ShapeDtypeStruct((1, V7X_LANES), F32)],
        compiler_params=_params("arbitrary"),
        name="merge_router",
    )(cat_a, cat_m, x, w_a, w_m, g_ffn, wr_hi, wr_lo, br)


def _row_tile(d):
    return (d // V7X_LANES, V7X_LANES)


def _row_copy(src_ref, src_row, dst_ref, dst_row, sem):
    return pltpu.make_async_copy(src_ref.at[pl.ds(src_row, 1)], dst_ref.at[pl.ds(dst_row, 1)], sem)


def _scatter_kernel(pos_ref, x_ref, xs_in_ref, xs_ref, sem):
    del xs_in_ref
    tm = x_ref.shape[0]

    def issue(g, _):
        for u in range(DMA_UNROLL):
            r = g * DMA_UNROLL + u
            for k in range(TOP_K):
                _row_copy(x_ref, r, xs_ref, pos_ref[0, 0, TOP_K * r + k], sem).start(priority=k % 2)
        return 0

    def drain(g, _):
        for _ in range(DMA_UNROLL * TOP_K):
            _row_copy(x_ref, 0, xs_ref, 0, sem).wait()
        return 0

    lax.fori_loop(0, tm // DMA_UNROLL, issue, 0)
    lax.fori_loop(0, tm // DMA_UNROLL, drain, 0)


def _scatter(pos3, x, xs, tm):
    t, sub, lanes = x.shape
    return pl.pallas_call(
        _scatter_kernel,
        grid=(t // tm,),
        in_specs=[pl.BlockSpec((1, 1, TOP_K * tm), lambda i: (i, 0, 0), memory_space=pltpu.SMEM),
                  pl.BlockSpec((tm, sub, lanes), lambda i: (i, 0, 0)),
                  pl.BlockSpec(memory_space=pl.ANY)],
        out_specs=pl.BlockSpec(memory_space=pl.ANY),
        out_shape=jax.ShapeDtypeStruct(xs.shape, xs.dtype),
        scratch_shapes=[pltpu.SemaphoreType.DMA(())],
        input_output_aliases={2: 0},
        compiler_params=_params("arbitrary"),
        name="moe_scatter",
    )(pos3, x, xs)


def _ffn_kernel(te_ref, nu_ref, x_ref, wgu_ref, bgu_ref, wd_ref, bd_ref, y_ref, wgu16_ref, wd16_ref):
    i = pl.program_id(0)
    d_ff = wd_ref.shape[1]

    @pl.when((i == 0) | (te_ref[i] != te_ref[jnp.maximum(i - 1, 0)]))
    def _():
        wgu16_ref[...] = wgu_ref[0].astype(BF16)
        wd16_ref[...] = wd_ref[0].astype(BF16)

    @pl.when(i < nu_ref[0])
    def _():
        x = x_ref[...].reshape(x_ref.shape[0], -1)
        gu = _dot(x.astype(BF16), wgu16_ref[...]) + bgu_ref[0]
        gate = jnp.minimum(gu[:, :d_ff], SWIGLU_LIMIT)
        up = jnp.clip(gu[:, d_ff:], -SWIGLU_LIMIT, SWIGLU_LIMIT)
        h = (up + 1.0) * (gate * _sigmoid(SWIGLU_ALPHA * gate))
        y = _dot(h.astype(BF16), wd16_ref[...]) + bd_ref[0]
        y_ref[...] = y.reshape(y_ref.shape)

    @pl.when(i >= nu_ref[0])
    def _():
        y_ref[...] = jnp.zeros_like(y_ref)


def _ffn(tile_expert, n_used, xs, w_gu, b_gu, w_d, b_d, tm):
    r = xs.shape[0]
    d_ff, d = w_d.shape[1:]
    tiles = pl.BlockSpec((tm,) + _row_tile(d), lambda i, te, nu: (i, 0, 0))
    grid_spec = pltpu.PrefetchScalarGridSpec(
        num_scalar_prefetch=2,
        grid=(r // tm,),
        in_specs=[tiles,
                  pl.BlockSpec((1, d, 2 * d_ff), lambda i, te, nu: (te[i], 0, 0)),
                  pl.BlockSpec((1, 1, 2 * d_ff), lambda i, te, nu: (te[i], 0, 0)),
                  pl.BlockSpec((1, d_ff, d), lambda i, te, nu: (te[i], 0, 0)),
                  pl.BlockSpec((1, 1, d), lambda i, te, nu: (te[i], 0, 0))],
        out_specs=tiles,
        scratch_shapes=[pltpu.VMEM((d, 2 * d_ff), BF16), pltpu.VMEM((d_ff, d), BF16)],
    )
    return pl.pallas_call(
        _ffn_kernel,
        grid_spec=grid_spec,
        out_shape=jax.ShapeDtypeStruct(xs.shape, F32),
        compiler_params=_params("arbitrary"),
        name="moe_ffn",
    )(tile_expert, n_used, xs, w_gu, b_gu, w_d, b_d)


def _combine_kernel(pos_ref, y1_ref, gate_ref, gfin_ref, ys_ref, o_ref, buf_ref, sem):
    tm = y1_ref.shape[0]

    def issue(g, _):
        for u in range(DMA_UNROLL):
            r = g * DMA_UNROLL + u
            for k in range(TOP_K):
                _row_copy(ys_ref, pos_ref[0, 0, TOP_K * r + k], buf_ref.at[k], r, sem).start(priority=k % 2)
        return 0

    def drain(g, _):
        for _ in range(DMA_UNROLL * TOP_K):
            _row_copy(ys_ref, 0, buf_ref.at[0], 0, sem).wait()
        return 0

    lax.fori_loop(0, tm // DMA_UNROLL, issue, 0)
    lax.fori_loop(0, tm // DMA_UNROLL, drain, 0)

    gate = gate_ref[...]
    moe = gate[:, 0:1] * buf_ref[0].reshape(tm, -1)
    for k in range(1, TOP_K):
        moe = moe + gate[:, k:k + 1] * buf_ref[k].reshape(tm, -1)
    y = y1_ref[...] + moe
    o_ref[...] = (y * lax.rsqrt(jnp.mean(y * y, axis=-1, keepdims=True) + EPS)) * gfin_ref[...]


def _combine(pos3, y1, rgate, g_final, ys, tm):
    t, d = y1.shape
    return pl.pallas_call(
        _combine_kernel,
        grid=(t // tm,),
        in_specs=[pl.BlockSpec((1, 1, TOP_K * tm), lambda i: (i, 0, 0), memory_space=pltpu.SMEM),
                  pl.BlockSpec((tm, d), lambda i: (i, 0)),
                  pl.BlockSpec((tm, V7X_LANES), lambda i: (i, 0)),
                  pl.BlockSpec((1, d), lambda i: (0, 0)),
                  pl.BlockSpec(memory_space=pl.ANY)],
        out_specs=pl.BlockSpec((tm, d), lambda i: (i, 0)),
        out_shape=jax.ShapeDtypeStruct((t, d), F32),
        scratch_shapes=[pltpu.VMEM((TOP_K, tm) + _row_tile(d), F32), pltpu.SemaphoreType.DMA(())],
        compiler_params=_params("arbitrary"),
        name="moe_combine",
    )(pos3, y1, rgate, g_final, ys)


def _mixer(x, hist, fox_cache, ml_state, w, tm, tq, tk, chunk):
    bsz, seq, _ = x.shape
    hist8 = jnp.pad(hist, ((0, 0), (V7X_SUBLANES - (CONV_W - 1), 0), (0, 0)))
    (fq, fk32, fv32, fk16, fv16, small, flogf, mq, mk, mv, mo, conv_new) = _project(
        x, hist8, w["g_mix"], w["wbig"], w["wsm"], w["bsm"], w["conv_w"], w["conv_b"], tm)

    if fox_cache is None:
        q_offset = 0
        lf_all, k_all, v_all = small, fk16, fv16
    else:
        ck_c, cv_c, clogf_c = fox_cache
        q_offset = ck_c.shape[1]
        clogf_c = jnp.pad(clogf_c, ((0, 0), (0, 0), (S_FF, V7X_LANES - S_FF - FOX_HEADS)))
        lf_all = jnp.concatenate([clogf_c, small], axis=1)
        k_all = jnp.concatenate([ck_c.reshape(bsz, q_offset, FOX_W).astype(BF16), fk16], axis=1)
        v_all = jnp.concatenate([cv_c.reshape(bsz, q_offset, FOX_W).astype(BF16), fv16], axis=1)
    lk = k_all.shape[1]
    lk_pad = -(-lk // tk) * tk
    pad_rows = ((0, 0), (0, lk_pad - lk), (0, 0))
    lf_all, k_all, v_all = jnp.pad(lf_all, pad_rows), jnp.pad(k_all, pad_rows), jnp.pad(v_all, pad_rows)
    ka0, ka1, va0, va1 = _fox_prep(lf_all, k_all, v_all, math.gcd(tk, FOX_BLOCK))
    cat_a = _fox(fq, ka0, ka1, va0, va1, w["g_fox"], q_offset, tq, tk)

    c0, n0, m0 = ml_state
    m0 = jnp.pad(m0, ((0, 0), (S_MI, V7X_LANES - S_MI - ML_HEADS)))
    cat_m, c_new, n_new, m_new = _mlstm(mq, mk, mv, mo, small, c0, n0, m0, w["g_ml"], chunk)
    m_new = m_new[:, S_MI:S_MI + ML_HEADS]

    states = (fk32.reshape(bsz, seq, FOX_HEADS, FOX_HD), fv32.reshape(bsz, seq, FOX_HEADS, FOX_HD), flogf,
              c_new, n_new, m_new, conv_new)
    return cat_a, cat_m, states


def kernel(x_prompt, x_sample, cache_fox_k, cache_fox_v, cache_fox_logf, state_mlstm_C, state_mlstm_n,
           state_mlstm_m, state_mlstm_conv, norm_mix_g, w_in, b_fox_f, conv_w, conv_b, b_ml_i, b_ml_f,
           g_fox, g_ml, w_out, norm_ffn_g, w_router, b_router, w_gate_up, b_gate_up, w_down, b_down,
           norm_final_g):
    depth = w_in.shape[0]
    assert depth == 1, "the final norm is fused into the last layer's combine; only depth 1 is wired up"
    bp, sp, d = x_prompt.shape
    bs, ss, _ = x_sample.shape
    n_exp = w_router.shape[-1]
    d_ff = w_down.shape[2]
    yp, ys = x_prompt, x_sample
    p_st, s_st = [], []

    for l in range(depth):
        wl = w_in[l]
        w = {
            "g_mix": norm_mix_g[l][None, :],
            "wbig": jnp.concatenate([wl[:, O_FQ:O_FF], wl[:, O_MQ:O_MI]], axis=1).astype(BF16),
            "wsm": jnp.pad(jnp.concatenate([wl[:, O_FF:O_MQ], wl[:, O_MI:P_IN]], axis=1),
                           ((0, 0), (0, V7X_LANES - S_END))).astype(BF16),
            "bsm": jnp.pad(jnp.concatenate([b_fox_f[l], b_ml_i[l], b_ml_f[l]]), (0, V7X_LANES - S_END))[None, :],
            "conv_w": conv_w[l],
            "conv_b": conv_b[l][None, :],
            "g_fox": g_fox[l][None, :],
            "g_ml": g_ml[l][None, :],
        }
        w_a = w_out[l][:FOX_W].astype(BF16)
        w_m = w_out[l][FOX_W:].astype(BF16)
        g_ffn = norm_ffn_g[l][None, :]
        wr = jnp.pad(w_router[l], ((0, 0), (0, V7X_LANES - n_exp)))
        wr_hi = wr.astype(BF16)
        wr_lo = (wr - wr_hi.astype(F32)).astype(BF16)
        br = jnp.pad(b_router[l], (0, V7X_LANES - n_exp), constant_values=NEG)[None, :]

        zeros_state = (jnp.zeros((bp, ML_HEADS, ML_DV, ML_DK), F32), jnp.zeros((bp, ML_HEADS, ML_DK), F32),
                       jnp.zeros((bp, ML_HEADS), F32))
        cat_a_p, cat_m_p, st_p = _mixer(yp, jnp.zeros((bp, CONV_W - 1, QK_W), F32), None, zeros_state, w,
                                        min(PROJ_ROWS, sp), min(FOX_QBLOCK, sp), min(FOX_BLOCK, sp), CHUNK)
        p_st.append(st_p)
        past = cache_fox_k.shape[2]
        tk_s = -(-(past + ss) // V7X_LANES) * V7X_LANES
        cat_a_s, cat_m_s, st_s = _mixer(
            ys, state_mlstm_conv[l], (cache_fox_k[l], cache_fox_v[l], cache_fox_logf[l]),
            (state_mlstm_C[l].astype(F32), state_mlstm_n[l].astype(F32), state_mlstm_m[l].astype(F32)),
            w, ss, ss, tk_s, ss)
        s_st.append(st_s)

        tp, ts = bp * sp, bs * ss
        y1_p, xn_p, ridx_p, rgate_p, cnt_p = _merge(
            cat_a_p.reshape(tp, FOX_W), cat_m_p.reshape(tp, ML_W), yp.reshape(tp, d), w_a, w_m, g_ffn,
            wr_hi, wr_lo, br, min(MERGE_ROWS, tp))
        y1_s, xn_s, ridx_s, rgate_s, cnt_s = _merge(
            cat_a_s.reshape(ts, FOX_W), cat_m_s.reshape(ts, ML_W), ys.reshape(ts, d), w_a, w_m, g_ffn,
            wr_hi, wr_lo, br, min(MERGE_ROWS, ts))

        cnt_p = cnt_p[0, :n_exp].astype(jnp.int32)
        cnt_s = cnt_s[0, :n_exp].astype(jnp.int32)
        seg_rows = -(-(cnt_p + cnt_s) // FFN_ROWS) * FFN_ROWS
        seg_end = jnp.cumsum(seg_rows)
        seg_start = seg_end - seg_rows
        n_rows = -(-((tp + ts) * TOP_K + n_exp * (FFN_ROWS - 1)) // FFN_ROWS) * FFN_ROWS
        n_tiles = n_rows // FFN_ROWS
        tile_row0 = jnp.arange(n_tiles, dtype=jnp.int32) * FFN_ROWS
        tile_expert = jnp.minimum(jnp.sum((seg_end[None, :] <= tile_row0[:, None]).astype(jnp.int32), axis=1),
                                  n_exp - 1)
        n_used = (seg_end[-1:] // FFN_ROWS).astype(jnp.int32)
        e_p, rank_p = ridx_p[:, :TOP_K], ridx_p[:, TOP_K:2 * TOP_K]
        e_s, rank_s = ridx_s[:, :TOP_K], ridx_s[:, TOP_K:2 * TOP_K]
        pos_p = seg_start[e_p] + rank_p
        pos_s = seg_start[e_s] + cnt_p[e_s] + rank_s
        tm_p, tm_s = min(MOE_ROWS, tp), min(MOE_ROWS, ts)
        pos3_p = pos_p.reshape(tp // tm_p, 1, TOP_K * tm_p)
        pos3_s = pos_s.reshape(ts // tm_s, 1, TOP_K * tm_s)

        xs = jnp.zeros((n_rows,) + _row_tile(d), F32)
        xs = _scatter(pos3_p, xn_p, xs, tm_p)
        xs = _scatter(pos3_s, xn_s, xs, tm_s)
        ysort = _ffn(tile_expert, n_used, xs, w_gate_up[l], b_gate_up[l].reshape(n_exp, 1, 2 * d_ff),
                     w_down[l], b_down[l].reshape(n_exp, 1, d), FFN_ROWS)

        g_fin = norm_final_g[None, :]
        yp = _combine(pos3_p, y1_p, rgate_p, g_fin, ysort, tm_p).reshape(bp, sp, d)
        ys = _combine(pos3_s, y1_s, rgate_s, g_fin, ysort, tm_s).reshape(bs, ss, d)

    p_out = tuple(jnp.stack(a) for a in zip(*p_st))
    s_out = tuple(jnp.stack(a) for a in zip(*s_st))
    return (yp, ys) + p_out + s_out
```

```python
import functools
import math

import numpy as np

import jax
import jax.numpy as jnp
from jax import lax
from jax.experimental import pallas as pl
from jax.experimental.pallas import tpu as pltpu

F32 = jnp.float32
BF16 = jnp.bfloat16

FOX_HEADS = 8
FOX_HD = 64
FOX_W = FOX_HEADS * FOX_HD
ML_HEADS = 4
ML_DK = 128
ML_DV = 128
ML_W = ML_HEADS * ML_DV
QK_W = 2 * ML_HEADS * ML_DK
CONV_W = 4
CHUNK = 64
TOP_K = 4
SWIGLU_LIMIT = 7.0
SWIGLU_ALPHA = 1.702
EPS = 1e-6
NEG = -1e30
LOG2E = math.log2(math.e)

O_FQ = 0
O_FK = O_FQ + FOX_W
O_FV = O_FK + FOX_W
O_FF = O_FV + FOX_W
O_MQ = O_FF + FOX_HEADS
O_MK = O_MQ + ML_HEADS * ML_DK
O_MV = O_MK + ML_HEADS * ML_DK
O_MO = O_MV + ML_W
O_MI = O_MO + ML_W
O_MF = O_MI + ML_HEADS
P_IN = O_MF + ML_HEADS

B_FQ, B_FK, B_FV, B_QK, B_MV, B_MO, B_END = 0, 512, 1024, 1536, 2560, 3072, 3584
S_FF, S_MI, S_MF, S_END = 0, 8, 12, 16
BIAS_TERMS = 3

V7X_LANES = 128
V7X_SUBLANES = 8
V7X_VMEM_LIMIT_BYTES = 56 * 1024 * 1024

PROJ_ROWS = 512
FOX_BLOCK = 512
FOX_QBLOCK = 1024
FOX_SUB = 512
FOX_UNROLLS = (4, 2, 1)
MERGE_ROWS = 512
MOE_ROWS = 512
FFN_ROWS = 512
DMA_UNROLL = 8


def _params(*semantics):
    return pltpu.CompilerParams(dimension_semantics=semantics, vmem_limit_bytes=V7X_VMEM_LIMIT_BYTES)


def _dot(a, b):
    return jnp.dot(a, b, preferred_element_type=F32)


def _dot_nt(a, b):
    return lax.dot_general(a, b, (((1,), (1,)), ((), ())), preferred_element_type=F32)


def _mxu_transpose(eye, x):
    return _dot_nt(eye, x)


def _sigmoid(x):
    return 1.0 / (1.0 + jnp.exp(-x))


def _log_sigmoid(x):
    return jnp.minimum(x, 0.0) - jnp.log1p(jnp.exp(-jnp.abs(x)))


def _split3(x):
    t1 = x.astype(BF16)
    r1 = x - t1.astype(F32)
    t2 = r1.astype(BF16)
    t3 = (r1 - t2.astype(F32)).astype(BF16)
    return t1, t2, t3


def _lower_tri(n):
    r = lax.broadcasted_iota(jnp.int32, (n, n), 0)
    c = lax.broadcasted_iota(jnp.int32, (n, n), 1)
    return c <= r


def _proj_kernel(x_ref, g_ref, wbig_ref, wsm_ref, bsm_ref, cw_ref, cb_ref, hist_ref,
                 fq_ref, fk32_ref, fv32_ref, fk16_ref, fv16_ref, sm_ref, flogf_ref, mq_ref, mk_ref, mv_ref,
                 mo_ref, cnew_ref, halo_ref):
    tm = x_ref.shape[1]

    @pl.when(pl.program_id(1) == 0)
    def _():
        halo_ref[...] = hist_ref[0]

    x = x_ref[0]
    xn = (x * lax.rsqrt(jnp.mean(x * x, axis=-1, keepdims=True) + EPS)) * g_ref[...]
    xb = xn.astype(BF16)
    z = _dot(xb, wbig_ref[...])

    fq_ref[0] = (z[:, B_FQ:B_FK] * (FOX_HD ** -0.5 * LOG2E)).astype(BF16)
    fk = z[:, B_FK:B_FV]
    fv = z[:, B_FV:B_QK]
    fk32_ref[0] = fk
    fv32_ref[0] = fv
    fk16_ref[0] = fk.astype(BF16)
    fv16_ref[0] = fv.astype(BF16)
    mv_ref[0] = z[:, B_MV:B_MO].astype(BF16)
    mo_ref[0] = _sigmoid(z[:, B_MO:B_END]).astype(BF16)

    zs = _dot(xb, wsm_ref[...]) + bsm_ref[...]
    lane = lax.broadcasted_iota(jnp.int32, zs.shape, 1)
    is_forget = (lane < S_MI) | ((lane >= S_MF) & (lane < S_END))
    sm = jnp.where(is_forget, _log_sigmoid(zs), zs)
    sm_ref[0] = sm
    flogf_ref[0] = sm[:, S_FF:S_MI]

    u = z[:, B_QK:B_MV]
    up = jnp.concatenate([halo_ref[...], u], axis=0)
    n = tm + V7X_SUBLANES
    first = V7X_SUBLANES - (CONV_W - 1)
    y = cb_ref[...] + cw_ref[0:1, :] * pltpu.roll(up, n - first, axis=0)[:tm]
    for j in range(1, CONV_W - 1):
        y = y + cw_ref[j:j + 1, :] * pltpu.roll(up, n - (first + j), axis=0)[:tm]
    y = y + cw_ref[CONV_W - 1:CONV_W, :] * u
    qk = y * _sigmoid(y)
    mq_ref[0] = qk[:, :QK_W // 2].astype(BF16)
    mk_ref[0] = (qk[:, QK_W // 2:] * (ML_DK ** -0.5)).astype(BF16)

    halo_ref[...] = u[tm - V7X_SUBLANES:, :]
    cnew_ref[0] = halo_ref[first:, :]


def _project(x, hist8, g, wbig, wsm, bsm, cw, cb, tm):
    bsz, seq, d = x.shape
    grid = (bsz, seq // tm)
    row = lambda c: pl.BlockSpec((1, tm, c), lambda b, s: (b, s, 0))
    const = lambda shape: pl.BlockSpec(shape, lambda b, s: (0,) * len(shape))
    outs = [
        (FOX_W, BF16), (FOX_W, F32), (FOX_W, F32), (FOX_W, BF16), (FOX_W, BF16), (V7X_LANES, F32),
        (FOX_HEADS, F32), (ML_W, BF16), (ML_W, BF16), (ML_W, BF16), (ML_W, BF16),
    ]
    out_shape = [jax.ShapeDtypeStruct((bsz, seq, c), dt) for c, dt in outs]
    out_specs = [row(c) for c, _ in outs]
    out_shape.append(jax.ShapeDtypeStruct((bsz, CONV_W - 1, QK_W), F32))
    out_specs.append(pl.BlockSpec((1, CONV_W - 1, QK_W), lambda b, s: (b, 0, 0)))
    return pl.pallas_call(
        _proj_kernel,
        grid=grid,
        in_specs=[row(d), const((1, d)), const(wbig.shape), const(wsm.shape), const((1, V7X_LANES)),
                  const((CONV_W, QK_W)), const((1, QK_W)),
                  pl.BlockSpec((1, V7X_SUBLANES, QK_W), lambda b, s: (b, 0, 0))],
        out_specs=out_specs,
        out_shape=out_shape,
        scratch_shapes=[pltpu.VMEM((V7X_SUBLANES, QK_W), F32)],
        compiler_params=_params("arbitrary", "arbitrary"),
        name="proj",
    )(x, g, wbig, wsm, bsm, cw, cb, hist8)


def _bias_lane(head, term):
    return (head // 2) * V7X_LANES + (FOX_HD if head % 2 == 0 else 0) + term


def _bias_placement():
    place = np.zeros((BIAS_TERMS, V7X_LANES, FOX_W), np.float32)
    for h in range(FOX_HEADS):
        for t in range(BIAS_TERMS):
            place[t, S_FF + h, _bias_lane(h, t)] = 1.0
    return jnp.asarray(place, BF16)


def _fox_prep_kernel(lf_ref, k_ref, v_ref, place_ref, ka0_ref, ka1_ref, va0_ref, va1_ref, carry_ref):
    tm = lf_ref.shape[1]

    @pl.when(pl.program_id(1) == 0)
    def _():
        carry_ref[...] = jnp.zeros_like(carry_ref)

    tri = jnp.where(_lower_tri(tm), 1.0, 0.0).astype(BF16)
    t1, t2, t3 = _split3(lf_ref[0])
    c = carry_ref[...] + (_dot(tri, t1) + (_dot(tri, t2) + _dot(tri, t3)))
    carry_ref[...] = c[tm - 1:tm, :]
    b1, b2, b3 = _split3(c * (-LOG2E))
    kb = _dot(b1, place_ref[0]) + (_dot(b2, place_ref[1]) + _dot(b3, place_ref[2]))

    k = k_ref[0].astype(F32)
    v = v_ref[0].astype(F32)
    first_head = lax.broadcasted_iota(jnp.int32, k.shape, 1) % V7X_LANES < FOX_HD
    ka0_ref[0] = jnp.where(first_head, k, kb).astype(BF16)
    ka1_ref[0] = jnp.where(first_head, kb, k).astype(BF16)
    va0_ref[0] = jnp.where(first_head, v, 1.0).astype(BF16)
    va1_ref[0] = jnp.where(first_head, 1.0, v).astype(BF16)


def _fox_prep(lf, k, v, tm):
    bsz, lk, _ = lf.shape
    place = _bias_placement()
    wide = pl.BlockSpec((1, tm, FOX_W), lambda b, s: (b, s, 0))
    sds = jax.ShapeDtypeStruct((bsz, lk, FOX_W), BF16)
    return pl.pallas_call(
        _fox_prep_kernel,
        grid=(bsz, lk // tm),
        in_specs=[pl.BlockSpec((1, tm, V7X_LANES), lambda b, s: (b, s, 0)), wide, wide,
                  pl.BlockSpec(place.shape, lambda b, s: (0, 0, 0))],
        out_specs=[wide, wide, wide, wide],
        out_shape=[sds, sds, sds, sds],
        scratch_shapes=[pltpu.VMEM((1, V7X_LANES), F32)],
        compiler_params=_params("arbitrary", "arbitrary"),
        name="fox_prep",
    )(lf, k, v, place)


def _fox_kernel(q_ref, ka0_ref, ka1_ref, va0_ref, va1_ref, g_ref, o_ref, *, tq, tk, sub, q_offset):
    ts = min(tq, sub)
    n_sub = tq // ts
    diag_blocks = max(ts // tk, 1)
    k_refs = (ka0_ref, ka1_ref)
    v_refs = (va0_ref, va1_ref)
    lane = lax.broadcasted_iota(jnp.int32, (ts, V7X_LANES), 1)
    first_head = lane < FOX_HD
    q_lo = q_offset + pl.program_id(2) * tq
    n_common = (q_lo + 1) // tk
    single_block = ka0_ref.shape[1] == tk
    even_common = q_offset % (2 * tk) == 0 and tq % (2 * tk) == 0

    def chain_q(r, hh):
        q = q_ref[0, r * ts:(r + 1) * ts, :].astype(F32)
        if hh == 0:
            return jnp.where(first_head, q, jnp.where(lane < FOX_HD + BIAS_TERMS, 1.0, 0.0)).astype(BF16)
        return jnp.where(first_head, jnp.where(lane < BIAS_TERMS, 1.0, 0.0), q).astype(BF16)

    chains = [(r, hh) for r in range(n_sub) for hh in range(2)]
    qs = [chain_q(r, hh) for r, hh in chains]

    def scores(c, j):
        return _dot_nt(qs[c], k_refs[chains[c][1]][0, pl.ds(pl.multiple_of(j * tk, tk), tk), :])

    def update(state, c, j, s, masked):
        r, hh = chains[c]
        m, acc = state
        start = pl.multiple_of(j * tk, tk)
        if masked:
            qpos = q_lo + r * ts + lax.broadcasted_iota(jnp.int32, s.shape, 0)
            kpos = start + lax.broadcasted_iota(jnp.int32, s.shape, 1)
            s = jnp.where(kpos <= qpos, s, NEG)
        m_new = jnp.maximum(m, jnp.max(s, axis=-1, keepdims=True))
        p = jnp.exp2(s - m_new).astype(BF16)
        acc = jnp.exp2(m - m_new) * acc + _dot(p, v_refs[hh][0, pl.ds(start, tk), :])
        return m_new, acc

    def step(state, c, j, masked):
        return update(state, c, j, scores(c, j), masked)

    def common(j, states):
        ss = [scores(c, j) for c in range(len(chains))]
        return tuple(update(st, c, j, ss[c], False) for c, st in enumerate(states))

    def common_run(t, states, nblk):
        states = list(states)
        cur = [scores(c, nblk * t) for c in range(len(chains))]
        for e in range(nblk):
            nxt = []
            for c in range(len(chains)):
                states[c] = update(states[c], c, nblk * t + e, cur[c], False)
                if e + 1 < nblk:
                    nxt.append(scores(c, nblk * t + e + 1))
            cur = nxt
        return tuple(states)

    init = (jnp.full((ts, 1), NEG, F32), jnp.zeros((ts, V7X_LANES), F32))
    states = (init,) * len(chains)
    if not single_block:
        done = 0
        for nblk in FOX_UNROLLS:
            if nblk == 1 and even_common:
                continue
            trips = n_common // nblk
            states = lax.fori_loop(done // nblk, trips, functools.partial(common_run, nblk=nblk), states)
            done = trips * nblk
    states = list(states)
    for c, (r, _) in enumerate(chains):
        for e in range(r * diag_blocks):
            states[c] = step(states[c], c, n_common + e, False)
        for e in range(diag_blocks):
            states[c] = step(states[c], c, n_common + r * diag_blocks + e, True)

    for r in range(n_sub):
        out = jnp.zeros((ts, V7X_LANES), F32)
        for hh in range(2):
            acc = states[chains.index((r, hh))][1]
            own = first_head if hh == 0 else jnp.logical_not(first_head)
            denom_lane = FOX_HD if hh == 0 else 0
            o = jnp.where(own, acc / acc[:, denom_lane:denom_lane + 1], 0.0)
            out = out + o * lax.rsqrt(jnp.sum(o * o, axis=-1, keepdims=True) * (1.0 / FOX_HD) + EPS)
        o_ref[0, r * ts:(r + 1) * ts, :] = (out * g_ref[...]).astype(BF16)


def _fox(q, ka0, ka1, va0, va1, g_fox, q_offset, tq, tk):
    bsz, lq, _ = q.shape
    lk = ka0.shape[1]
    assert (q_offset % tk == 0 and tq % tk == 0) or (lq == tq <= tk and lk == tk), (q_offset, tq, tk, lq, lk)
    pairs = FOX_HEADS // 2
    grid = (bsz, pairs, lq // tq)
    kv_spec = pl.BlockSpec((1, lk, V7X_LANES), lambda b, p, i: (b, 0, p))
    q_spec = pl.BlockSpec((1, tq, V7X_LANES), lambda b, p, i: (b, i, p))
    return pl.pallas_call(
        functools.partial(_fox_kernel, tq=tq, tk=tk, sub=FOX_SUB, q_offset=q_offset),
        grid=grid,
        in_specs=[q_spec, kv_spec, kv_spec, kv_spec, kv_spec,
                  pl.BlockSpec((1, V7X_LANES), lambda b, p, i: (0, p))],
        out_specs=q_spec,
        out_shape=jax.ShapeDtypeStruct((bsz, lq, FOX_W), BF16),
        compiler_params=_params("arbitrary", "arbitrary", "arbitrary"),
        name="fox_attention",
    )(q, ka0, ka1, va0, va1, g_fox)


def _mlstm_kernel(q_ref, k_ref, v_ref, mo_ref, sm_ref, c0_ref, n0_ref, m0_ref, g_ref,
                  o_ref, c_ref, n_ref, m_ref):
    bsz, L, _ = q_ref.shape

    @pl.when(pl.program_id(0) == 0)
    def _():
        c_ref[...] = c0_ref[...]
        n_ref[...] = n0_ref[...]
        m_ref[...] = m0_ref[...]

    causal = _lower_tri(L)
    tri = jnp.where(causal, 1.0, 0.0).astype(BF16)
    eye = jnp.where(lax.broadcasted_iota(jnp.int32, (V7X_LANES, V7X_LANES), 0)
                    == lax.broadcasted_iota(jnp.int32, (V7X_LANES, V7X_LANES), 1), 1.0, 0.0).astype(BF16)
    rows = lax.broadcasted_iota(jnp.int32, (L, V7X_LANES), 0)
    lanes = lax.broadcasted_iota(jnp.int32, (L, V7X_LANES), 1)
    gate_lanes = (lanes >= S_MI) & (lanes < S_END)

    heads = [(b, h) for b in range(bsz) for h in range(ML_HEADS)]
    cols = lambda h: slice(h * ML_DK, (h + 1) * ML_DK)

    qk, qc, n_prev = {}, {}, {}
    for b, h in heads:
        qh = q_ref[b, :, cols(h)]
        qk[b, h] = _dot_nt(qh, k_ref[b, :, cols(h)])
        qc[b, h] = _dot_nt(qh, c_ref[b, h].astype(BF16))

    gates = {}
    for b in range(bsz):
        sm = jnp.where(gate_lanes, sm_ref[b], 0.0)
        t1, t2, t3 = _split3(sm)
        bcum = _dot(tri, t1) + (_dot(tri, t2) + _dot(tri, t3))
        bcum = pltpu.roll(bcum, V7X_LANES - (S_MF - S_MI), axis=1)
        g = sm - bcum
        gmax = g
        sh = 1
        while sh < L:
            gmax = jnp.maximum(gmax, jnp.where(rows >= sh, pltpu.roll(gmax, sh, axis=0), -jnp.inf))
            sh *= 2
        m = m_ref[pl.ds(b, 1), :]
        u = jnp.maximum(m, gmax)
        a = jnp.exp(m - u)
        mt = bcum + u
        m_ref[pl.ds(b, 1), :] = mt[L - 1:L, :]
        g1, g2, g3 = _split3(g)
        g_rows = _mxu_transpose(eye, g1) + (_mxu_transpose(eye, g2) + _mxu_transpose(eye, g3))
        gates[b] = dict(u=u, a=a, em=jnp.exp(-mt), a_last=a[L - 1:L, :], wcol=jnp.exp(g - u[L - 1:L, :]),
                        g_rows=g_rows)

    for b, h in heads:
        gl = S_MI + h
        kh = k_ref[b, :, cols(h)]
        w_h = gates[b]["wcol"][:, gl:gl + 1]
        a_l = gates[b]["a_last"][:, gl:gl + 1]
        vw = (v_ref[b, :, cols(h)].astype(F32) * w_h).astype(BF16)
        n_old = n_ref[b, h:h + 1, :]
        n_prev[b, h] = n_old
        c_ref[b, h] = a_l * c_ref[b, h] + _dot(_mxu_transpose(eye, vw).astype(BF16), kh)
        n_ref[b, h:h + 1, :] = a_l * n_old + jnp.sum(kh.astype(F32) * w_h, axis=0, keepdims=True)

    for b, h in heads:
        gl = S_MI + h
        gb = gates[b]
        decay = jnp.exp(jnp.where(causal, gb["g_rows"][gl:gl + 1, :] - gb["u"][:, gl:gl + 1], NEG))
        s = qk[b, h] * decay
        a_h = gb["a"][:, gl:gl + 1]
        num = a_h * qc[b, h] + _dot(s.astype(BF16), v_ref[b, :, cols(h)])
        den = (a_h * jnp.sum(q_ref[b, :, cols(h)].astype(F32) * n_prev[b, h], axis=-1, keepdims=True)
               + jnp.sum(s, axis=-1, keepdims=True))
        hv = num / jnp.maximum(jnp.abs(den), gb["em"][:, gl:gl + 1])
        hn = hv * lax.rsqrt(jnp.mean(hv * hv, axis=-1, keepdims=True) + EPS)
        o_ref[b, :, cols(h)] = (hn * g_ref[:, cols(h)] * mo_ref[b, :, cols(h)].astype(F32)).astype(BF16)


def _mlstm(mq, mk, mv, mo, small, c0, n0, m0, g_ml, L):
    bsz, seq, _ = mq.shape
    chunk = pl.BlockSpec((bsz, L, ML_W), lambda c: (0, c, 0))
    full = lambda shape: pl.BlockSpec(shape, lambda c: (0,) * len(shape))
    return pl.pallas_call(
        _mlstm_kernel,
        grid=(seq // L,),
        in_specs=[chunk, chunk, chunk, chunk,
                  pl.BlockSpec((bsz, L, V7X_LANES), lambda c: (0, c, 0)),
                  full(c0.shape), full(n0.shape), full(m0.shape), full((1, ML_W))],
        out_specs=[chunk, full(c0.shape), full(n0.shape), full(m0.shape)],
        out_shape=[jax.ShapeDtypeStruct((bsz, seq, ML_W), BF16),
                   jax.ShapeDtypeStruct(c0.shape, F32),
                   jax.ShapeDtypeStruct(n0.shape, F32),
                   jax.ShapeDtypeStruct(m0.shape, F32)],
        compiler_params=_params("arbitrary"),
        name="mlstm",
    )(mq, mk, mv, mo, small, c0, n0, m0, g_ml)


def _merge_kernel(ca_ref, cm_ref, x_ref, wa_ref, wm_ref, g_ref, wrh_ref, wrl_ref, br_ref,
                  y1_ref, xn_ref, ridx_ref, rgate_ref, cnt_ref):
    tm = x_ref.shape[0]

    @pl.when(pl.program_id(0) == 0)
    def _():
        cnt_ref[...] = jnp.zeros_like(cnt_ref)

    y1 = x_ref[...] + (_dot(ca_ref[...], wa_ref[...]) + _dot(cm_ref[...], wm_ref[...]))
    y1_ref[...] = y1
    xn = (y1 * lax.rsqrt(jnp.mean(y1 * y1, axis=-1, keepdims=True) + EPS)) * g_ref[...]
    xn_ref[...] = xn.reshape(xn_ref.shape)

    xh = xn.astype(BF16)
    xl = (xn - xh.astype(F32)).astype(BF16)
    logits = (_dot(xh, wrh_ref[...]) + (_dot(xl, wrh_ref[...]) + _dot(xh, wrl_ref[...]))) + br_ref[...]

    lane = lax.broadcasted_iota(jnp.int32, logits.shape, 1)
    vals, sel = [], []
    work = logits
    for _ in range(TOP_K):
        mx = jnp.max(work, axis=-1, keepdims=True)
        idx = jnp.min(jnp.where(work == mx, lane, V7X_LANES), axis=-1, keepdims=True)
        vals.append(mx)
        sel.append(idx)
        work = jnp.where(lane == idx, -jnp.inf, work)
    ex = [jnp.exp(v - vals[0]) for v in vals]
    tot = ex[0] + ex[1] + ex[2] + ex[3]

    onehot = [(lane == idx) for idx in sel]
    picked = jnp.where(onehot[0] | onehot[1] | onehot[2] | onehot[3], 1.0, 0.0)
    earlier = (lax.broadcasted_iota(jnp.int32, (tm, tm), 1) < lax.broadcasted_iota(jnp.int32, (tm, tm), 0))
    before = _dot(jnp.where(earlier, 1.0, 0.0).astype(BF16), picked.astype(BF16)) + cnt_ref[...]
    cnt_ref[...] = cnt_ref[...] + jnp.sum(picked, axis=0, keepdims=True)

    ridx = jnp.zeros(logits.shape, jnp.int32)
    rgate = jnp.zeros(logits.shape, F32)
    for k in range(TOP_K):
        rank = jnp.sum(jnp.where(onehot[k], before, 0.0), axis=-1, keepdims=True).astype(jnp.int32)
        ridx = jnp.where(lane == k, sel[k], ridx)
        ridx = jnp.where(lane == TOP_K + k, rank, ridx)
        rgate = jnp.where(lane == k, ex[k] / tot, rgate)
    ridx_ref[...] = ridx
    rgate_ref[...] = rgate


def _merge(cat_a, cat_m, x, w_a, w_m, g_ffn, wr_hi, wr_lo, br, tm):
    t, d = x.shape
    row = lambda c: pl.BlockSpec((tm, c), lambda i: (i, 0))
    const = lambda shape: pl.BlockSpec(shape, lambda i: (0,) * len(shape))
    return pl.pallas_call(
        _merge_kernel,
        grid=(t // tm,),
        in_specs=[row(FOX_W), row(ML_W), row(d), const(w_a.shape), const(w_m.shape), const((1, d)),
                  const(wr_hi.shape), const(wr_lo.shape), const((1, V7X_LANES))],
        out_specs=[row(d), pl.BlockSpec((tm,) + _row_tile(d), lambda i: (i, 0, 0)), row(V7X_LANES),
                   row(V7X_LANES), const((1, V7X_LANES))],
        out_shape=[jax.ShapeDtypeStruct((t, d), F32), jax.ShapeDtypeStruct((t,) + _row_tile(d), F32),
                   jax.ShapeDtypeStruct((t, V7X_LANES), jnp.int32),
                   jax.ShapeDtypeStruct((t, V7X_LANES), F32),
                   jax.ShapeDtypeStruct((1, V7X_LANES), F32)],
        compiler_params=_params("arbitrary"),
        name="merge_router",
    )(cat_a, cat_m, x, w_a, w_m, g_ffn, wr_hi, wr_lo, br)


def _row_tile(d):
    return (d // V7X_LANES, V7X_LANES)


def _row_copy(src_ref, src_row, dst_ref, dst_row, sem):
    return pltpu.make_async_copy(src_ref.at[pl.ds(src_row, 1)], dst_ref.at[pl.ds(dst_row, 1)], sem)


def _scatter_rows(pos_ref, x_ref, xs_ref, sem):
    tm = x_ref.shape[0]

    def issue(g, _):
        for u in range(DMA_UNROLL):
            r = g * DMA_UNROLL + u
            for k in range(TOP_K):
                _row_copy(x_ref, r, xs_ref, pos_ref[0, 0, TOP_K * r + k], sem).start()
        return 0

    def drain(g, _):
        for _ in range(DMA_UNROLL * TOP_K):
            _row_copy(x_ref, 0, xs_ref, 0, sem).wait()
        return 0

    lax.fori_loop(0, tm // DMA_UNROLL, issue, 0)
    lax.fori_loop(0, tm // DMA_UNROLL, drain, 0)


def _dispatch_kernel(fs_ref, fl_ref, pos_p_ref, x_p_ref, pos_s_ref, x_s_ref, xs_ref, zero_ref, sem, *, ntp, nts):
    i = pl.program_id(0)

    @pl.when(i < ntp)
    def _():
        _scatter_rows(pos_p_ref, x_p_ref, xs_ref, sem)

    @pl.when((i >= ntp) & (i < ntp + nts))
    def _():
        _scatter_rows(pos_s_ref, x_s_ref, xs_ref, sem)

    @pl.when(i == ntp + nts)
    def _():
        zero_ref[...] = jnp.zeros_like(zero_ref)
        full = zero_ref.shape[0]
        bits = [1 << p for p in range(full.bit_length() - 2, -1, -1)]

        def zeros_to(start, size):
            return pltpu.make_async_copy(zero_ref.at[pl.ds(0, size)], xs_ref.at[pl.ds(start, size)], sem)

        def per_run(e, _, wait):
            start, n = fs_ref[e], fl_ref[e]
            whole = n // full

            def whole_copy(q, _):
                c = zeros_to(start + q * full, full)
                c.wait() if wait else c.start()
                return 0

            lax.fori_loop(0, whole, whole_copy, 0)
            for p in bits:
                @pl.when((n & p) != 0)
                def _():
                    c = zeros_to(start + (n & ~(2 * p - 1)), p)
                    c.wait() if wait else c.start()
            return 0

        lax.fori_loop(0, fs_ref.shape[0], functools.partial(per_run, wait=False), 0)
        lax.fori_loop(0, fs_ref.shape[0], functools.partial(per_run, wait=True), 0)


def _dispatch(fill_start, fill_len, pos3_p, x_p, pos3_s, x_s, n_rows, tm_p, tm_s):
    tp, sub, lanes = x_p.shape
    ntp, nts = tp // tm_p, x_s.shape[0] // tm_s
    clamp_p = lambda i, fs, fl: (jnp.minimum(i, ntp - 1), 0, 0)
    clamp_s = lambda i, fs, fl: (jnp.clip(i - ntp, 0, nts - 1), 0, 0)
    grid_spec = pltpu.PrefetchScalarGridSpec(
        num_scalar_prefetch=2,
        grid=(ntp + nts + 1,),
        in_specs=[pl.BlockSpec((1, 1, TOP_K * tm_p), clamp_p, memory_space=pltpu.SMEM),
                  pl.BlockSpec((tm_p, sub, lanes), clamp_p),
                  pl.BlockSpec((1, 1, TOP_K * tm_s), clamp_s, memory_space=pltpu.SMEM),
                  pl.BlockSpec((tm_s, sub, lanes), clamp_s)],
        out_specs=pl.BlockSpec(memory_space=pl.ANY),
        scratch_shapes=[pltpu.VMEM((FFN_ROWS, sub, lanes), F32), pltpu.SemaphoreType.DMA(())],
    )
    return pl.pallas_call(
        functools.partial(_dispatch_kernel, ntp=ntp, nts=nts),
        grid_spec=grid_spec,
        out_shape=jax.ShapeDtypeStruct((n_rows, sub, lanes), F32),
        compiler_params=_params("arbitrary"),
        name="moe_dispatch",
    )(fill_start, fill_len, pos3_p, x_p, pos3_s, x_s)


def _ffn_kernel(te_ref, nu_ref, x_ref, wgu_ref, bgu_ref, wd_ref, bd_ref, y_ref, wgu16_ref, wd16_ref):
    i = pl.program_id(0)
    d_ff = wd_ref.shape[1]

    @pl.when((i == 0) | (te_ref[i] != te_ref[jnp.maximum(i - 1, 0)]))
    def _():
        wgu16_ref[...] = wgu_ref[0].astype(BF16)
        wd16_ref[...] = wd_ref[0].astype(BF16)

    @pl.when(i < nu_ref[0])
    def _():
        x = x_ref[...].reshape(x_ref.shape[0], -1)
        gu = _dot(x.astype(BF16), wgu16_ref[...]) + bgu_ref[0]
        gate = jnp.minimum(gu[:, :d_ff], SWIGLU_LIMIT)
        up = jnp.clip(gu[:, d_ff:], -SWIGLU_LIMIT, SWIGLU_LIMIT)
        h = (up + 1.0) * (gate * _sigmoid(SWIGLU_ALPHA * gate))
        y = _dot(h.astype(BF16), wd16_ref[...]) + bd_ref[0]
        y_ref[...] = y.reshape(y_ref.shape)

    @pl.when(i >= nu_ref[0])
    def _():
        y_ref[...] = jnp.zeros_like(y_ref)


def _ffn(tile_expert, n_used, xs, w_gu, b_gu, w_d, b_d, tm):
    r = xs.shape[0]
    d_ff, d = w_d.shape[1:]
    tiles = pl.BlockSpec((tm,) + _row_tile(d), lambda i, te, nu: (i, 0, 0))
    grid_spec = pltpu.PrefetchScalarGridSpec(
        num_scalar_prefetch=2,
        grid=(r // tm,),
        in_specs=[tiles,
                  pl.BlockSpec((1, d, 2 * d_ff), lambda i, te, nu: (te[i], 0, 0)),
                  pl.BlockSpec((1, 1, 2 * d_ff), lambda i, te, nu: (te[i], 0, 0)),
                  pl.BlockSpec((1, d_ff, d), lambda i, te, nu: (te[i], 0, 0)),
                  pl.BlockSpec((1, 1, d), lambda i, te, nu: (te[i], 0, 0))],
        out_specs=tiles,
        scratch_shapes=[pltpu.VMEM((d, 2 * d_ff), BF16), pltpu.VMEM((d_ff, d), BF16)],
    )
    return pl.pallas_call(
        _ffn_kernel,
        grid_spec=grid_spec,
        out_shape=jax.ShapeDtypeStruct(xs.shape, F32),
        compiler_params=_params("arbitrary"),
        name="moe_ffn",
    )(tile_expert, n_used, xs, w_gu, b_gu, w_d, b_d)


def _combine_kernel(pos_ref, pos_next_ref, y1_ref, gate_ref, gfin_ref, ys_ref, o_ref, buf_ref, sem):
    tm = y1_ref.shape[0]
    i = pl.program_id(0)
    slot = i % 2

    def gather(p_ref, s):
        def issue(g, _):
            for u in range(DMA_UNROLL):
                r = g * DMA_UNROLL + u
                for k in range(TOP_K):
                    _row_copy(ys_ref, p_ref[0, 0, TOP_K * r + k], buf_ref.at[s, k], r, sem.at[s]).start()
            return 0

        lax.fori_loop(0, tm // DMA_UNROLL, issue, 0)

    @pl.when(i == 0)
    def _():
        gather(pos_ref, 0)

    @pl.when(i + 1 < pl.num_programs(0))
    def _():
        gather(pos_next_ref, 1 - slot)

    def drain(g, _):
        for _ in range(DMA_UNROLL * TOP_K):
            _row_copy(ys_ref, 0, buf_ref.at[slot, 0], 0, sem.at[slot]).wait()
        return 0

    lax.fori_loop(0, tm // DMA_UNROLL, drain, 0)

    gate = gate_ref[...]
    moe = gate[:, 0:1] * buf_ref[slot, 0].reshape(tm, -1)
    for k in range(1, TOP_K):
        moe = moe + gate[:, k:k + 1] * buf_ref[slot, k].reshape(tm, -1)
    y = y1_ref[...] + moe
    o_ref[...] = (y * lax.rsqrt(jnp.mean(y * y, axis=-1, keepdims=True) + EPS)) * gfin_ref[...]


def _combine(pos3, y1, rgate, g_final, ys, tm):
    t, d = y1.shape
    nt = t // tm
    pos_spec = lambda f: pl.BlockSpec((1, 1, TOP_K * tm), f, memory_space=pltpu.SMEM)
    return pl.pallas_call(
        _combine_kernel,
        grid=(nt,),
        in_specs=[pos_spec(lambda i: (i, 0, 0)), pos_spec(lambda i: (jnp.minimum(i + 1, nt - 1), 0, 0)),
                  pl.BlockSpec((tm, d), lambda i: (i, 0)),
                  pl.BlockSpec((tm, V7X_LANES), lambda i: (i, 0)),
                  pl.BlockSpec((1, d), lambda i: (0, 0)),
                  pl.BlockSpec(memory_space=pl.ANY)],
        out_specs=pl.BlockSpec((tm, d), lambda i: (i, 0)),
        out_shape=jax.ShapeDtypeStruct((t, d), F32),
        scratch_shapes=[pltpu.VMEM((2, TOP_K, tm) + _row_tile(d), F32), pltpu.SemaphoreType.DMA((2,))],
        compiler_params=_params("arbitrary"),
        name="moe_combine",
    )(pos3, pos3, y1, rgate, g_final, ys)


def _mixer(x, hist, fox_cache, ml_state, w, tm, tq, tk, chunk):
    bsz, seq, _ = x.shape
    hist8 = jnp.pad(hist, ((0, 0), (V7X_SUBLANES - (CONV_W - 1), 0), (0, 0)))
    (fq, fk32, fv32, fk16, fv16, small, flogf, mq, mk, mv, mo, conv_new) = _project(
        x, hist8, w["g_mix"], w["wbig"], w["wsm"], w["bsm"], w["conv_w"], w["conv_b"], tm)

    if fox_cache is None:
        q_offset = 0
        lf_all, k_all, v_all = small, fk16, fv16
    else:
        ck_c, cv_c, clogf_c = fox_cache
        q_offset = ck_c.shape[1]
        clogf_c = jnp.pad(clogf_c, ((0, 0), (0, 0), (S_FF, V7X_LANES - S_FF - FOX_HEADS)))
        lf_all = jnp.concatenate([clogf_c, small], axis=1)
        k_all = jnp.concatenate([ck_c.reshape(bsz, q_offset, FOX_W).astype(BF16), fk16], axis=1)
        v_all = jnp.concatenate([cv_c.reshape(bsz, q_offset, FOX_W).astype(BF16), fv16], axis=1)
    lk = k_all.shape[1]
    lk_pad = -(-lk // tk) * tk
    pad_rows = ((0, 0), (0, lk_pad - lk), (0, 0))
    lf_all, k_all, v_all = jnp.pad(lf_all, pad_rows), jnp.pad(k_all, pad_rows), jnp.pad(v_all, pad_rows)
    ka0, ka1, va0, va1 = _fox_prep(lf_all, k_all, v_all, math.gcd(tk, FOX_BLOCK))
    cat_a = _fox(fq, ka0, ka1, va0, va1, w["g_fox"], q_offset, tq, tk)

    c0, n0, m0 = ml_state
    m0 = jnp.pad(m0, ((0, 0), (S_MI, V7X_LANES - S_MI - ML_HEADS)))
    cat_m, c_new, n_new, m_new = _mlstm(mq, mk, mv, mo, small, c0, n0, m0, w["g_ml"], chunk)
    m_new = m_new[:, S_MI:S_MI + ML_HEADS]

    states = (fk32.reshape(bsz, seq, FOX_HEADS, FOX_HD), fv32.reshape(bsz, seq, FOX_HEADS, FOX_HD), flogf,
              c_new, n_new, m_new, conv_new)
    return cat_a, cat_m, states


def kernel(x_prompt, x_sample, cache_fox_k, cache_fox_v, cache_fox_logf, state_mlstm_C, state_mlstm_n,
           state_mlstm_m, state_mlstm_conv, norm_mix_g, w_in, b_fox_f, conv_w, conv_b, b_ml_i, b_ml_f,
           g_fox, g_ml, w_out, norm_ffn_g, w_router, b_router, w_gate_up, b_gate_up, w_down, b_down,
           norm_final_g):
    depth = w_in.shape[0]
    assert depth == 1, "the final norm is fused into the last layer's combine; only depth 1 is wired up"
    bp, sp, d = x_prompt.shape
    bs, ss, _ = x_sample.shape
    n_exp = w_router.shape[-1]
    d_ff = w_down.shape[2]
    yp, ys = x_prompt, x_sample
    p_st, s_st = [], []

    for l in range(depth):
        wl = w_in[l]
        w = {
            "g_mix": norm_mix_g[l][None, :],
            "wbig": jnp.concatenate([wl[:, O_FQ:O_FF], wl[:, O_MQ:O_MI]], axis=1).astype(BF16),
            "wsm": jnp.pad(jnp.concatenate([wl[:, O_FF:O_MQ], wl[:, O_MI:P_IN]], axis=1),
                           ((0, 0), (0, V7X_LANES - S_END))).astype(BF16),
            "bsm": jnp.pad(jnp.concatenate([b_fox_f[l], b_ml_i[l], b_ml_f[l]]), (0, V7X_LANES - S_END))[None, :],
            "conv_w": conv_w[l],
            "conv_b": conv_b[l][None, :],
            "g_fox": g_fox[l][None, :],
            "g_ml": g_ml[l][None, :],
        }
        w_a = w_out[l][:FOX_W].astype(BF16)
        w_m = w_out[l][FOX_W:].astype(BF16)
        g_ffn = norm_ffn_g[l][None, :]
        wr = jnp.pad(w_router[l], ((0, 0), (0, V7X_LANES - n_exp)))
        wr_hi = wr.astype(BF16)
        wr_lo = (wr - wr_hi.astype(F32)).astype(BF16)
        br = jnp.pad(b_router[l], (0, V7X_LANES - n_exp), constant_values=NEG)[None, :]

        zeros_state = (jnp.zeros((bp, ML_HEADS, ML_DV, ML_DK), F32), jnp.zeros((bp, ML_HEADS, ML_DK), F32),
                       jnp.zeros((bp, ML_HEADS), F32))
        cat_a_p, cat_m_p, st_p = _mixer(yp, jnp.zeros((bp, CONV_W - 1, QK_W), F32), None, zeros_state, w,
                                        min(PROJ_ROWS, sp), min(FOX_QBLOCK, sp), min(FOX_BLOCK, sp), CHUNK)
        p_st.append(st_p)
        past = cache_fox_k.shape[2]
        tk_s = -(-(past + ss) // V7X_LANES) * V7X_LANES
        cat_a_s, cat_m_s, st_s = _mixer(
            ys, state_mlstm_conv[l], (cache_fox_k[l], cache_fox_v[l], cache_fox_logf[l]),
            (state_mlstm_C[l].astype(F32), state_mlstm_n[l].astype(F32), state_mlstm_m[l].astype(F32)),
            w, ss, ss, tk_s, ss)
        s_st.append(st_s)

        tp, ts = bp * sp, bs * ss
        y1_p, xn_p, ridx_p, rgate_p, cnt_p = _merge(
            cat_a_p.reshape(tp, FOX_W), cat_m_p.reshape(tp, ML_W), yp.reshape(tp, d), w_a, w_m, g_ffn,
            wr_hi, wr_lo, br, min(MERGE_ROWS, tp))
        y1_s, xn_s, ridx_s, rgate_s, cnt_s = _merge(
            cat_a_s.reshape(ts, FOX_W), cat_m_s.reshape(ts, ML_W), ys.reshape(ts, d), w_a, w_m, g_ffn,
            wr_hi, wr_lo, br, min(MERGE_ROWS, ts))

        cnt_p = cnt_p[0, :n_exp].astype(jnp.int32)
        cnt_s = cnt_s[0, :n_exp].astype(jnp.int32)
        seg_rows = -(-(cnt_p + cnt_s) // FFN_ROWS) * FFN_ROWS
        seg_end = jnp.cumsum(seg_rows)
        seg_start = seg_end - seg_rows
        n_rows = -(-((tp + ts) * TOP_K + n_exp * (FFN_ROWS - 1)) // FFN_ROWS) * FFN_ROWS
        n_tiles = n_rows // FFN_ROWS
        tile_row0 = jnp.arange(n_tiles, dtype=jnp.int32) * FFN_ROWS
        tile_expert = jnp.minimum(jnp.sum((seg_end[None, :] <= tile_row0[:, None]).astype(jnp.int32), axis=1),
                                  n_exp - 1)
        n_used = (seg_end[-1:] // FFN_ROWS).astype(jnp.int32)
        e_p, rank_p = ridx_p[:, :TOP_K], ridx_p[:, TOP_K:2 * TOP_K]
        e_s, rank_s = ridx_s[:, :TOP_K], ridx_s[:, TOP_K:2 * TOP_K]
        pos_p = seg_start[e_p] + rank_p
        pos_s = seg_start[e_s] + cnt_p[e_s] + rank_s
        tm_p, tm_s = min(MOE_ROWS, tp), min(MOE_ROWS, ts)
        pos3_p = pos_p.reshape(tp // tm_p, 1, TOP_K * tm_p)
        pos3_s = pos_s.reshape(ts // tm_s, 1, TOP_K * tm_s)

        cnt = cnt_p + cnt_s
        fill_start = jnp.concatenate([seg_start + cnt, seg_end[-1:]]).astype(jnp.int32)
        fill_len = jnp.concatenate([seg_rows - cnt, n_rows - seg_end[-1:]]).astype(jnp.int32)
        xs = _dispatch(fill_start, fill_len, pos3_p, xn_p, pos3_s, xn_s, n_rows, tm_p, tm_s)
        ysort = _ffn(tile_expert, n_used, xs, w_gate_up[l], b_gate_up[l].reshape(n_exp, 1, 2 * d_ff),
                     w_down[l], b_down[l].reshape(n_exp, 1, d), FFN_ROWS)

        g_fin = norm_final_g[None, :]
        yp = _combine(pos3_p, y1_p, rgate_p, g_fin, ysort, tm_p).reshape(bp, sp, d)
        ys = _combine(pos3_s, y1_s, rgate_s, g_fin, ysort, tm_s).reshape(bs, ss, d)

    p_out = tuple(jnp.stack(a) for a in zip(*p_st))
    s_out = tuple(jnp.stack(a) for a in zip(*s_st))
    return (yp, ys) + p_out + s_out
```

```python
import functools
import math

import numpy as np

import jax
import jax.numpy as jnp
from jax import lax
from jax.experimental import pallas as pl
from jax.experimental.pallas import tpu as pltpu

F32 = jnp.float32
BF16 = jnp.bfloat16

FOX_HEADS = 8
FOX_HD = 64
FOX_W = FOX_HEADS * FOX_HD
ML_HEADS = 4
ML_DK = 128
ML_DV = 128
ML_W = ML_HEADS * ML_DV
QK_W = 2 * ML_HEADS * ML_DK
CONV_W = 4
CHUNK = 64
TOP_K = 4
SWIGLU_LIMIT = 7.0
SWIGLU_ALPHA = 1.702
EPS = 1e-6
NEG = -1e30
LOG2E = math.log2(math.e)

O_FQ = 0
O_FK = O_FQ + FOX_W
O_FV = O_FK + FOX_W
O_FF = O_FV + FOX_W
O_MQ = O_FF + FOX_HEADS
O_MK = O_MQ + ML_HEADS * ML_DK
O_MV = O_MK + ML_HEADS * ML_DK
O_MO = O_MV + ML_W
O_MI = O_MO + ML_W
O_MF = O_MI + ML_HEADS
P_IN = O_MF + ML_HEADS

B_FQ, B_FK, B_FV, B_QK, B_MV, B_MO, B_END = 0, 512, 1024, 1536, 2560, 3072, 3584
S_FF, S_MI, S_MF, S_END = 0, 8, 12, 16
BIAS_TERMS = 3

V7X_LANES = 128
V7X_SUBLANES = 8
V7X_VMEM_LIMIT_BYTES = 56 * 1024 * 1024

PROJ_ROWS = 512
FOX_BLOCK = 512
FOX_QBLOCK = 1024
FOX_SUB = 512
FOX_UNROLLS = (4, 2, 1)
MERGE_ROWS = 512
MOE_ROWS = 512
FFN_ROWS = 512
DMA_UNROLL = 8


def _params(*semantics):
    return pltpu.CompilerParams(dimension_semantics=semantics, vmem_limit_bytes=V7X_VMEM_LIMIT_BYTES)


def _dot(a, b):
    return jnp.dot(a, b, preferred_element_type=F32)


def _dot_nt(a, b):
    return lax.dot_general(a, b, (((1,), (1,)), ((), ())), preferred_element_type=F32)


def _mxu_transpose(eye, x):
    return _dot_nt(eye, x)


def _sigmoid(x):
    return 1.0 / (1.0 + jnp.exp(-x))


def _log_sigmoid(x):
    return jnp.minimum(x, 0.0) - jnp.log1p(jnp.exp(-jnp.abs(x)))


def _split3(x):
    t1 = x.astype(BF16)
    r1 = x - t1.astype(F32)
    t2 = r1.astype(BF16)
    t3 = (r1 - t2.astype(F32)).astype(BF16)
    return t1, t2, t3


def _lower_tri(n):
    r = lax.broadcasted_iota(jnp.int32, (n, n), 0)
    c = lax.broadcasted_iota(jnp.int32, (n, n), 1)
    return c <= r


def _proj_kernel(x_ref, g_ref, wbig_ref, wsm_ref, bsm_ref, cw_ref, cb_ref, hist_ref,
                 fq_ref, fk32_ref, fv32_ref, fk16_ref, fv16_ref, sm_ref, flogf_ref, mq_ref, mk_ref, mv_ref,
                 mo_ref, cnew_ref, halo_ref):
    tm = x_ref.shape[1]

    @pl.when(pl.program_id(1) == 0)
    def _():
        halo_ref[...] = hist_ref[0]

    x = x_ref[0]
    xn = (x * lax.rsqrt(jnp.mean(x * x, axis=-1, keepdims=True) + EPS)) * g_ref[...]
    xb = xn.astype(BF16)
    z = _dot(xb, wbig_ref[...])

    fq_ref[0] = (z[:, B_FQ:B_FK] * (FOX_HD ** -0.5 * LOG2E)).astype(BF16)
    fk = z[:, B_FK:B_FV]
    fv = z[:, B_FV:B_QK]
    fk32_ref[0] = fk
    fv32_ref[0] = fv
    fk16_ref[0] = fk.astype(BF16)
    fv16_ref[0] = fv.astype(BF16)
    mv_ref[0] = z[:, B_MV:B_MO].astype(BF16)
    mo_ref[0] = _sigmoid(z[:, B_MO:B_END]).astype(BF16)

    zs = _dot(xb, wsm_ref[...]) + bsm_ref[...]
    lane = lax.broadcasted_iota(jnp.int32, zs.shape, 1)
    is_forget = (lane < S_MI) | ((lane >= S_MF) & (lane < S_END))
    sm = jnp.where(is_forget, _log_sigmoid(zs), zs)
    sm_ref[0] = sm
    flogf_ref[0] = sm[:, S_FF:S_MI]

    u = z[:, B_QK:B_MV]
    up = jnp.concatenate([halo_ref[...], u], axis=0)
    n = tm + V7X_SUBLANES
    first = V7X_SUBLANES - (CONV_W - 1)
    y = cb_ref[...] + cw_ref[0:1, :] * pltpu.roll(up, n - first, axis=0)[:tm]
    for j in range(1, CONV_W - 1):
        y = y + cw_ref[j:j + 1, :] * pltpu.roll(up, n - (first + j), axis=0)[:tm]
    y = y + cw_ref[CONV_W - 1:CONV_W, :] * u
    qk = y * _sigmoid(y)
    mq_ref[0] = qk[:, :QK_W // 2].astype(BF16)
    mk_ref[0] = (qk[:, QK_W // 2:] * (ML_DK ** -0.5)).astype(BF16)

    halo_ref[...] = u[tm - V7X_SUBLANES:, :]
    cnew_ref[0] = halo_ref[first:, :]


def _project(x, hist8, g, wbig, wsm, bsm, cw, cb, tm):
    bsz, seq, d = x.shape
    grid = (bsz, seq // tm)
    row = lambda c: pl.BlockSpec((1, tm, c), lambda b, s: (b, s, 0))
    const = lambda shape: pl.BlockSpec(shape, lambda b, s: (0,) * len(shape))
    outs = [
        (FOX_W, BF16), (FOX_W, F32), (FOX_W, F32), (FOX_W, BF16), (FOX_W, BF16), (V7X_LANES, F32),
        (FOX_HEADS, F32), (ML_W, BF16), (ML_W, BF16), (ML_W, BF16), (ML_W, BF16),
    ]
    out_shape = [jax.ShapeDtypeStruct((bsz, seq, c), dt) for c, dt in outs]
    out_specs = [row(c) for c, _ in outs]
    out_shape.append(jax.ShapeDtypeStruct((bsz, CONV_W - 1, QK_W), F32))
    out_specs.append(pl.BlockSpec((1, CONV_W - 1, QK_W), lambda b, s: (b, 0, 0)))
    return pl.pallas_call(
        _proj_kernel,
        grid=grid,
        in_specs=[row(d), const((1, d)), const(wbig.shape), const(wsm.shape), const((1, V7X_LANES)),
                  const((CONV_W, QK_W)), const((1, QK_W)),
                  pl.BlockSpec((1, V7X_SUBLANES, QK_W), lambda b, s: (b, 0, 0))],
        out_specs=out_specs,
        out_shape=out_shape,
        scratch_shapes=[pltpu.VMEM((V7X_SUBLANES, QK_W), F32)],
        compiler_params=_params("arbitrary", "arbitrary"),
        name="proj",
    )(x, g, wbig, wsm, bsm, cw, cb, hist8)


def _bias_lane(head, term):
    return (head // 2) * V7X_LANES + (FOX_HD if head % 2 == 0 else 0) + term


def _bias_placement():
    place = np.zeros((BIAS_TERMS, V7X_LANES, FOX_W), np.float32)
    for h in range(FOX_HEADS):
        for t in range(BIAS_TERMS):
            place[t, S_FF + h, _bias_lane(h, t)] = 1.0
    return jnp.asarray(place, BF16)


def _fox_prep_kernel(lf_ref, *refs, n_parts):
    k_refs, v_refs = refs[:n_parts], refs[n_parts:2 * n_parts]
    place_ref, ka0_ref, ka1_ref, va0_ref, va1_ref, carry_ref = refs[2 * n_parts:]
    tm = lf_ref.shape[1]

    def rows(part_refs):
        parts = [r[0].astype(F32) for r in part_refs]
        missing = tm - sum(p.shape[0] for p in parts)
        if missing:
            parts.append(jnp.zeros((missing, FOX_W), F32))
        return parts[0] if len(parts) == 1 else jnp.concatenate(parts, axis=0)

    @pl.when(pl.program_id(1) == 0)
    def _():
        carry_ref[...] = jnp.zeros_like(carry_ref)

    tri = jnp.where(_lower_tri(tm), 1.0, 0.0).astype(BF16)
    t1, t2, t3 = _split3(lf_ref[0])
    c = carry_ref[...] + (_dot(tri, t1) + (_dot(tri, t2) + _dot(tri, t3)))
    carry_ref[...] = c[tm - 1:tm, :]
    b1, b2, b3 = _split3(c * (-LOG2E))
    kb = _dot(b1, place_ref[0]) + (_dot(b2, place_ref[1]) + _dot(b3, place_ref[2]))

    k = rows(k_refs)
    v = rows(v_refs)
    first_head = lax.broadcasted_iota(jnp.int32, k.shape, 1) % V7X_LANES < FOX_HD
    ka0_ref[0] = jnp.where(first_head, k, kb).astype(BF16)
    ka1_ref[0] = jnp.where(first_head, kb, k).astype(BF16)
    va0_ref[0] = jnp.where(first_head, v, 1.0).astype(BF16)
    va1_ref[0] = jnp.where(first_head, 1.0, v).astype(BF16)


def _fox_prep(lf, k_parts, v_parts, tm):
    bsz, lk, _ = lf.shape
    place = _bias_placement()
    wide = pl.BlockSpec((1, tm, FOX_W), lambda b, s: (b, s, 0))
    if len(k_parts) == 1:
        part_specs = [wide, wide]
    else:
        assert lk == tm, "several key/value parts are only stitched inside a single tile"
        part_specs = [pl.BlockSpec((1,) + p.shape[1:], lambda b, s: (b, 0, 0)) for p in k_parts + v_parts]
    sds = jax.ShapeDtypeStruct((bsz, lk, FOX_W), BF16)
    return pl.pallas_call(
        functools.partial(_fox_prep_kernel, n_parts=len(k_parts)),
        grid=(bsz, lk // tm),
        in_specs=[pl.BlockSpec((1, tm, V7X_LANES), lambda b, s: (b, s, 0))] + part_specs
        + [pl.BlockSpec(place.shape, lambda b, s: (0, 0, 0))],
        out_specs=[wide, wide, wide, wide],
        out_shape=[sds, sds, sds, sds],
        scratch_shapes=[pltpu.VMEM((1, V7X_LANES), F32)],
        compiler_params=_params("arbitrary", "arbitrary"),
        name="fox_prep",
    )(lf, *k_parts, *v_parts, place)


def _fox_kernel(q_ref, ka0_ref, ka1_ref, va0_ref, va1_ref, g_ref, o_ref, *, tq, tk, sub, q_offset):
    ts = min(tq, sub)
    n_sub = tq // ts
    diag_blocks = max(ts // tk, 1)
    k_refs = (ka0_ref, ka1_ref)
    v_refs = (va0_ref, va1_ref)
    lane = lax.broadcasted_iota(jnp.int32, (ts, V7X_LANES), 1)
    first_head = lane < FOX_HD
    q_lo = q_offset + pl.program_id(2) * tq
    n_common = (q_lo + 1) // tk
    single_block = ka0_ref.shape[1] == tk
    even_common = q_offset % (2 * tk) == 0 and tq % (2 * tk) == 0

    def chain_q(r, hh):
        q = q_ref[0, r * ts:(r + 1) * ts, :].astype(F32)
        if hh == 0:
            return jnp.where(first_head, q, jnp.where(lane < FOX_HD + BIAS_TERMS, 1.0, 0.0)).astype(BF16)
        return jnp.where(first_head, jnp.where(lane < BIAS_TERMS, 1.0, 0.0), q).astype(BF16)

    chains = [(r, hh) for r in range(n_sub) for hh in range(2)]
    qs = [chain_q(r, hh) for r, hh in chains]

    def scores(c, j):
        return _dot_nt(qs[c], k_refs[chains[c][1]][0, pl.ds(pl.multiple_of(j * tk, tk), tk), :])

    def update(state, c, j, s, masked):
        r, hh = chains[c]
        m, acc = state
        start = pl.multiple_of(j * tk, tk)
        if masked:
            qpos = q_lo + r * ts + lax.broadcasted_iota(jnp.int32, s.shape, 0)
            kpos = start + lax.broadcasted_iota(jnp.int32, s.shape, 1)
            s = jnp.where(kpos <= qpos, s, NEG)
        m_new = jnp.maximum(m, jnp.max(s, axis=-1, keepdims=True))
        p = jnp.exp2(s - m_new).astype(BF16)
        acc = jnp.exp2(m - m_new) * acc + _dot(p, v_refs[hh][0, pl.ds(start, tk), :])
        return m_new, acc

    def step(state, c, j, masked):
        return update(state, c, j, scores(c, j), masked)

    def common(j, states):
        ss = [scores(c, j) for c in range(len(chains))]
        return tuple(update(st, c, j, ss[c], False) for c, st in enumerate(states))

    def common_run(t, states, nblk):
        states = list(states)
        cur = [scores(c, nblk * t) for c in range(len(chains))]
        for e in range(nblk):
            nxt = []
            for c in range(len(chains)):
                states[c] = update(states[c], c, nblk * t + e, cur[c], False)
                if e + 1 < nblk:
                    nxt.append(scores(c, nblk * t + e + 1))
            cur = nxt
        return tuple(states)

    init = (jnp.full((ts, 1), NEG, F32), jnp.zeros((ts, V7X_LANES), F32))
    states = (init,) * len(chains)
    if not single_block:
        done = 0
        for nblk in FOX_UNROLLS:
            if nblk == 1 and even_common:
                continue
            trips = n_common // nblk
            states = lax.fori_loop(done // nblk, trips, functools.partial(common_run, nblk=nblk), states)
            done = trips * nblk
    states = list(states)
    for c, (r, _) in enumerate(chains):
        for e in range(r * diag_blocks):
            states[c] = step(states[c], c, n_common + e, False)
        for e in range(diag_blocks):
            states[c] = step(states[c], c, n_common + r * diag_blocks + e, True)

    for r in range(n_sub):
        out = jnp.zeros((ts, V7X_LANES), F32)
        for hh in range(2):
            acc = states[chains.index((r, hh))][1]
            own = first_head if hh == 0 else jnp.logical_not(first_head)
            denom_lane = FOX_HD if hh == 0 else 0
            o = jnp.where(own, acc / acc[:, denom_lane:denom_lane + 1], 0.0)
            out = out + o * lax.rsqrt(jnp.sum(o * o, axis=-1, keepdims=True) * (1.0 / FOX_HD) + EPS)
        o_ref[0, r * ts:(r + 1) * ts, :] = (out * g_ref[...]).astype(BF16)


def _fox(q, ka0, ka1, va0, va1, g_fox, q_offset, tq, tk):
    bsz, lq, _ = q.shape
    lk = ka0.shape[1]
    assert (q_offset % tk == 0 and tq % tk == 0) or (lq == tq <= tk and lk == tk), (q_offset, tq, tk, lq, lk)
    pairs = FOX_HEADS // 2
    grid = (bsz, pairs, lq // tq)
    kv_spec = pl.BlockSpec((1, lk, V7X_LANES), lambda b, p, i: (b, 0, p))
    q_spec = pl.BlockSpec((1, tq, V7X_LANES), lambda b, p, i: (b, i, p))
    return pl.pallas_call(
        functools.partial(_fox_kernel, tq=tq, tk=tk, sub=FOX_SUB, q_offset=q_offset),
        grid=grid,
        in_specs=[q_spec, kv_spec, kv_spec, kv_spec, kv_spec,
                  pl.BlockSpec((1, V7X_LANES), lambda b, p, i: (0, p))],
        out_specs=q_spec,
        out_shape=jax.ShapeDtypeStruct((bsz, lq, FOX_W), BF16),
        compiler_params=_params("arbitrary", "arbitrary", "arbitrary"),
        name="fox_attention",
    )(q, ka0, ka1, va0, va1, g_fox)


def _mlstm_kernel(q_ref, k_ref, v_ref, mo_ref, sm_ref, c0_ref, n0_ref, m0_ref, g_ref,
                  o_ref, c_ref, n_ref, m_ref):
    bsz, L, _ = q_ref.shape

    @pl.when(pl.program_id(0) == 0)
    def _():
        c_ref[...] = c0_ref[...]
        n_ref[...] = n0_ref[...]
        m_ref[...] = m0_ref[...]

    causal = _lower_tri(L)
    tri = jnp.where(causal, 1.0, 0.0).astype(BF16)
    eye = jnp.where(lax.broadcasted_iota(jnp.int32, (V7X_LANES, V7X_LANES), 0)
                    == lax.broadcasted_iota(jnp.int32, (V7X_LANES, V7X_LANES), 1), 1.0, 0.0).astype(BF16)
    rows = lax.broadcasted_iota(jnp.int32, (L, V7X_LANES), 0)
    lanes = lax.broadcasted_iota(jnp.int32, (L, V7X_LANES), 1)
    gate_lanes = (lanes >= S_MI) & (lanes < S_END)

    heads = [(b, h) for b in range(bsz) for h in range(ML_HEADS)]
    cols = lambda h: slice(h * ML_DK, (h + 1) * ML_DK)

    qk, qc, n_prev = {}, {}, {}
    for b, h in heads:
        qh = q_ref[b, :, cols(h)]
        qk[b, h] = _dot_nt(qh, k_ref[b, :, cols(h)])
        qc[b, h] = _dot_nt(qh, c_ref[b, h].astype(BF16))

    gates = {}
    for b in range(bsz):
        sm = jnp.where(gate_lanes, sm_ref[b], 0.0)
        t1, t2, t3 = _split3(sm)
        bcum = _dot(tri, t1) + (_dot(tri, t2) + _dot(tri, t3))
        bcum = pltpu.roll(bcum, V7X_LANES - (S_MF - S_MI), axis=1)
        g = sm - bcum
        gmax = g
        sh = 1
        while sh < L:
            gmax = jnp.maximum(gmax, jnp.where(rows >= sh, pltpu.roll(gmax, sh, axis=0), -jnp.inf))
            sh *= 2
        m = m_ref[pl.ds(b, 1), :]
        u = jnp.maximum(m, gmax)
        a = jnp.exp(m - u)
        mt = bcum + u
        m_ref[pl.ds(b, 1), :] = mt[L - 1:L, :]
        g1, g2, g3 = _split3(g)
        g_rows = _mxu_transpose(eye, g1) + (_mxu_transpose(eye, g2) + _mxu_transpose(eye, g3))
        gates[b] = dict(u=u, a=a, em=jnp.exp(-mt), a_last=a[L - 1:L, :], wcol=jnp.exp(g - u[L - 1:L, :]),
                        g_rows=g_rows)

    for b, h in heads:
        gl = S_MI + h
        kh = k_ref[b, :, cols(h)]
        w_h = gates[b]["wcol"][:, gl:gl + 1]
        a_l = gates[b]["a_last"][:, gl:gl + 1]
        vw = (v_ref[b, :, cols(h)].astype(F32) * w_h).astype(BF16)
        n_old = n_ref[b, h:h + 1, :]
        n_prev[b, h] = n_old
        c_ref[b, h] = a_l * c_ref[b, h] + _dot(_mxu_transpose(eye, vw).astype(BF16), kh)
        n_ref[b, h:h + 1, :] = a_l * n_old + jnp.sum(kh.astype(F32) * w_h, axis=0, keepdims=True)

    for b, h in heads:
        gl = S_MI + h
        gb = gates[b]
        decay = jnp.exp(jnp.where(causal, gb["g_rows"][gl:gl + 1, :] - gb["u"][:, gl:gl + 1], NEG))
        s = qk[b, h] * decay
        a_h = gb["a"][:, gl:gl + 1]
        num = a_h * qc[b, h] + _dot(s.astype(BF16), v_ref[b, :, cols(h)])
        den = (a_h * jnp.sum(q_ref[b, :, cols(h)].astype(F32) * n_prev[b, h], axis=-1, keepdims=True)
               + jnp.sum(s, axis=-1, keepdims=True))
        hv = num / jnp.maximum(jnp.abs(den), gb["em"][:, gl:gl + 1])
        hn = hv * lax.rsqrt(jnp.mean(hv * hv, axis=-1, keepdims=True) + EPS)
        o_ref[b, :, cols(h)] = (hn * g_ref[:, cols(h)] * mo_ref[b, :, cols(h)].astype(F32)).astype(BF16)


def _mlstm(mq, mk, mv, mo, small, c0, n0, m0, g_ml, L):
    bsz, seq, _ = mq.shape
    chunk = pl.BlockSpec((bsz, L, ML_W), lambda c: (0, c, 0))
    full = lambda shape: pl.BlockSpec(shape, lambda c: (0,) * len(shape))
    return pl.pallas_call(
        _mlstm_kernel,
        grid=(seq // L,),
        in_specs=[chunk, chunk, chunk, chunk,
                  pl.BlockSpec((bsz, L, V7X_LANES), lambda c: (0, c, 0)),
                  full(c0.shape), full(n0.shape), full(m0.shape), full((1, ML_W))],
        out_specs=[chunk, full(c0.shape), full(n0.shape), full(m0.shape)],
        out_shape=[jax.ShapeDtypeStruct((bsz, seq, ML_W), BF16),
                   jax.ShapeDtypeStruct(c0.shape, F32),
                   jax.ShapeDtypeStruct(n0.shape, F32),
                   jax.ShapeDtypeStruct(m0.shape, F32)],
        compiler_params=_params("arbitrary"),
        name="mlstm",
    )(mq, mk, mv, mo, small, c0, n0, m0, g_ml)


def _merge_kernel(ca_ref, cm_ref, x_ref, wa_ref, wm_ref, g_ref, wrh_ref, wrl_ref, br_ref,
                  y1_ref, xn_ref, ridx_ref, rgate_ref, cnt_ref):
    tm = x_ref.shape[0]

    @pl.when(pl.program_id(0) == 0)
    def _():
        cnt_ref[...] = jnp.zeros_like(cnt_ref)

    y1 = x_ref[...] + (_dot(ca_ref[...], wa_ref[...]) + _dot(cm_ref[...], wm_ref[...]))
    y1_ref[...] = y1
    xn = (y1 * lax.rsqrt(jnp.mean(y1 * y1, axis=-1, keepdims=True) + EPS)) * g_ref[...]
    xn_ref[...] = xn.reshape(xn_ref.shape)

    xh = xn.astype(BF16)
    xl = (xn - xh.astype(F32)).astype(BF16)
    logits = (_dot(xh, wrh_ref[...]) + (_dot(xl, wrh_ref[...]) + _dot(xh, wrl_ref[...]))) + br_ref[...]

    lane = lax.broadcasted_iota(jnp.int32, logits.shape, 1)
    vals, sel = [], []
    work = logits
    for _ in range(TOP_K):
        mx = jnp.max(work, axis=-1, keepdims=True)
        idx = jnp.min(jnp.where(work == mx, lane, V7X_LANES), axis=-1, keepdims=True)
        vals.append(mx)
        sel.append(idx)
        work = jnp.where(lane == idx, -jnp.inf, work)
    ex = [jnp.exp(v - vals[0]) for v in vals]
    tot = ex[0] + ex[1] + ex[2] + ex[3]

    onehot = [(lane == idx) for idx in sel]
    picked = jnp.where(onehot[0] | onehot[1] | onehot[2] | onehot[3], 1.0, 0.0)
    earlier = (lax.broadcasted_iota(jnp.int32, (tm, tm), 1) < lax.broadcasted_iota(jnp.int32, (tm, tm), 0))
    before = _dot(jnp.where(earlier, 1.0, 0.0).astype(BF16), picked.astype(BF16)) + cnt_ref[...]
    cnt_ref[...] = cnt_ref[...] + jnp.sum(picked, axis=0, keepdims=True)

    ridx = jnp.zeros(logits.shape, jnp.int32)
    rgate = jnp.zeros(logits.shape, F32)
    for k in range(TOP_K):
        rank = jnp.sum(jnp.where(onehot[k], before, 0.0), axis=-1, keepdims=True).astype(jnp.int32)
        ridx = jnp.where(lane == k, sel[k], ridx)
        ridx = jnp.where(lane == TOP_K + k, rank, ridx)
        rgate = jnp.where(lane == k, ex[k] / tot, rgate)
    ridx_ref[...] = ridx
    rgate_ref[...] = rgate


def _merge(cat_a, cat_m, x, w_a, w_m, g_ffn, wr_hi, wr_lo, br, tm):
    t, d = x.shape
    row = lambda c: pl.BlockSpec((tm, c), lambda i: (i, 0))
    const = lambda shape: pl.BlockSpec(shape, lambda i: (0,) * len(shape))
    return pl.pallas_call(
        _merge_kernel,
        grid=(t // tm,),
        in_specs=[row(FOX_W), row(ML_W), row(d), const(w_a.shape), const(w_m.shape), const((1, d)),
                  const(wr_hi.shape), const(wr_lo.shape), const((1, V7X_LANES))],
        out_specs=[row(d), pl.BlockSpec((tm,) + _row_tile(d), lambda i: (i, 0, 0)), row(V7X_LANES),
                   row(V7X_LANES), const((1, V7X_LANES))],
        out_shape=[jax.ShapeDtypeStruct((t, d), F32), jax.ShapeDtypeStruct((t,) + _row_tile(d), F32),
                   jax.ShapeDtypeStruct((t, V7X_LANES), jnp.int32),
                   jax.ShapeDtypeStruct((t, V7X_LANES), F32),
                   jax.ShapeDtypeStruct((1, V7X_LANES), F32)],
        compiler_params=_params("arbitrary"),
        name="merge_router",
    )(cat_a, cat_m, x, w_a, w_m, g_ffn, wr_hi, wr_lo, br)


def _row_tile(d):
    return (d // V7X_LANES, V7X_LANES)


def _row_copy(src_ref, src_row, dst_ref, dst_row, sem):
    return pltpu.make_async_copy(src_ref.at[pl.ds(src_row, 1)], dst_ref.at[pl.ds(dst_row, 1)], sem)


def _scatter_rows(pos_ref, x_ref, xs_ref, sem):
    tm = x_ref.shape[0]

    def issue(g, _):
        for u in range(DMA_UNROLL):
            r = g * DMA_UNROLL + u
            for k in range(TOP_K):
                _row_copy(x_ref, r, xs_ref, pos_ref[0, 0, TOP_K * r + k], sem).start(priority=k % 2)
        return 0

    def drain(g, _):
        for _ in range(DMA_UNROLL * TOP_K):
            _row_copy(x_ref, 0, xs_ref, 0, sem).wait()
        return 0

    lax.fori_loop(0, tm // DMA_UNROLL, issue, 0)
    lax.fori_loop(0, tm // DMA_UNROLL, drain, 0)


def _dispatch_kernel(fs_ref, fl_ref, pos_p_ref, x_p_ref, pos_s_ref, x_s_ref, xs_ref, zero_ref, sem, *, ntp, nts):
    i = pl.program_id(0)

    @pl.when(i < ntp)
    def _():
        _scatter_rows(pos_p_ref, x_p_ref, xs_ref, sem)

    @pl.when((i >= ntp) & (i < ntp + nts))
    def _():
        _scatter_rows(pos_s_ref, x_s_ref, xs_ref, sem)

    @pl.when(i == ntp + nts)
    def _():
        zero_ref[...] = jnp.zeros_like(zero_ref)
        full = zero_ref.shape[0]
        bits = [1 << p for p in range(full.bit_length() - 2, -1, -1)]

        def zeros_to(start, size):
            return pltpu.make_async_copy(zero_ref.at[pl.ds(0, size)], xs_ref.at[pl.ds(start, size)], sem)

        def per_run(e, _, wait):
            start, n = fs_ref[e], fl_ref[e]
            whole = n // full

            def whole_copy(q, _):
                c = zeros_to(start + q * full, full)
                c.wait() if wait else c.start()
                return 0

            lax.fori_loop(0, whole, whole_copy, 0)
            for p in bits:
                @pl.when((n & p) != 0)
                def _():
                    c = zeros_to(start + (n & ~(2 * p - 1)), p)
                    c.wait() if wait else c.start()
            return 0

        lax.fori_loop(0, fs_ref.shape[0], functools.partial(per_run, wait=False), 0)
        lax.fori_loop(0, fs_ref.shape[0], functools.partial(per_run, wait=True), 0)


def _dispatch(fill_start, fill_len, pos3_p, x_p, pos3_s, x_s, n_rows, tm_p, tm_s):
    tp, sub, lanes = x_p.shape
    ntp, nts = tp // tm_p, x_s.shape[0] // tm_s
    clamp_p = lambda i, fs, fl: (jnp.minimum(i, ntp - 1), 0, 0)
    clamp_s = lambda i, fs, fl: (jnp.clip(i - ntp, 0, nts - 1), 0, 0)
    grid_spec = pltpu.PrefetchScalarGridSpec(
        num_scalar_prefetch=2,
        grid=(ntp + nts + 1,),
        in_specs=[pl.BlockSpec((1, 1, TOP_K * tm_p), clamp_p, memory_space=pltpu.SMEM),
                  pl.BlockSpec((tm_p, sub, lanes), clamp_p),
                  pl.BlockSpec((1, 1, TOP_K * tm_s), clamp_s, memory_space=pltpu.SMEM),
                  pl.BlockSpec((tm_s, sub, lanes), clamp_s)],
        out_specs=pl.BlockSpec(memory_space=pl.ANY),
        scratch_shapes=[pltpu.VMEM((FFN_ROWS, sub, lanes), F32), pltpu.SemaphoreType.DMA(())],
    )
    return pl.pallas_call(
        functools.partial(_dispatch_kernel, ntp=ntp, nts=nts),
        grid_spec=grid_spec,
        out_shape=jax.ShapeDtypeStruct((n_rows, sub, lanes), F32),
        compiler_params=_params("arbitrary"),
        name="moe_dispatch",
    )(fill_start, fill_len, pos3_p, x_p, pos3_s, x_s)


def _ffn_kernel(te_ref, nu_ref, x_ref, wgu_ref, bgu_ref, wd_ref, bd_ref, y_ref, wgu16_ref, wd16_ref):
    i = pl.program_id(0)
    d_ff = wd_ref.shape[1]

    @pl.when((i == 0) | (te_ref[i] != te_ref[jnp.maximum(i - 1, 0)]))
    def _():
        wgu16_ref[...] = wgu_ref[0].astype(BF16)
        wd16_ref[...] = wd_ref[0].astype(BF16)

    @pl.when(i < nu_ref[0])
    def _():
        x = x_ref[...].reshape(x_ref.shape[0], -1)
        gu = _dot(x.astype(BF16), wgu16_ref[...]) + bgu_ref[0]
        gate = jnp.minimum(gu[:, :d_ff], SWIGLU_LIMIT)
        up = jnp.clip(gu[:, d_ff:], -SWIGLU_LIMIT, SWIGLU_LIMIT)
        h = (up + 1.0) * (gate * _sigmoid(SWIGLU_ALPHA * gate))
        y = _dot(h.astype(BF16), wd16_ref[...]) + bd_ref[0]
        y_ref[...] = y.reshape(y_ref.shape)

    @pl.when(i >= nu_ref[0])
    def _():
        y_ref[...] = jnp.zeros_like(y_ref)


def _ffn(tile_expert, n_used, xs, w_gu, b_gu, w_d, b_d, tm):
    r = xs.shape[0]
    d_ff, d = w_d.shape[1:]
    tiles = pl.BlockSpec((tm,) + _row_tile(d), lambda i, te, nu: (i, 0, 0))
    grid_spec = pltpu.PrefetchScalarGridSpec(
        num_scalar_prefetch=2,
        grid=(r // tm,),
        in_specs=[tiles,
                  pl.BlockSpec((1, d, 2 * d_ff), lambda i, te, nu: (te[i], 0, 0)),
                  pl.BlockSpec((1, 1, 2 * d_ff), lambda i, te, nu: (te[i], 0, 0)),
                  pl.BlockSpec((1, d_ff, d), lambda i, te, nu: (te[i], 0, 0)),
                  pl.BlockSpec((1, 1, d), lambda i, te, nu: (te[i], 0, 0))],
        out_specs=tiles,
        scratch_shapes=[pltpu.VMEM((d, 2 * d_ff), BF16), pltpu.VMEM((d_ff, d), BF16)],
    )
    return pl.pallas_call(
        _ffn_kernel,
        grid_spec=grid_spec,
        out_shape=jax.ShapeDtypeStruct(xs.shape, F32),
        compiler_params=_params("arbitrary"),
        name="moe_ffn",
    )(tile_expert, n_used, xs, w_gu, b_gu, w_d, b_d)


def _combine_kernel(pos_ref, pos_next_ref, y1_ref, gate_ref, gfin_ref, ys_ref, o_ref, buf_ref, sem):
    tm = y1_ref.shape[0]
    i = pl.program_id(0)
    slot = i % 2

    def gather(p_ref, s):
        def issue(g, _):
            for u in range(DMA_UNROLL):
                r = g * DMA_UNROLL + u
                for k in range(TOP_K):
                    _row_copy(ys_ref, p_ref[0, 0, TOP_K * r + k], buf_ref.at[s, k], r,
                              sem.at[s]).start(priority=k % 2)
            return 0

        lax.fori_loop(0, tm // DMA_UNROLL, issue, 0)

    @pl.when(i == 0)
    def _():
        gather(pos_ref, 0)

    @pl.when(i + 1 < pl.num_programs(0))
    def _():
        gather(pos_next_ref, 1 - slot)

    def drain(g, _):
        for _ in range(DMA_UNROLL * TOP_K):
            _row_copy(ys_ref, 0, buf_ref.at[slot, 0], 0, sem.at[slot]).wait()
        return 0

    lax.fori_loop(0, tm // DMA_UNROLL, drain, 0)

    gate = gate_ref[...]
    moe = gate[:, 0:1] * buf_ref[slot, 0].reshape(tm, -1)
    for k in range(1, TOP_K):
        moe = moe + gate[:, k:k + 1] * buf_ref[slot, k].reshape(tm, -1)
    y = y1_ref[...] + moe
    o_ref[...] = (y * lax.rsqrt(jnp.mean(y * y, axis=-1, keepdims=True) + EPS)) * gfin_ref[...]


def _combine(pos3, y1, rgate, g_final, ys, tm):
    t, d = y1.shape
    nt = t // tm
    pos_spec = lambda f: pl.BlockSpec((1, 1, TOP_K * tm), f, memory_space=pltpu.SMEM)
    return pl.pallas_call(
        _combine_kernel,
        grid=(nt,),
        in_specs=[pos_spec(lambda i: (i, 0, 0)), pos_spec(lambda i: (jnp.minimum(i + 1, nt - 1), 0, 0)),
                  pl.BlockSpec((tm, d), lambda i: (i, 0)),
                  pl.BlockSpec((tm, V7X_LANES), lambda i: (i, 0)),
                  pl.BlockSpec((1, d), lambda i: (0, 0)),
                  pl.BlockSpec(memory_space=pl.ANY)],
        out_specs=pl.BlockSpec((tm, d), lambda i: (i, 0)),
        out_shape=jax.ShapeDtypeStruct((t, d), F32),
        scratch_shapes=[pltpu.VMEM((2, TOP_K, tm) + _row_tile(d), F32), pltpu.SemaphoreType.DMA((2,))],
        compiler_params=_params("arbitrary"),
        name="moe_combine",
    )(pos3, pos3, y1, rgate, g_final, ys)


def _mixer(x, hist, fox_cache, ml_state, w, tm, tq, tk, chunk):
    bsz, seq, _ = x.shape
    hist8 = jnp.pad(hist, ((0, 0), (V7X_SUBLANES - (CONV_W - 1), 0), (0, 0)))
    (fq, fk32, fv32, fk16, fv16, small, flogf, mq, mk, mv, mo, conv_new) = _project(
        x, hist8, w["g_mix"], w["wbig"], w["wsm"], w["bsm"], w["conv_w"], w["conv_b"], tm)

    if fox_cache is None:
        q_offset = 0
        assert seq % tk == 0
        ka0, ka1, va0, va1 = _fox_prep(small, [fk16], [fv16], math.gcd(tk, FOX_BLOCK))
    else:
        ck_c, cv_c, clogf_c = fox_cache
        q_offset = ck_c.shape[1]
        assert q_offset + seq <= tk
        clogf_c = jnp.pad(clogf_c, ((0, 0), (0, 0), (S_FF, V7X_LANES - S_FF - FOX_HEADS)))
        lf_all = jnp.pad(jnp.concatenate([clogf_c, small], axis=1), ((0, 0), (0, tk - q_offset - seq), (0, 0)))
        ka0, ka1, va0, va1 = _fox_prep(lf_all, [ck_c.reshape(bsz, q_offset, FOX_W), fk16],
                                       [cv_c.reshape(bsz, q_offset, FOX_W), fv16], tk)
    cat_a = _fox(fq, ka0, ka1, va0, va1, w["g_fox"], q_offset, tq, tk)

    c0, n0, m0 = ml_state
    m0 = jnp.pad(m0, ((0, 0), (S_MI, V7X_LANES - S_MI - ML_HEADS)))
    cat_m, c_new, n_new, m_new = _mlstm(mq, mk, mv, mo, small, c0, n0, m0, w["g_ml"], chunk)
    m_new = m_new[:, S_MI:S_MI + ML_HEADS]

    states = (fk32.reshape(bsz, seq, FOX_HEADS, FOX_HD), fv32.reshape(bsz, seq, FOX_HEADS, FOX_HD), flogf,
              c_new, n_new, m_new, conv_new)
    return cat_a, cat_m, states


def kernel(x_prompt, x_sample, cache_fox_k, cache_fox_v, cache_fox_logf, state_mlstm_C, state_mlstm_n,
           state_mlstm_m, state_mlstm_conv, norm_mix_g, w_in, b_fox_f, conv_w, conv_b, b_ml_i, b_ml_f,
           g_fox, g_ml, w_out, norm_ffn_g, w_router, b_router, w_gate_up, b_gate_up, w_down, b_down,
           norm_final_g):
    depth = w_in.shape[0]
    assert depth == 1, "the final norm is fused into the last layer's combine; only depth 1 is wired up"
    bp, sp, d = x_prompt.shape
    bs, ss, _ = x_sample.shape
    n_exp = w_router.shape[-1]
    d_ff = w_down.shape[2]
    yp, ys = x_prompt, x_sample
    p_st, s_st = [], []

    for l in range(depth):
        wl = w_in[l]
        w = {
            "g_mix": norm_mix_g[l][None, :],
            "wbig": jnp.concatenate([wl[:, O_FQ:O_FF], wl[:, O_MQ:O_MI]], axis=1).astype(BF16),
            "wsm": jnp.pad(jnp.concatenate([wl[:, O_FF:O_MQ], wl[:, O_MI:P_IN]], axis=1),
                           ((0, 0), (0, V7X_LANES - S_END))).astype(BF16),
            "bsm": jnp.pad(jnp.concatenate([b_fox_f[l], b_ml_i[l], b_ml_f[l]]), (0, V7X_LANES - S_END))[None, :],
            "conv_w": conv_w[l],
            "conv_b": conv_b[l][None, :],
            "g_fox": g_fox[l][None, :],
            "g_ml": g_ml[l][None, :],
        }
        w_a = w_out[l][:FOX_W].astype(BF16)
        w_m = w_out[l][FOX_W:].astype(BF16)
        g_ffn = norm_ffn_g[l][None, :]
        wr = jnp.pad(w_router[l], ((0, 0), (0, V7X_LANES - n_exp)))
        wr_hi = wr.astype(BF16)
        wr_lo = (wr - wr_hi.astype(F32)).astype(BF16)
        br = jnp.pad(b_router[l], (0, V7X_LANES - n_exp), constant_values=NEG)[None, :]

        zeros_state = (jnp.zeros((bp, ML_HEADS, ML_DV, ML_DK), F32), jnp.zeros((bp, ML_HEADS, ML_DK), F32),
                       jnp.zeros((bp, ML_HEADS), F32))
        cat_a_p, cat_m_p, st_p = _mixer(yp, jnp.zeros((bp, CONV_W - 1, QK_W), F32), None, zeros_state, w,
                                        min(PROJ_ROWS, sp), min(FOX_QBLOCK, sp), min(FOX_BLOCK, sp), CHUNK)
        p_st.append(st_p)
        past = cache_fox_k.shape[2]
        tk_s = -(-(past + ss) // V7X_LANES) * V7X_LANES
        cat_a_s, cat_m_s, st_s = _mixer(
            ys, state_mlstm_conv[l], (cache_fox_k[l], cache_fox_v[l], cache_fox_logf[l]),
            (state_mlstm_C[l].astype(F32), state_mlstm_n[l].astype(F32), state_mlstm_m[l].astype(F32)),
            w, ss, ss, tk_s, ss)
        s_st.append(st_s)

        tp, ts = bp * sp, bs * ss
        y1_p, xn_p, ridx_p, rgate_p, cnt_p = _merge(
            cat_a_p.reshape(tp, FOX_W), cat_m_p.reshape(tp, ML_W), yp.reshape(tp, d), w_a, w_m, g_ffn,
            wr_hi, wr_lo, br, min(MERGE_ROWS, tp))
        y1_s, xn_s, ridx_s, rgate_s, cnt_s = _merge(
            cat_a_s.reshape(ts, FOX_W), cat_m_s.reshape(ts, ML_W), ys.reshape(ts, d), w_a, w_m, g_ffn,
            wr_hi, wr_lo, br, min(MERGE_ROWS, ts))

        cnt_p = cnt_p[0, :n_exp].astype(jnp.int32)
        cnt_s = cnt_s[0, :n_exp].astype(jnp.int32)
        seg_rows = -(-(cnt_p + cnt_s) // FFN_ROWS) * FFN_ROWS
        seg_end = jnp.cumsum(seg_rows)
        seg_start = seg_end - seg_rows
        n_rows = -(-((tp + ts) * TOP_K + n_exp * (FFN_ROWS - 1)) // FFN_ROWS) * FFN_ROWS
        n_tiles = n_rows // FFN_ROWS
        tile_row0 = jnp.arange(n_tiles, dtype=jnp.int32) * FFN_ROWS
        tile_expert = jnp.minimum(jnp.sum((seg_end[None, :] <= tile_row0[:, None]).astype(jnp.int32), axis=1),
                                  n_exp - 1)
        n_used = (seg_end[-1:] // FFN_ROWS).astype(jnp.int32)
        e_p, rank_p = ridx_p[:, :TOP_K], ridx_p[:, TOP_K:2 * TOP_K]
        e_s, rank_s = ridx_s[:, :TOP_K], ridx_s[:, TOP_K:2 * TOP_K]
        pos_p = seg_start[e_p] + rank_p
        pos_s = seg_start[e_s] + cnt_p[e_s] + rank_s
        tm_p, tm_s = min(MOE_ROWS, tp), min(MOE_ROWS, ts)
        pos3_p = pos_p.reshape(tp // tm_p, 1, TOP_K * tm_p)
        pos3_s = pos_s.reshape(ts // tm_s, 1, TOP_K * tm_s)

        cnt = cnt_p + cnt_s
        fill_start = jnp.concatenate([seg_start + cnt, seg_end[-1:]]).astype(jnp.int32)
        fill_len = jnp.concatenate([seg_rows - cnt, n_rows - seg_end[-1:]]).astype(jnp.int32)
        xs = _dispatch(fill_start, fill_len, pos3_p, xn_p, pos3_s, xn_s, n_rows, tm_p, tm_s)
        ysort = _ffn(tile_expert, n_used, xs, w_gate_up[l], b_gate_up[l].reshape(n_exp, 1, 2 * d_ff),
                     w_down[l], b_down[l].reshape(n_exp, 1, d), FFN_ROWS)

        g_fin = norm_final_g[None, :]
        yp = _combine(pos3_p, y1_p, rgate_p, g_fin, ysort, tm_p).reshape(bp, sp, d)
        ys = _combine(pos3_s, y1_s, rgate_s, g_fin, ysort, tm_s).reshape(bs, ss, d)

    p_out = tuple(jnp.stack(a) for a in zip(*p_st))
    s_out = tuple(jnp.stack(a) for a in zip(*s_st))
    return (yp, ys) + p_out + s_out
```

```python
import functools
import math

import numpy as np

import jax
import jax.numpy as jnp
from jax import lax
from jax.experimental import pallas as pl
from jax.experimental.pallas import tpu as pltpu

F32 = jnp.float32
BF16 = jnp.bfloat16

FOX_HEADS = 8
FOX_HD = 64
FOX_W = FOX_HEADS * FOX_HD
ML_HEADS = 4
ML_DK = 128
ML_DV = 128
ML_W = ML_HEADS * ML_DV
QK_W = 2 * ML_HEADS * ML_DK
CONV_W = 4
CHUNK = 64
TOP_K = 4
ROUTE_W = 2 * TOP_K
SWIGLU_LIMIT = 7.0
SWIGLU_ALPHA = 1.702
EPS = 1e-6
NEG = -1e30
LOG2E = math.log2(math.e)

O_FQ = 0
O_FK = O_FQ + FOX_W
O_FV = O_FK + FOX_W
O_FF = O_FV + FOX_W
O_MQ = O_FF + FOX_HEADS
O_MK = O_MQ + ML_HEADS * ML_DK
O_MV = O_MK + ML_HEADS * ML_DK
O_MO = O_MV + ML_W
O_MI = O_MO + ML_W
O_MF = O_MI + ML_HEADS
P_IN = O_MF + ML_HEADS

B_FQ, B_FK, B_FV, B_QK, B_MV, B_MO, B_END = 0, 512, 1024, 1536, 2560, 3072, 3584
S_FF, S_MI, S_MF, S_END = 0, 8, 12, 16
BIAS_TERMS = 3

V7X_LANES = 128
V7X_SUBLANES = 8
V7X_VMEM_LIMIT_BYTES = 56 * 1024 * 1024

PROJ_ROWS = 512
FOX_BLOCK = 512
FOX_QBLOCK = 1024
FOX_SUB = 512
FOX_UNROLLS = (4, 2, 1)
MERGE_ROWS = 512
MOE_ROWS = 512
FFN_ROWS = 512
DMA_UNROLL = 8


def _params(*semantics):
    return pltpu.CompilerParams(dimension_semantics=semantics, vmem_limit_bytes=V7X_VMEM_LIMIT_BYTES)


def _dot(a, b):
    return jnp.dot(a, b, preferred_element_type=F32)


def _dot_nt(a, b):
    return lax.dot_general(a, b, (((1,), (1,)), ((), ())), preferred_element_type=F32)


def _mxu_transpose(eye, x):
    return _dot_nt(eye, x)


def _sigmoid(x):
    return 1.0 / (1.0 + jnp.exp(-x))


def _log_sigmoid(x):
    return jnp.minimum(x, 0.0) - jnp.log1p(jnp.exp(-jnp.abs(x)))


def _split3(x):
    t1 = x.astype(BF16)
    r1 = x - t1.astype(F32)
    t2 = r1.astype(BF16)
    t3 = (r1 - t2.astype(F32)).astype(BF16)
    return t1, t2, t3


def _lower_tri(n):
    r = lax.broadcasted_iota(jnp.int32, (n, n), 0)
    c = lax.broadcasted_iota(jnp.int32, (n, n), 1)
    return c <= r


def _proj_kernel(x_ref, g_ref, wbig_ref, wsm_ref, bsm_ref, cw_ref, cb_ref, hist_ref,
                 fq_ref, fk32_ref, fv32_ref, fk16_ref, fv16_ref, sm_ref, flogf_ref, mq_ref, mk_ref, mv_ref,
                 mo_ref, cnew_ref, halo_ref):
    tm = x_ref.shape[1]

    @pl.when(pl.program_id(1) == 0)
    def _():
        halo_ref[...] = hist_ref[0]

    x = x_ref[0]
    xn = (x * lax.rsqrt(jnp.mean(x * x, axis=-1, keepdims=True) + EPS)) * g_ref[...]
    xb = xn.astype(BF16)
    z = _dot(xb, wbig_ref[...])

    fq_ref[0] = (z[:, B_FQ:B_FK] * (FOX_HD ** -0.5 * LOG2E)).astype(BF16)
    fk = z[:, B_FK:B_FV]
    fv = z[:, B_FV:B_QK]
    fk32_ref[0] = fk
    fv32_ref[0] = fv
    fk16_ref[0] = fk.astype(BF16)
    fv16_ref[0] = fv.astype(BF16)
    mv_ref[0] = z[:, B_MV:B_MO].astype(BF16)
    mo_ref[0] = _sigmoid(z[:, B_MO:B_END]).astype(BF16)

    zs = _dot(xb, wsm_ref[...]) + bsm_ref[...]
    lane = lax.broadcasted_iota(jnp.int32, zs.shape, 1)
    is_forget = (lane < S_MI) | ((lane >= S_MF) & (lane < S_END))
    sm = jnp.where(is_forget, _log_sigmoid(zs), zs)
    sm_ref[0] = sm
    flogf_ref[0] = sm[:, S_FF:S_MI]

    u = z[:, B_QK:B_MV]
    up = jnp.concatenate([halo_ref[...], u], axis=0)
    n = tm + V7X_SUBLANES
    first = V7X_SUBLANES - (CONV_W - 1)
    y = cb_ref[...] + cw_ref[0:1, :] * pltpu.roll(up, n - first, axis=0)[:tm]
    for j in range(1, CONV_W - 1):
        y = y + cw_ref[j:j + 1, :] * pltpu.roll(up, n - (first + j), axis=0)[:tm]
    y = y + cw_ref[CONV_W - 1:CONV_W, :] * u
    qk = y * _sigmoid(y)
    mq_ref[0] = qk[:, :QK_W // 2].astype(BF16)
    mk_ref[0] = (qk[:, QK_W // 2:] * (ML_DK ** -0.5)).astype(BF16)

    halo_ref[...] = u[tm - V7X_SUBLANES:, :]
    cnew_ref[0] = halo_ref[first:, :]


def _project(x, hist8, g, wbig, wsm, bsm, cw, cb, tm):
    bsz, seq, d = x.shape
    grid = (bsz, seq // tm)
    row = lambda c: pl.BlockSpec((1, tm, c), lambda b, s: (b, s, 0))
    const = lambda shape: pl.BlockSpec(shape, lambda b, s: (0,) * len(shape))
    outs = [
        (FOX_W, BF16), (FOX_W, F32), (FOX_W, F32), (FOX_W, BF16), (FOX_W, BF16), (V7X_LANES, F32),
        (FOX_HEADS, F32), (ML_W, BF16), (ML_W, BF16), (ML_W, BF16), (ML_W, BF16),
    ]
    out_shape = [jax.ShapeDtypeStruct((bsz, seq, c), dt) for c, dt in outs]
    out_specs = [row(c) for c, _ in outs]
    out_shape.append(jax.ShapeDtypeStruct((bsz, CONV_W - 1, QK_W), F32))
    out_specs.append(pl.BlockSpec((1, CONV_W - 1, QK_W), lambda b, s: (b, 0, 0)))
    return pl.pallas_call(
        _proj_kernel,
        grid=grid,
        in_specs=[row(d), const((1, d)), const(wbig.shape), const(wsm.shape), const((1, V7X_LANES)),
                  const((CONV_W, QK_W)), const((1, QK_W)),
                  pl.BlockSpec((1, V7X_SUBLANES, QK_W), lambda b, s: (b, 0, 0))],
        out_specs=out_specs,
        out_shape=out_shape,
        scratch_shapes=[pltpu.VMEM((V7X_SUBLANES, QK_W), F32)],
        compiler_params=_params("arbitrary", "arbitrary"),
        name="proj",
    )(x, g, wbig, wsm, bsm, cw, cb, hist8)


def _bias_lane(head, term):
    return (head // 2) * V7X_LANES + (FOX_HD if head % 2 == 0 else 0) + term


def _bias_placement():
    place = np.zeros((BIAS_TERMS, V7X_LANES, FOX_W), np.float32)
    for h in range(FOX_HEADS):
        for t in range(BIAS_TERMS):
            place[t, S_FF + h, _bias_lane(h, t)] = 1.0
    return jnp.asarray(place, BF16)


def _fox_prep_kernel(lf_ref, *refs, n_parts):
    k_refs, v_refs = refs[:n_parts], refs[n_parts:2 * n_parts]
    place_ref, ka0_ref, ka1_ref, va0_ref, va1_ref, carry_ref = refs[2 * n_parts:]
    tm = lf_ref.shape[1]

    def rows(part_refs):
        parts = [r[0].reshape(r.shape[1], FOX_W).astype(F32) for r in part_refs]
        missing = tm - sum(p.shape[0] for p in parts)
        if missing:
            parts.append(jnp.zeros((missing, FOX_W), F32))
        return parts[0] if len(parts) == 1 else jnp.concatenate(parts, axis=0)

    @pl.when(pl.program_id(1) == 0)
    def _():
        carry_ref[...] = jnp.zeros_like(carry_ref)

    tri = jnp.where(_lower_tri(tm), 1.0, 0.0).astype(BF16)
    t1, t2, t3 = _split3(lf_ref[0])
    c = carry_ref[...] + (_dot(tri, t1) + (_dot(tri, t2) + _dot(tri, t3)))
    carry_ref[...] = c[tm - 1:tm, :]
    b1, b2, b3 = _split3(c * (-LOG2E))
    kb = _dot(b1, place_ref[0]) + (_dot(b2, place_ref[1]) + _dot(b3, place_ref[2]))

    k = rows(k_refs)
    v = rows(v_refs)
    first_head = lax.broadcasted_iota(jnp.int32, k.shape, 1) % V7X_LANES < FOX_HD
    ka0_ref[0] = jnp.where(first_head, k, kb).astype(BF16)
    ka1_ref[0] = jnp.where(first_head, kb, k).astype(BF16)
    va0_ref[0] = jnp.where(first_head, v, 1.0).astype(BF16)
    va1_ref[0] = jnp.where(first_head, 1.0, v).astype(BF16)


def _fox_prep(lf, k_parts, v_parts, tm):
    bsz, lk, _ = lf.shape
    place = _bias_placement()
    wide = pl.BlockSpec((1, tm, FOX_W), lambda b, s: (b, s, 0))
    if len(k_parts) == 1:
        part_specs = [wide, wide]
    else:
        assert lk == tm, "several key/value parts are only stitched inside a single tile"
        part_specs = [pl.BlockSpec((1,) + p.shape[1:], lambda b, s, nd=p.ndim: (b,) + (0,) * (nd - 1))
                      for p in k_parts + v_parts]
    sds = jax.ShapeDtypeStruct((bsz, lk, FOX_W), BF16)
    return pl.pallas_call(
        functools.partial(_fox_prep_kernel, n_parts=len(k_parts)),
        grid=(bsz, lk // tm),
        in_specs=[pl.BlockSpec((1, tm, V7X_LANES), lambda b, s: (b, s, 0))] + part_specs
        + [pl.BlockSpec(place.shape, lambda b, s: (0, 0, 0))],
        out_specs=[wide, wide, wide, wide],
        out_shape=[sds, sds, sds, sds],
        scratch_shapes=[pltpu.VMEM((1, V7X_LANES), F32)],
        compiler_params=_params("arbitrary", "arbitrary"),
        name="fox_prep",
    )(lf, *k_parts, *v_parts, place)


def _fox_kernel(q_ref, ka0_ref, ka1_ref, va0_ref, va1_ref, g_ref, o_ref, *, tq, tk, sub, q_offset):
    ts = min(tq, sub)
    n_sub = tq // ts
    diag_blocks = max(ts // tk, 1)
    k_refs = (ka0_ref, ka1_ref)
    v_refs = (va0_ref, va1_ref)
    lane = lax.broadcasted_iota(jnp.int32, (ts, V7X_LANES), 1)
    first_head = lane < FOX_HD
    q_lo = q_offset + pl.program_id(2) * tq
    n_common = (q_lo + 1) // tk
    single_block = ka0_ref.shape[1] == tk
    even_common = q_offset % (2 * tk) == 0 and tq % (2 * tk) == 0

    def chain_q(r, hh):
        q = q_ref[0, r * ts:(r + 1) * ts, :].astype(F32)
        if hh == 0:
            return jnp.where(first_head, q, jnp.where(lane < FOX_HD + BIAS_TERMS, 1.0, 0.0)).astype(BF16)
        return jnp.where(first_head, jnp.where(lane < BIAS_TERMS, 1.0, 0.0), q).astype(BF16)

    chains = [(r, hh) for r in range(n_sub) for hh in range(2)]
    qs = [chain_q(r, hh) for r, hh in chains]

    def scores(c, j):
        return _dot_nt(qs[c], k_refs[chains[c][1]][0, pl.ds(pl.multiple_of(j * tk, tk), tk), :])

    def update(state, c, j, s, masked):
        r, hh = chains[c]
        m, acc = state
        start = pl.multiple_of(j * tk, tk)
        if masked:
            qpos = q_lo + r * ts + lax.broadcasted_iota(jnp.int32, s.shape, 0)
            kpos = start + lax.broadcasted_iota(jnp.int32, s.shape, 1)
            s = jnp.where(kpos <= qpos, s, NEG)
        m_new = jnp.maximum(m, jnp.max(s, axis=-1, keepdims=True))
        p = jnp.exp2(s - m_new).astype(BF16)
        acc = jnp.exp2(m - m_new) * acc + _dot(p, v_refs[hh][0, pl.ds(start, tk), :])
        return m_new, acc

    def step(state, c, j, masked):
        return update(state, c, j, scores(c, j), masked)

    def common(j, states):
        ss = [scores(c, j) for c in range(len(chains))]
        return tuple(update(st, c, j, ss[c], False) for c, st in enumerate(states))

    def common_run(t, states, nblk):
        states = list(states)
        cur = [scores(c, nblk * t) for c in range(len(chains))]
        for e in range(nblk):
            nxt = []
            for c in range(len(chains)):
                states[c] = update(states[c], c, nblk * t + e, cur[c], False)
                if e + 1 < nblk:
                    nxt.append(scores(c, nblk * t + e + 1))
            cur = nxt
        return tuple(states)

    init = (jnp.full((ts, 1), NEG, F32), jnp.zeros((ts, V7X_LANES), F32))
    states = (init,) * len(chains)
    if not single_block:
        done = 0
        for nblk in FOX_UNROLLS:
            if nblk == 1 and even_common:
                continue
            trips = n_common // nblk
            states = lax.fori_loop(done // nblk, trips, functools.partial(common_run, nblk=nblk), states)
            done = trips * nblk
    states = list(states)
    for c, (r, _) in enumerate(chains):
        for e in range(r * diag_blocks):
            states[c] = step(states[c], c, n_common + e, False)
        for e in range(diag_blocks):
            states[c] = step(states[c], c, n_common + r * diag_blocks + e, True)

    for r in range(n_sub):
        out = jnp.zeros((ts, V7X_LANES), F32)
        for hh in range(2):
            acc = states[chains.index((r, hh))][1]
            own = first_head if hh == 0 else jnp.logical_not(first_head)
            denom_lane = FOX_HD if hh == 0 else 0
            o = jnp.where(own, acc / acc[:, denom_lane:denom_lane + 1], 0.0)
            out = out + o * lax.rsqrt(jnp.sum(o * o, axis=-1, keepdims=True) * (1.0 / FOX_HD) + EPS)
        o_ref[0, r * ts:(r + 1) * ts, :] = (out * g_ref[...]).astype(BF16)


def _fox(q, ka0, ka1, va0, va1, g_fox, q_offset, tq, tk):
    bsz, lq, _ = q.shape
    lk = ka0.shape[1]
    assert (q_offset % tk == 0 and tq % tk == 0) or (lq == tq <= tk and lk == tk), (q_offset, tq, tk, lq, lk)
    pairs = FOX_HEADS // 2
    grid = (bsz, pairs, lq // tq)
    kv_spec = pl.BlockSpec((1, lk, V7X_LANES), lambda b, p, i: (b, 0, p))
    q_spec = pl.BlockSpec((1, tq, V7X_LANES), lambda b, p, i: (b, i, p))
    return pl.pallas_call(
        functools.partial(_fox_kernel, tq=tq, tk=tk, sub=FOX_SUB, q_offset=q_offset),
        grid=grid,
        in_specs=[q_spec, kv_spec, kv_spec, kv_spec, kv_spec,
                  pl.BlockSpec((1, V7X_LANES), lambda b, p, i: (0, p))],
        out_specs=q_spec,
        out_shape=jax.ShapeDtypeStruct((bsz, lq, FOX_W), BF16),
        compiler_params=_params("arbitrary", "arbitrary", "arbitrary"),
        name="fox_attention",
    )(q, ka0, ka1, va0, va1, g_fox)


def _mlstm_kernel(q_ref, k_ref, v_ref, mo_ref, sm_ref, c0_ref, n0_ref, m0_ref, g_ref,
                  o_ref, c_ref, n_ref, m_ref):
    bsz, L, _ = q_ref.shape

    @pl.when(pl.program_id(0) == 0)
    def _():
        c_ref[...] = c0_ref[...]
        n_ref[...] = n0_ref[...]
        m_ref[...] = m0_ref[...]

    causal = _lower_tri(L)
    tri = jnp.where(causal, 1.0, 0.0).astype(BF16)
    eye = jnp.where(lax.broadcasted_iota(jnp.int32, (V7X_LANES, V7X_LANES), 0)
                    == lax.broadcasted_iota(jnp.int32, (V7X_LANES, V7X_LANES), 1), 1.0, 0.0).astype(BF16)
    rows = lax.broadcasted_iota(jnp.int32, (L, V7X_LANES), 0)
    lanes = lax.broadcasted_iota(jnp.int32, (L, V7X_LANES), 1)
    gate_lanes = (lanes >= S_MI) & (lanes < S_END)

    heads = [(b, h) for b in range(bsz) for h in range(ML_HEADS)]
    cols = lambda h: slice(h * ML_DK, (h + 1) * ML_DK)

    ones = jnp.ones((V7X_LANES, V7X_LANES), BF16)
    qk, qc, qn = {}, {}, {}
    for b, h in heads:
        qh = q_ref[b, :, cols(h)]
        qk[b, h] = _dot_nt(qh, k_ref[b, :, cols(h)])
        qc[b, h] = _dot_nt(qh, c_ref[b, h].astype(BF16))
        qn[b, h] = _dot((qh.astype(F32) * n_ref[b, h:h + 1, :]).astype(BF16), ones)

    gates = {}
    for b in range(bsz):
        sm = jnp.where(gate_lanes, sm_ref[b], 0.0)
        t1, t2, t3 = _split3(sm)
        bcum = _dot(tri, t1) + (_dot(tri, t2) + _dot(tri, t3))
        bcum = pltpu.roll(bcum, V7X_LANES - (S_MF - S_MI), axis=1)
        g = sm - bcum
        gmax = g
        sh = 1
        while sh < L:
            gmax = jnp.maximum(gmax, jnp.where(rows >= sh, pltpu.roll(gmax, sh, axis=0), -jnp.inf))
            sh *= 2
        m = m_ref[pl.ds(b, 1), :]
        u = jnp.maximum(m, gmax)
        a = jnp.exp(m - u)
        mt = bcum + u
        m_ref[pl.ds(b, 1), :] = mt[L - 1:L, :]
        g1, g2, g3 = _split3(g)
        g_rows = _mxu_transpose(eye, g1) + (_mxu_transpose(eye, g2) + _mxu_transpose(eye, g3))
        gates[b] = dict(u=u, a=a, em=jnp.exp(-mt), a_last=a[L - 1:L, :], wcol=jnp.exp(g - u[L - 1:L, :]),
                        g_rows=g_rows)

    for b, h in heads:
        gl = S_MI + h
        kh = k_ref[b, :, cols(h)]
        w_h = gates[b]["wcol"][:, gl:gl + 1]
        a_l = gates[b]["a_last"][:, gl:gl + 1]
        vw = (v_ref[b, :, cols(h)].astype(F32) * w_h).astype(BF16)
        c_ref[b, h] = a_l * c_ref[b, h] + _dot(_mxu_transpose(eye, vw).astype(BF16), kh)
        n_ref[b, h:h + 1, :] = (a_l * n_ref[b, h:h + 1, :]
                                + jnp.sum(kh.astype(F32) * w_h, axis=0, keepdims=True))

    for b, h in heads:
        gl = S_MI + h
        gb = gates[b]
        decay = jnp.exp(jnp.where(causal, gb["g_rows"][gl:gl + 1, :] - gb["u"][:, gl:gl + 1], NEG))
        s = qk[b, h] * decay
        a_h = gb["a"][:, gl:gl + 1]
        sb = s.astype(BF16)
        num = a_h * qc[b, h] + _dot(sb, v_ref[b, :, cols(h)])
        den = a_h * qn[b, h] + _dot(sb, ones[:L, :])
        hv = num / jnp.maximum(jnp.abs(den), gb["em"][:, gl:gl + 1])
        hn = hv * lax.rsqrt(_dot((hv * hv).astype(BF16), ones) * (1.0 / ML_DV) + EPS)
        o_ref[b, :, cols(h)] = (hn * g_ref[:, cols(h)] * mo_ref[b, :, cols(h)].astype(F32)).astype(BF16)


def _mlstm(mq, mk, mv, mo, small, c0, n0, m0, g_ml, L):
    bsz, seq, _ = mq.shape
    chunk = pl.BlockSpec((bsz, L, ML_W), lambda c: (0, c, 0))
    full = lambda shape: pl.BlockSpec(shape, lambda c: (0,) * len(shape))
    return pl.pallas_call(
        _mlstm_kernel,
        grid=(seq // L,),
        in_specs=[chunk, chunk, chunk, chunk,
                  pl.BlockSpec((bsz, L, V7X_LANES), lambda c: (0, c, 0)),
                  full(c0.shape), full(n0.shape), full(m0.shape), full((1, ML_W))],
        out_specs=[chunk, full(c0.shape), full(n0.shape), full(m0.shape)],
        out_shape=[jax.ShapeDtypeStruct((bsz, seq, ML_W), BF16),
                   jax.ShapeDtypeStruct(c0.shape, F32),
                   jax.ShapeDtypeStruct(n0.shape, F32),
                   jax.ShapeDtypeStruct(m0.shape, F32)],
        compiler_params=_params("arbitrary"),
        name="mlstm",
    )(mq, mk, mv, mo, small, c0, n0, m0, g_ml)


def _merge_kernel(ca_ref, cm_ref, x_ref, wa_ref, wm_ref, g_ref, wrh_ref, wrl_ref, br_ref,
                  y1_ref, xn_ref, ridx_ref, rgate_ref, cnt_ref):
    tm = x_ref.shape[0]

    @pl.when(pl.program_id(0) == 0)
    def _():
        cnt_ref[...] = jnp.zeros_like(cnt_ref)

    y1 = x_ref[...] + (_dot(ca_ref[...], wa_ref[...]) + _dot(cm_ref[...], wm_ref[...]))
    y1_ref[...] = y1
    xn = (y1 * lax.rsqrt(jnp.mean(y1 * y1, axis=-1, keepdims=True) + EPS)) * g_ref[...]
    xn_ref[...] = xn.reshape(xn_ref.shape)

    xh = xn.astype(BF16)
    xl = (xn - xh.astype(F32)).astype(BF16)
    logits = (_dot(xh, wrh_ref[...]) + (_dot(xl, wrh_ref[...]) + _dot(xh, wrl_ref[...]))) + br_ref[...]

    lane = lax.broadcasted_iota(jnp.int32, logits.shape, 1)
    vals, sel = [], []
    work = logits
    for _ in range(TOP_K):
        mx = jnp.max(work, axis=-1, keepdims=True)
        idx = jnp.min(jnp.where(work == mx, lane, V7X_LANES), axis=-1, keepdims=True)
        vals.append(mx)
        sel.append(idx)
        work = jnp.where(lane == idx, -jnp.inf, work)
    ex = [jnp.exp(v - vals[0]) for v in vals]
    tot = ex[0] + ex[1] + ex[2] + ex[3]

    onehot = [(lane == idx) for idx in sel]
    picked = jnp.where(onehot[0] | onehot[1] | onehot[2] | onehot[3], 1.0, 0.0)
    earlier = (lax.broadcasted_iota(jnp.int32, (tm, tm), 1) < lax.broadcasted_iota(jnp.int32, (tm, tm), 0))
    before = _dot(jnp.where(earlier, 1.0, 0.0).astype(BF16), picked.astype(BF16)) + cnt_ref[...]
    cnt_ref[...] = cnt_ref[...] + jnp.sum(picked, axis=0, keepdims=True)

    ridx = jnp.zeros(logits.shape, jnp.int32)
    rgate = jnp.zeros(logits.shape, F32)
    for k in range(TOP_K):
        rank = jnp.sum(jnp.where(onehot[k], before, 0.0), axis=-1, keepdims=True).astype(jnp.int32)
        ridx = jnp.where(lane == k, sel[k], ridx)
        ridx = jnp.where(lane == TOP_K + k, rank, ridx)
        rgate = jnp.where(lane == k, ex[k] / tot, rgate)
    ridx_ref[...] = ridx
    rgate_ref[...] = rgate


def _merge(cat_a, cat_m, x, w_a, w_m, g_ffn, wr_hi, wr_lo, br, tm):
    t, d = x.shape
    row = lambda c: pl.BlockSpec((tm, c), lambda i: (i, 0))
    const = lambda shape: pl.BlockSpec(shape, lambda i: (0,) * len(shape))
    return pl.pallas_call(
        _merge_kernel,
        grid=(t // tm,),
        in_specs=[row(FOX_W), row(ML_W), row(d), const(w_a.shape), const(w_m.shape), const((1, d)),
                  const(wr_hi.shape), const(wr_lo.shape), const((1, V7X_LANES))],
        out_specs=[row(d), pl.BlockSpec((tm,) + _row_tile(d), lambda i: (i, 0, 0)), row(V7X_LANES),
                   row(V7X_LANES), const((1, V7X_LANES))],
        out_shape=[jax.ShapeDtypeStruct((t, d), F32), jax.ShapeDtypeStruct((t,) + _row_tile(d), F32),
                   jax.ShapeDtypeStruct((t, V7X_LANES), jnp.int32),
                   jax.ShapeDtypeStruct((t, V7X_LANES), F32),
                   jax.ShapeDtypeStruct((1, V7X_LANES), F32)],
        compiler_params=_params("arbitrary"),
        name="merge_router",
    )(cat_a, cat_m, x, w_a, w_m, g_ffn, wr_hi, wr_lo, br)


def _row_tile(d):
    return (d // V7X_LANES, V7X_LANES)


def _row_copy(src_ref, src_row, dst_ref, dst_row, sem):
    return pltpu.make_async_copy(src_ref.at[pl.ds(src_row, 1)], dst_ref.at[pl.ds(dst_row, 1)], sem)


def _sorted_row(base_ref, route_ref, r, k):
    at = ROUTE_W * r + k
    return base_ref[route_ref[0, 0, at]] + route_ref[0, 0, at + TOP_K]


def _scatter_rows(base_ref, route_ref, x_ref, xs_ref, sem):
    tm = x_ref.shape[0]

    def issue(g, _):
        for u in range(DMA_UNROLL):
            r = g * DMA_UNROLL + u
            for k in range(TOP_K):
                _row_copy(x_ref, r, xs_ref, _sorted_row(base_ref, route_ref, r, k), sem).start(priority=k % 2)
        return 0

    def drain(g, _):
        for _ in range(DMA_UNROLL * TOP_K):
            _row_copy(x_ref, 0, xs_ref, 0, sem).wait()
        return 0

    lax.fori_loop(0, tm // DMA_UNROLL, issue, 0)
    lax.fori_loop(0, tm // DMA_UNROLL, drain, 0)


def _dispatch_kernel(fs_ref, fl_ref, base_p_ref, base_s_ref, route_p_ref, x_p_ref, route_s_ref, x_s_ref, xs_ref,
                     zero_ref, sem, *, ntp, nts):
    i = pl.program_id(0)

    @pl.when(i < ntp)
    def _():
        _scatter_rows(base_p_ref, route_p_ref, x_p_ref, xs_ref, sem)

    @pl.when((i >= ntp) & (i < ntp + nts))
    def _():
        _scatter_rows(base_s_ref, route_s_ref, x_s_ref, xs_ref, sem)

    @pl.when(i == ntp + nts)
    def _():
        zero_ref[...] = jnp.zeros_like(zero_ref)
        full = zero_ref.shape[0]
        bits = [1 << p for p in range(full.bit_length() - 2, -1, -1)]

        def zeros_to(start, size):
            return pltpu.make_async_copy(zero_ref.at[pl.ds(0, size)], xs_ref.at[pl.ds(start, size)], sem)

        def per_run(e, _, wait):
            start, n = fs_ref[e], fl_ref[e]
            whole = n // full

            def whole_copy(q, _):
                c = zeros_to(start + q * full, full)
                c.wait() if wait else c.start()
                return 0

            lax.fori_loop(0, whole, whole_copy, 0)
            for p in bits:
                @pl.when((n & p) != 0)
                def _():
                    c = zeros_to(start + (n & ~(2 * p - 1)), p)
                    c.wait() if wait else c.start()
            return 0

        lax.fori_loop(0, fs_ref.shape[0], functools.partial(per_run, wait=False), 0)
        lax.fori_loop(0, fs_ref.shape[0], functools.partial(per_run, wait=True), 0)


def _dispatch(fill_start, fill_len, base_p, base_s, route_p, x_p, route_s, x_s, n_rows, tm_p, tm_s):
    tp, sub, lanes = x_p.shape
    ntp, nts = tp // tm_p, x_s.shape[0] // tm_s
    clamp_p = lambda i, *_: (jnp.minimum(i, ntp - 1), 0, 0)
    clamp_s = lambda i, *_: (jnp.clip(i - ntp, 0, nts - 1), 0, 0)
    grid_spec = pltpu.PrefetchScalarGridSpec(
        num_scalar_prefetch=4,
        grid=(ntp + nts + 1,),
        in_specs=[pl.BlockSpec((1, 1, ROUTE_W * tm_p), clamp_p, memory_space=pltpu.SMEM),
                  pl.BlockSpec((tm_p, sub, lanes), clamp_p),
                  pl.BlockSpec((1, 1, ROUTE_W * tm_s), clamp_s, memory_space=pltpu.SMEM),
                  pl.BlockSpec((tm_s, sub, lanes), clamp_s)],
        out_specs=pl.BlockSpec(memory_space=pl.ANY),
        scratch_shapes=[pltpu.VMEM((FFN_ROWS, sub, lanes), F32), pltpu.SemaphoreType.DMA(())],
    )
    return pl.pallas_call(
        functools.partial(_dispatch_kernel, ntp=ntp, nts=nts),
        grid_spec=grid_spec,
        out_shape=jax.ShapeDtypeStruct((n_rows, sub, lanes), F32),
        compiler_params=_params("arbitrary"),
        name="moe_dispatch",
    )(fill_start, fill_len, base_p, base_s, route_p, x_p, route_s, x_s)


def _ffn_kernel(te_ref, nu_ref, x_ref, wgu_ref, bgu_ref, wd_ref, bd_ref, y_ref, wgu16_ref, wd16_ref):
    i = pl.program_id(0)
    d_ff = wd_ref.shape[1]

    @pl.when((i == 0) | (te_ref[i] != te_ref[jnp.maximum(i - 1, 0)]))
    def _():
        wgu16_ref[...] = wgu_ref[0].astype(BF16)
        wd16_ref[...] = wd_ref[0].astype(BF16)

    @pl.when(i < nu_ref[0])
    def _():
        x = x_ref[...].reshape(x_ref.shape[0], -1)
        gu = _dot(x.astype(BF16), wgu16_ref[...]) + bgu_ref[0]
        gate = jnp.minimum(gu[:, :d_ff], SWIGLU_LIMIT)
        up = jnp.clip(gu[:, d_ff:], -SWIGLU_LIMIT, SWIGLU_LIMIT)
        h = (up + 1.0) * (gate * _sigmoid(SWIGLU_ALPHA * gate))
        y = _dot(h.astype(BF16), wd16_ref[...]) + bd_ref[0]
        y_ref[...] = y.reshape(y_ref.shape)

    @pl.when(i >= nu_ref[0])
    def _():
        y_ref[...] = jnp.zeros_like(y_ref)


def _ffn(tile_expert, n_used, xs, w_gu, b_gu, w_d, b_d, tm):
    r = xs.shape[0]
    d_ff, d = w_d.shape[1:]
    tiles = pl.BlockSpec((tm,) + _row_tile(d), lambda i, te, nu: (i, 0, 0))
    grid_spec = pltpu.PrefetchScalarGridSpec(
        num_scalar_prefetch=2,
        grid=(r // tm,),
        in_specs=[tiles,
                  pl.BlockSpec((1, d, 2 * d_ff), lambda i, te, nu: (te[i], 0, 0)),
                  pl.BlockSpec((1, 1, 2 * d_ff), lambda i, te, nu: (te[i], 0, 0)),
                  pl.BlockSpec((1, d_ff, d), lambda i, te, nu: (te[i], 0, 0)),
                  pl.BlockSpec((1, 1, d), lambda i, te, nu: (te[i], 0, 0))],
        out_specs=tiles,
        scratch_shapes=[pltpu.VMEM((d, 2 * d_ff), BF16), pltpu.VMEM((d_ff, d), BF16)],
    )
    return pl.pallas_call(
        _ffn_kernel,
        grid_spec=grid_spec,
        out_shape=jax.ShapeDtypeStruct(xs.shape, F32),
        compiler_params=_params("arbitrary"),
        name="moe_ffn",
    )(tile_expert, n_used, xs, w_gu, b_gu, w_d, b_d)


def _combine_kernel(base_ref, pos_ref, pos_next_ref, y1_ref, gate_ref, gfin_ref, ys_ref, o_ref, buf_ref, sem):
    tm = y1_ref.shape[0]
    i = pl.program_id(0)
    slot = i % 2

    def gather(p_ref, s):
        def issue(g, _):
            for u in range(DMA_UNROLL):
                r = g * DMA_UNROLL + u
                for k in range(TOP_K):
                    _row_copy(ys_ref, _sorted_row(base_ref, p_ref, r, k), buf_ref.at[s, k], r,
                              sem.at[s]).start(priority=k % 2)
            return 0

        lax.fori_loop(0, tm // DMA_UNROLL, issue, 0)

    @pl.when(i == 0)
    def _():
        gather(pos_ref, 0)

    @pl.when(i + 1 < pl.num_programs(0))
    def _():
        gather(pos_next_ref, 1 - slot)

    def drain(g, _):
        for _ in range(DMA_UNROLL * TOP_K):
            _row_copy(ys_ref, 0, buf_ref.at[slot, 0], 0, sem.at[slot]).wait()
        return 0

    lax.fori_loop(0, tm // DMA_UNROLL, drain, 0)

    gate = gate_ref[...]
    moe = gate[:, 0:1] * buf_ref[slot, 0].reshape(tm, -1)
    for k in range(1, TOP_K):
        moe = moe + gate[:, k:k + 1] * buf_ref[slot, k].reshape(tm, -1)
    y = y1_ref[...] + moe
    o_ref[...] = (y * lax.rsqrt(jnp.mean(y * y, axis=-1, keepdims=True) + EPS)) * gfin_ref[...]


def _combine(base, route, y1, rgate, g_final, ys, tm):
    t, d = y1.shape
    nt = t // tm
    route_spec = lambda f: pl.BlockSpec((1, 1, ROUTE_W * tm), f, memory_space=pltpu.SMEM)
    grid_spec = pltpu.PrefetchScalarGridSpec(
        num_scalar_prefetch=1,
        grid=(nt,),
        in_specs=[route_spec(lambda i, _: (i, 0, 0)), route_spec(lambda i, _: (jnp.minimum(i + 1, nt - 1), 0, 0)),
                  pl.BlockSpec((tm, d), lambda i, _: (i, 0)),
                  pl.BlockSpec((tm, V7X_LANES), lambda i, _: (i, 0)),
                  pl.BlockSpec((1, d), lambda i, _: (0, 0)),
                  pl.BlockSpec(memory_space=pl.ANY)],
        out_specs=pl.BlockSpec((tm, d), lambda i, _: (i, 0)),
        scratch_shapes=[pltpu.VMEM((2, TOP_K, tm) + _row_tile(d), F32), pltpu.SemaphoreType.DMA((2,))],
    )
    return pl.pallas_call(
        _combine_kernel,
        grid_spec=grid_spec,
        out_shape=jax.ShapeDtypeStruct((t, d), F32),
        compiler_params=_params("arbitrary"),
        name="moe_combine",
    )(base, route, route, y1, rgate, g_final, ys)


def _mixer(x, hist, fox_cache, ml_state, w, tm, tq, tk, chunk):
    bsz, seq, _ = x.shape
    hist8 = jnp.pad(hist, ((0, 0), (V7X_SUBLANES - (CONV_W - 1), 0), (0, 0)))
    (fq, fk32, fv32, fk16, fv16, small, flogf, mq, mk, mv, mo, conv_new) = _project(
        x, hist8, w["g_mix"], w["wbig"], w["wsm"], w["bsm"], w["conv_w"], w["conv_b"], tm)

    if fox_cache is None:
        q_offset = 0
        assert seq % tk == 0
        ka0, ka1, va0, va1 = _fox_prep(small, [fk16], [fv16], math.gcd(tk, FOX_BLOCK))
    else:
        ck_c, cv_c, clogf_c = fox_cache
        q_offset = ck_c.shape[1]
        assert q_offset + seq <= tk
        clogf_c = jnp.pad(clogf_c, ((0, 0), (0, 0), (S_FF, V7X_LANES - S_FF - FOX_HEADS)))
        lf_all = jnp.pad(jnp.concatenate([clogf_c, small], axis=1), ((0, 0), (0, tk - q_offset - seq), (0, 0)))
        ka0, ka1, va0, va1 = _fox_prep(lf_all, [ck_c, fk16], [cv_c, fv16], tk)
    cat_a = _fox(fq, ka0, ka1, va0, va1, w["g_fox"], q_offset, tq, tk)

    c0, n0, m0 = ml_state
    m0 = jnp.pad(m0, ((0, 0), (S_MI, V7X_LANES - S_MI - ML_HEADS)))
    cat_m, c_new, n_new, m_new = _mlstm(mq, mk, mv, mo, small, c0, n0, m0, w["g_ml"], chunk)
    m_new = m_new[:, S_MI:S_MI + ML_HEADS]

    states = (fk32.reshape(bsz, seq, FOX_HEADS, FOX_HD), fv32.reshape(bsz, seq, FOX_HEADS, FOX_HD), flogf,
              c_new, n_new, m_new, conv_new)
    return cat_a, cat_m, states


def kernel(x_prompt, x_sample, cache_fox_k, cache_fox_v, cache_fox_logf, state_mlstm_C, state_mlstm_n,
           state_mlstm_m, state_mlstm_conv, norm_mix_g, w_in, b_fox_f, conv_w, conv_b, b_ml_i, b_ml_f,
           g_fox, g_ml, w_out, norm_ffn_g, w_router, b_router, w_gate_up, b_gate_up, w_down, b_down,
           norm_final_g):
    depth = w_in.shape[0]
    assert depth == 1, "the final norm is fused into the last layer's combine; only depth 1 is wired up"
    bp, sp, d = x_prompt.shape
    bs, ss, _ = x_sample.shape
    n_exp = w_router.shape[-1]
    d_ff = w_down.shape[2]
    yp, ys = x_prompt, x_sample
    p_st, s_st = [], []

    for l in range(depth):
        wl = w_in[l]
        w = {
            "g_mix": norm_mix_g[l][None, :],
            "wbig": jnp.concatenate([wl[:, O_FQ:O_FF], wl[:, O_MQ:O_MI]], axis=1).astype(BF16),
            "wsm": jnp.pad(jnp.concatenate([wl[:, O_FF:O_MQ], wl[:, O_MI:P_IN]], axis=1),
                           ((0, 0), (0, V7X_LANES - S_END))).astype(BF16),
            "bsm": jnp.pad(jnp.concatenate([b_fox_f[l], b_ml_i[l], b_ml_f[l]]), (0, V7X_LANES - S_END))[None, :],
            "conv_w": conv_w[l],
            "conv_b": conv_b[l][None, :],
            "g_fox": g_fox[l][None, :],
            "g_ml": g_ml[l][None, :],
        }
        w_a = w_out[l][:FOX_W].astype(BF16)
        w_m = w_out[l][FOX_W:].astype(BF16)
        g_ffn = norm_ffn_g[l][None, :]
        wr = jnp.pad(w_router[l], ((0, 0), (0, V7X_LANES - n_exp)))
        wr_hi = wr.astype(BF16)
        wr_lo = (wr - wr_hi.astype(F32)).astype(BF16)
        br = jnp.pad(b_router[l], (0, V7X_LANES - n_exp), constant_values=NEG)[None, :]

        zeros_state = (jnp.zeros((bp, ML_HEADS, ML_DV, ML_DK), F32), jnp.zeros((bp, ML_HEADS, ML_DK), F32),
                       jnp.zeros((bp, ML_HEADS), F32))
        cat_a_p, cat_m_p, st_p = _mixer(yp, jnp.zeros((bp, CONV_W - 1, QK_W), F32), None, zeros_state, w,
                                        min(PROJ_ROWS, sp), min(FOX_QBLOCK, sp), min(FOX_BLOCK, sp), CHUNK)
        p_st.append(st_p)
        past = cache_fox_k.shape[2]
        tk_s = -(-(past + ss) // V7X_LANES) * V7X_LANES
        cat_a_s, cat_m_s, st_s = _mixer(
            ys, state_mlstm_conv[l], (cache_fox_k[l], cache_fox_v[l], cache_fox_logf[l]),
            (state_mlstm_C[l].astype(F32), state_mlstm_n[l].astype(F32), state_mlstm_m[l].astype(F32)),
            w, ss, ss, tk_s, ss)
        s_st.append(st_s)

        tp, ts = bp * sp, bs * ss
        y1_p, xn_p, ridx_p, rgate_p, cnt_p = _merge(
            cat_a_p.reshape(tp, FOX_W), cat_m_p.reshape(tp, ML_W), yp.reshape(tp, d), w_a, w_m, g_ffn,
            wr_hi, wr_lo, br, min(MERGE_ROWS, tp))
        y1_s, xn_s, ridx_s, rgate_s, cnt_s = _merge(
            cat_a_s.reshape(ts, FOX_W), cat_m_s.reshape(ts, ML_W), ys.reshape(ts, d), w_a, w_m, g_ffn,
            wr_hi, wr_lo, br, min(MERGE_ROWS, ts))

        cnt_p = cnt_p[0, :n_exp].astype(jnp.int32)
        cnt_s = cnt_s[0, :n_exp].astype(jnp.int32)
        seg_rows = -(-(cnt_p + cnt_s) // FFN_ROWS) * FFN_ROWS
        seg_end = jnp.cumsum(seg_rows)
        seg_start = seg_end - seg_rows
        n_rows = -(-((tp + ts) * TOP_K + n_exp * (FFN_ROWS - 1)) // FFN_ROWS) * FFN_ROWS
        n_tiles = n_rows // FFN_ROWS
        tile_row0 = jnp.arange(n_tiles, dtype=jnp.int32) * FFN_ROWS
        tile_expert = jnp.minimum(jnp.sum((seg_end[None, :] <= tile_row0[:, None]).astype(jnp.int32), axis=1),
                                  n_exp - 1)
        n_used = (seg_end[-1:] // FFN_ROWS).astype(jnp.int32)
        tm_p, tm_s = min(MOE_ROWS, tp), min(MOE_ROWS, ts)
        base_p = seg_start.astype(jnp.int32)
        base_s = (seg_start + cnt_p).astype(jnp.int32)
        route_p = ridx_p[:, :ROUTE_W].reshape(tp // tm_p, 1, ROUTE_W * tm_p)
        route_s = ridx_s[:, :ROUTE_W].reshape(ts // tm_s, 1, ROUTE_W * tm_s)

        cnt = cnt_p + cnt_s
        fill_start = jnp.concatenate([seg_start + cnt, seg_end[-1:]]).astype(jnp.int32)
        fill_len = jnp.concatenate([seg_rows - cnt, n_rows - seg_end[-1:]]).astype(jnp.int32)
        xs = _dispatch(fill_start, fill_len, base_p, base_s, route_p, xn_p, route_s, xn_s, n_rows, tm_p, tm_s)
        ysort = _ffn(tile_expert, n_used, xs, w_gate_up[l], b_gate_up[l].reshape(n_exp, 1, 2 * d_ff),
                     w_down[l], b_down[l].reshape(n_exp, 1, d), FFN_ROWS)

        g_fin = norm_final_g[None, :]
        yp = _combine(base_p, route_p, y1_p, rgate_p, g_fin, ysort, tm_p).reshape(bp, sp, d)
        ys = _combine(base_s, route_s, y1_s, rgate_s, g_fin, ysort, tm_s).reshape(bs, ss, d)

    p_out = tuple(jnp.stack(a) for a in zip(*p_st))
    s_out = tuple(jnp.stack(a) for a in zip(*s_st))
    return (yp, ys) + p_out + s_out
```

```python
import functools
import math

import numpy as np

import jax
import jax.numpy as jnp
from jax import lax
from jax.experimental import pallas as pl
from jax.experimental.pallas import tpu as pltpu

F32 = jnp.float32
BF16 = jnp.bfloat16

FOX_HEADS = 8
FOX_HD = 64
FOX_W = FOX_HEADS * FOX_HD
ML_HEADS = 4
ML_DK = 128
ML_DV = 128
ML_W = ML_HEADS * ML_DV
QK_W = 2 * ML_HEADS * ML_DK
CONV_W = 4
CHUNK = 64
TOP_K = 4
SWIGLU_LIMIT = 7.0
SWIGLU_ALPHA = 1.702
EPS = 1e-6
NEG = -1e30
LOG2E = math.log2(math.e)

O_FQ = 0
O_FK = O_FQ + FOX_W
O_FV = O_FK + FOX_W
O_FF = O_FV + FOX_W
O_MQ = O_FF + FOX_HEADS
O_MK = O_MQ + ML_HEADS * ML_DK
O_MV = O_MK + ML_HEADS * ML_DK
O_MO = O_MV + ML_W
O_MI = O_MO + ML_W
O_MF = O_MI + ML_HEADS
P_IN = O_MF + ML_HEADS

B_FQ, B_FK, B_FV, B_QK, B_MV, B_MO, B_END = 0, 512, 1024, 1536, 2560, 3072, 3584
S_FF, S_MI, S_MF, S_END = 0, 8, 12, 16
BIAS_TERMS = 3

V7X_LANES = 128
V7X_SUBLANES = 8
V7X_VMEM_LIMIT_BYTES = 56 * 1024 * 1024

PROJ_ROWS = 512
FOX_BLOCK = 512
FOX_QBLOCK = 1024
FOX_SUB = 512
FOX_UNROLLS = (4, 2, 1)
MERGE_ROWS = 512
MOE_ROWS = 512
FFN_ROWS = 512
DMA_UNROLL = 8


def _params(*semantics):
    return pltpu.CompilerParams(dimension_semantics=semantics, vmem_limit_bytes=V7X_VMEM_LIMIT_BYTES)


def _dot(a, b):
    return jnp.dot(a, b, preferred_element_type=F32)


def _dot_nt(a, b):
    return lax.dot_general(a, b, (((1,), (1,)), ((), ())), preferred_element_type=F32)


def _mxu_transpose(eye, x):
    return _dot_nt(eye, x)


def _sigmoid(x):
    return 1.0 / (1.0 + jnp.exp(-x))


def _log_sigmoid(x):
    return jnp.minimum(x, 0.0) - jnp.log1p(jnp.exp(-jnp.abs(x)))


def _split3(x):
    t1 = x.astype(BF16)
    r1 = x - t1.astype(F32)
    t2 = r1.astype(BF16)
    t3 = (r1 - t2.astype(F32)).astype(BF16)
    return t1, t2, t3


def _lower_tri(n):
    r = lax.broadcasted_iota(jnp.int32, (n, n), 0)
    c = lax.broadcasted_iota(jnp.int32, (n, n), 1)
    return c <= r


def _proj_kernel(x_ref, g_ref, wbig_ref, wsm_ref, bsm_ref, cw_ref, cb_ref, hist_ref,
                 fq_ref, fk32_ref, fv32_ref, fk16_ref, fv16_ref, sm_ref, flogf_ref, mq_ref, mk_ref, mv_ref,
                 mo_ref, cnew_ref, halo_ref):
    tm = x_ref.shape[1]

    @pl.when(pl.program_id(1) == 0)
    def _():
        halo_ref[...] = hist_ref[0]

    x = x_ref[0]
    xn = (x * lax.rsqrt(jnp.mean(x * x, axis=-1, keepdims=True) + EPS)) * g_ref[...]
    xb = xn.astype(BF16)
    z = _dot(xb, wbig_ref[...])

    fq_ref[0] = (z[:, B_FQ:B_FK] * (FOX_HD ** -0.5 * LOG2E)).astype(BF16)
    fk = z[:, B_FK:B_FV]
    fv = z[:, B_FV:B_QK]
    fk32_ref[0] = fk
    fv32_ref[0] = fv
    fk16_ref[0] = fk.astype(BF16)
    fv16_ref[0] = fv.astype(BF16)
    mv_ref[0] = z[:, B_MV:B_MO].astype(BF16)
    mo_ref[0] = _sigmoid(z[:, B_MO:B_END]).astype(BF16)

    zs = _dot(xb, wsm_ref[...]) + bsm_ref[...]
    lane = lax.broadcasted_iota(jnp.int32, zs.shape, 1)
    is_forget = (lane < S_MI) | ((lane >= S_MF) & (lane < S_END))
    sm = jnp.where(is_forget, _log_sigmoid(zs), zs)
    sm_ref[0] = sm
    flogf_ref[0] = sm[:, S_FF:S_MI]

    u = z[:, B_QK:B_MV]
    up = jnp.concatenate([halo_ref[...], u], axis=0)
    n = tm + V7X_SUBLANES
    first = V7X_SUBLANES - (CONV_W - 1)
    y = cb_ref[...] + cw_ref[0:1, :] * pltpu.roll(up, n - first, axis=0)[:tm]
    for j in range(1, CONV_W - 1):
        y = y + cw_ref[j:j + 1, :] * pltpu.roll(up, n - (first + j), axis=0)[:tm]
    y = y + cw_ref[CONV_W - 1:CONV_W, :] * u
    qk = y * _sigmoid(y)
    mq_ref[0] = qk[:, :QK_W // 2].astype(BF16)
    mk_ref[0] = (qk[:, QK_W // 2:] * (ML_DK ** -0.5)).astype(BF16)

    halo_ref[...] = u[tm - V7X_SUBLANES:, :]
    cnew_ref[0] = halo_ref[first:, :]


def _project(x, hist8, g, wbig, wsm, bsm, cw, cb, tm):
    bsz, seq, d = x.shape
    grid = (bsz, seq // tm)
    row = lambda c: pl.BlockSpec((1, tm, c), lambda b, s: (b, s, 0))
    const = lambda shape: pl.BlockSpec(shape, lambda b, s: (0,) * len(shape))
    outs = [
        (FOX_W, BF16), (FOX_W, F32), (FOX_W, F32), (FOX_W, BF16), (FOX_W, BF16), (V7X_LANES, F32),
        (FOX_HEADS, F32), (ML_W, BF16), (ML_W, BF16), (ML_W, BF16), (ML_W, BF16),
    ]
    out_shape = [jax.ShapeDtypeStruct((bsz, seq, c), dt) for c, dt in outs]
    out_specs = [row(c) for c, _ in outs]
    out_shape.append(jax.ShapeDtypeStruct((bsz, CONV_W - 1, QK_W), F32))
    out_specs.append(pl.BlockSpec((1, CONV_W - 1, QK_W), lambda b, s: (b, 0, 0)))
    return pl.pallas_call(
        _proj_kernel,
        grid=grid,
        in_specs=[row(d), const((1, d)), const(wbig.shape), const(wsm.shape), const((1, V7X_LANES)),
                  const((CONV_W, QK_W)), const((1, QK_W)),
                  pl.BlockSpec((1, V7X_SUBLANES, QK_W), lambda b, s: (b, 0, 0))],
        out_specs=out_specs,
        out_shape=out_shape,
        scratch_shapes=[pltpu.VMEM((V7X_SUBLANES, QK_W), F32)],
        compiler_params=_params("arbitrary", "arbitrary"),
        name="proj",
    )(x, g, wbig, wsm, bsm, cw, cb, hist8)


def _bias_lane(head, term):
    return (head // 2) * V7X_LANES + (FOX_HD if head % 2 == 0 else 0) + term


def _bias_placement():
    place = np.zeros((BIAS_TERMS, V7X_LANES, FOX_W), np.float32)
    for h in range(FOX_HEADS):
        for t in range(BIAS_TERMS):
            place[t, S_FF + h, _bias_lane(h, t)] = 1.0
    return jnp.asarray(place, BF16)


def _fox_prep_kernel(lf_ref, *refs, n_parts):
    k_refs, v_refs = refs[:n_parts], refs[n_parts:2 * n_parts]
    place_ref, ka0_ref, ka1_ref, va0_ref, va1_ref, carry_ref = refs[2 * n_parts:]
    tm = lf_ref.shape[1]

    def rows(part_refs):
        parts = [r[0].astype(F32) for r in part_refs]
        missing = tm - sum(p.shape[0] for p in parts)
        if missing:
            parts.append(jnp.zeros((missing, FOX_W), F32))
        return parts[0] if len(parts) == 1 else jnp.concatenate(parts, axis=0)

    @pl.when(pl.program_id(1) == 0)
    def _():
        carry_ref[...] = jnp.zeros_like(carry_ref)

    tri = jnp.where(_lower_tri(tm), 1.0, 0.0).astype(BF16)
    t1, t2, t3 = _split3(lf_ref[0])
    c = carry_ref[...] + (_dot(tri, t1) + (_dot(tri, t2) + _dot(tri, t3)))
    carry_ref[...] = c[tm - 1:tm, :]
    b1, b2, b3 = _split3(c * (-LOG2E))
    kb = _dot(b1, place_ref[0]) + (_dot(b2, place_ref[1]) + _dot(b3, place_ref[2]))

    k = rows(k_refs)
    v = rows(v_refs)
    first_head = lax.broadcasted_iota(jnp.int32, k.shape, 1) % V7X_LANES < FOX_HD
    ka0_ref[0] = jnp.where(first_head, k, kb).astype(BF16)
    ka1_ref[0] = jnp.where(first_head, kb, k).astype(BF16)
    va0_ref[0] = jnp.where(first_head, v, 1.0).astype(BF16)
    va1_ref[0] = jnp.where(first_head, 1.0, v).astype(BF16)


def _fox_prep(lf, k_parts, v_parts, tm):
    bsz, lk, _ = lf.shape
    place = _bias_placement()
    wide = pl.BlockSpec((1, tm, FOX_W), lambda b, s: (b, s, 0))
    if len(k_parts) == 1:
        part_specs = [wide, wide]
    else:
        assert lk == tm, "several key/value parts are only stitched inside a single tile"
        part_specs = [pl.BlockSpec((1,) + p.shape[1:], lambda b, s: (b, 0, 0)) for p in k_parts + v_parts]
    sds = jax.ShapeDtypeStruct((bsz, lk, FOX_W), BF16)
    return pl.pallas_call(
        functools.partial(_fox_prep_kernel, n_parts=len(k_parts)),
        grid=(bsz, lk // tm),
        in_specs=[pl.BlockSpec((1, tm, V7X_LANES), lambda b, s: (b, s, 0))] + part_specs
        + [pl.BlockSpec(place.shape, lambda b, s: (0, 0, 0))],
        out_specs=[wide, wide, wide, wide],
        out_shape=[sds, sds, sds, sds],
        scratch_shapes=[pltpu.VMEM((1, V7X_LANES), F32)],
        compiler_params=_params("arbitrary", "arbitrary"),
        name="fox_prep",
    )(lf, *k_parts, *v_parts, place)


def _fox_kernel(q_ref, ka0_ref, ka1_ref, va0_ref, va1_ref, g_ref, o_ref, *, tq, tk, sub, q_offset):
    ts = min(tq, sub)
    n_sub = tq // ts
    diag_blocks = max(ts // tk, 1)
    k_refs = (ka0_ref, ka1_ref)
    v_refs = (va0_ref, va1_ref)
    lane = lax.broadcasted_iota(jnp.int32, (ts, V7X_LANES), 1)
    first_head = lane < FOX_HD
    q_lo = q_offset + pl.program_id(2) * tq
    n_common = (q_lo + 1) // tk
    single_block = ka0_ref.shape[1] == tk
    even_common = q_offset % (2 * tk) == 0 and tq % (2 * tk) == 0

    def chain_q(r, hh):
        q = q_ref[0, r * ts:(r + 1) * ts, :].astype(F32)
        if hh == 0:
            return jnp.where(first_head, q, jnp.where(lane < FOX_HD + BIAS_TERMS, 1.0, 0.0)).astype(BF16)
        return jnp.where(first_head, jnp.where(lane < BIAS_TERMS, 1.0, 0.0), q).astype(BF16)

    chains = [(r, hh) for r in range(n_sub) for hh in range(2)]
    qs = [chain_q(r, hh) for r, hh in chains]

    def scores(c, j):
        return _dot_nt(qs[c], k_refs[chains[c][1]][0, pl.ds(pl.multiple_of(j * tk, tk), tk), :])

    def update(state, c, j, s, masked):
        r, hh = chains[c]
        m, acc = state
        start = pl.multiple_of(j * tk, tk)
        if masked:
            qpos = q_lo + r * ts + lax.broadcasted_iota(jnp.int32, s.shape, 0)
            kpos = start + lax.broadcasted_iota(jnp.int32, s.shape, 1)
            s = jnp.where(kpos <= qpos, s, NEG)
        m_new = jnp.maximum(m, jnp.max(s, axis=-1, keepdims=True))
        p = jnp.exp2(s - m_new).astype(BF16)
        acc = jnp.exp2(m - m_new) * acc + _dot(p, v_refs[hh][0, pl.ds(start, tk), :])
        return m_new, acc

    def step(state, c, j, masked):
        return update(state, c, j, scores(c, j), masked)

    def common(j, states):
        ss = [scores(c, j) for c in range(len(chains))]
        return tuple(update(st, c, j, ss[c], False) for c, st in enumerate(states))

    def common_run(t, states, nblk):
        states = list(states)
        cur = [scores(c, nblk * t) for c in range(len(chains))]
        for e in range(nblk):
            nxt = []
            for c in range(len(chains)):
                states[c] = update(states[c], c, nblk * t + e, cur[c], False)
                if e + 1 < nblk:
                    nxt.append(scores(c, nblk * t + e + 1))
            cur = nxt
        return tuple(states)

    init = (jnp.full((ts, 1), NEG, F32), jnp.zeros((ts, V7X_LANES), F32))
    states = (init,) * len(chains)
    if not single_block:
        done = 0
        for nblk in FOX_UNROLLS:
            if nblk == 1 and even_common:
                continue
            trips = n_common // nblk
            states = lax.fori_loop(done // nblk, trips, functools.partial(common_run, nblk=nblk), states)
            done = trips * nblk
    states = list(states)
    for c, (r, _) in enumerate(chains):
        for e in range(r * diag_blocks):
            states[c] = step(states[c], c, n_common + e, False)
        for e in range(diag_blocks):
            states[c] = step(states[c], c, n_common + r * diag_blocks + e, True)

    for r in range(n_sub):
        out = jnp.zeros((ts, V7X_LANES), F32)
        for hh in range(2):
            acc = states[chains.index((r, hh))][1]
            own = first_head if hh == 0 else jnp.logical_not(first_head)
            denom_lane = FOX_HD if hh == 0 else 0
            o = jnp.where(own, acc / acc[:, denom_lane:denom_lane + 1], 0.0)
            out = out + o * lax.rsqrt(jnp.sum(o * o, axis=-1, keepdims=True) * (1.0 / FOX_HD) + EPS)
        o_ref[0, r * ts:(r + 1) * ts, :] = (out * g_ref[...]).astype(BF16)


def _fox(q, ka0, ka1, va0, va1, g_fox, q_offset, tq, tk):
    bsz, lq, _ = q.shape
    lk = ka0.shape[1]
    assert (q_offset % tk == 0 and tq % tk == 0) or (lq == tq <= tk and lk == tk), (q_offset, tq, tk, lq, lk)
    pairs = FOX_HEADS // 2
    grid = (bsz, pairs, lq // tq)
    kv_spec = pl.BlockSpec((1, lk, V7X_LANES), lambda b, p, i: (b, 0, p))
    q_spec = pl.BlockSpec((1, tq, V7X_LANES), lambda b, p, i: (b, i, p))
    return pl.pallas_call(
        functools.partial(_fox_kernel, tq=tq, tk=tk, sub=FOX_SUB, q_offset=q_offset),
        grid=grid,
        in_specs=[q_spec, kv_spec, kv_spec, kv_spec, kv_spec,
                  pl.BlockSpec((1, V7X_LANES), lambda b, p, i: (0, p))],
        out_specs=q_spec,
        out_shape=jax.ShapeDtypeStruct((bsz, lq, FOX_W), BF16),
        compiler_params=_params("arbitrary", "arbitrary", "arbitrary"),
        name="fox_attention",
    )(q, ka0, ka1, va0, va1, g_fox)


def _mlstm_kernel(q_ref, k_ref, v_ref, mo_ref, sm_ref, c0_ref, n0_ref, m0_ref, g_ref,
                  o_ref, c_ref, n_ref, m_ref):
    bsz, L, _ = q_ref.shape

    @pl.when(pl.program_id(0) == 0)
    def _():
        c_ref[...] = c0_ref[...]
        n_ref[...] = n0_ref[...]
        m_ref[...] = m0_ref[...]

    causal = _lower_tri(L)
    tri = jnp.where(causal, 1.0, 0.0).astype(BF16)
    eye = jnp.where(lax.broadcasted_iota(jnp.int32, (V7X_LANES, V7X_LANES), 0)
                    == lax.broadcasted_iota(jnp.int32, (V7X_LANES, V7X_LANES), 1), 1.0, 0.0).astype(BF16)
    rows = lax.broadcasted_iota(jnp.int32, (L, V7X_LANES), 0)
    lanes = lax.broadcasted_iota(jnp.int32, (L, V7X_LANES), 1)
    gate_lanes = (lanes >= S_MI) & (lanes < S_END)

    heads = [(b, h) for b in range(bsz) for h in range(ML_HEADS)]
    cols = lambda h: slice(h * ML_DK, (h + 1) * ML_DK)

    ones = jnp.ones((V7X_LANES, V7X_LANES), BF16)
    qk, qc, qn = {}, {}, {}
    for b, h in heads:
        qh = q_ref[b, :, cols(h)]
        qk[b, h] = _dot_nt(qh, k_ref[b, :, cols(h)])
        qc[b, h] = _dot_nt(qh, c_ref[b, h].astype(BF16))
        qn[b, h] = _dot((qh.astype(F32) * n_ref[b, h:h + 1, :]).astype(BF16), ones)

    gates = {}
    for b in range(bsz):
        sm = jnp.where(gate_lanes, sm_ref[b], 0.0)
        t1, t2, t3 = _split3(sm)
        bcum = _dot(tri, t1) + (_dot(tri, t2) + _dot(tri, t3))
        bcum = pltpu.roll(bcum, V7X_LANES - (S_MF - S_MI), axis=1)
        g = sm - bcum
        gmax = g
        sh = 1
        while sh < L:
            gmax = jnp.maximum(gmax, jnp.where(rows >= sh, pltpu.roll(gmax, sh, axis=0), -jnp.inf))
            sh *= 2
        m = m_ref[pl.ds(b, 1), :]
        u = jnp.maximum(m, gmax)
        a = jnp.exp(m - u)
        mt = bcum + u
        m_ref[pl.ds(b, 1), :] = mt[L - 1:L, :]
        g1, g2, g3 = _split3(g)
        g_rows = _mxu_transpose(eye, g1) + (_mxu_transpose(eye, g2) + _mxu_transpose(eye, g3))
        gates[b] = dict(u=u, a=a, em=jnp.exp(-mt), a_last=a[L - 1:L, :], wcol=jnp.exp(g - u[L - 1:L, :]),
                        g_rows=g_rows)

    for b, h in heads:
        gl = S_MI + h
        kh = k_ref[b, :, cols(h)]
        w_h = gates[b]["wcol"][:, gl:gl + 1]
        a_l = gates[b]["a_last"][:, gl:gl + 1]
        vw = (v_ref[b, :, cols(h)].astype(F32) * w_h).astype(BF16)
        c_ref[b, h] = a_l * c_ref[b, h] + _dot(_mxu_transpose(eye, vw).astype(BF16), kh)
        n_ref[b, h:h + 1, :] = (a_l * n_ref[b, h:h + 1, :]
                                + jnp.sum(kh.astype(F32) * w_h, axis=0, keepdims=True))

    for b, h in heads:
        gl = S_MI + h
        gb = gates[b]
        decay = jnp.exp(jnp.where(causal, gb["g_rows"][gl:gl + 1, :] - gb["u"][:, gl:gl + 1], NEG))
        s = qk[b, h] * decay
        a_h = gb["a"][:, gl:gl + 1]
        sb = s.astype(BF16)
        num = a_h * qc[b, h] + _dot(sb, v_ref[b, :, cols(h)])
        den = a_h * qn[b, h] + _dot(sb, ones[:L, :])
        hv = num / jnp.maximum(jnp.abs(den), gb["em"][:, gl:gl + 1])
        hn = hv * lax.rsqrt(_dot((hv * hv).astype(BF16), ones) * (1.0 / ML_DV) + EPS)
        o_ref[b, :, cols(h)] = (hn * g_ref[:, cols(h)] * mo_ref[b, :, cols(h)].astype(F32)).astype(BF16)


def _mlstm(mq, mk, mv, mo, small, c0, n0, m0, g_ml, L):
    bsz, seq, _ = mq.shape
    chunk = pl.BlockSpec((bsz, L, ML_W), lambda c: (0, c, 0))
    full = lambda shape: pl.BlockSpec(shape, lambda c: (0,) * len(shape))
    return pl.pallas_call(
        _mlstm_kernel,
        grid=(seq // L,),
        in_specs=[chunk, chunk, chunk, chunk,
                  pl.BlockSpec((bsz, L, V7X_LANES), lambda c: (0, c, 0)),
                  full(c0.shape), full(n0.shape), full(m0.shape), full((1, ML_W))],
        out_specs=[chunk, full(c0.shape), full(n0.shape), full(m0.shape)],
        out_shape=[jax.ShapeDtypeStruct((bsz, seq, ML_W), BF16),
                   jax.ShapeDtypeStruct(c0.shape, F32),
                   jax.ShapeDtypeStruct(n0.shape, F32),
                   jax.ShapeDtypeStruct(m0.shape, F32)],
        compiler_params=_params("arbitrary"),
        name="mlstm",
    )(mq, mk, mv, mo, small, c0, n0, m0, g_ml)


def _merge_kernel(ca_ref, cm_ref, x_ref, wa_ref, wm_ref, g_ref, wrh_ref, wrl_ref, br_ref,
                  y1_ref, xn_ref, ridx_ref, rgate_ref, cnt_ref):
    tm = x_ref.shape[0]

    @pl.when(pl.program_id(0) == 0)
    def _():
        cnt_ref[...] = jnp.zeros_like(cnt_ref)

    y1 = x_ref[...] + (_dot(ca_ref[...], wa_ref[...]) + _dot(cm_ref[...], wm_ref[...]))
    y1_ref[...] = y1
    xn = (y1 * lax.rsqrt(jnp.mean(y1 * y1, axis=-1, keepdims=True) + EPS)) * g_ref[...]
    xn_ref[...] = xn.reshape(xn_ref.shape)

    xh = xn.astype(BF16)
    xl = (xn - xh.astype(F32)).astype(BF16)
    logits = (_dot(xh, wrh_ref[...]) + (_dot(xl, wrh_ref[...]) + _dot(xh, wrl_ref[...]))) + br_ref[...]

    lane = lax.broadcasted_iota(jnp.int32, logits.shape, 1)
    vals, sel = [], []
    work = logits
    for _ in range(TOP_K):
        mx = jnp.max(work, axis=-1, keepdims=True)
        idx = jnp.min(jnp.where(work == mx, lane, V7X_LANES), axis=-1, keepdims=True)
        vals.append(mx)
        sel.append(idx)
        work = jnp.where(lane == idx, -jnp.inf, work)
    ex = [jnp.exp(v - vals[0]) for v in vals]
    tot = ex[0] + ex[1] + ex[2] + ex[3]

    onehot = [(lane == idx) for idx in sel]
    picked = jnp.where(onehot[0] | onehot[1] | onehot[2] | onehot[3], 1.0, 0.0)
    earlier = (lax.broadcasted_iota(jnp.int32, (tm, tm), 1) < lax.broadcasted_iota(jnp.int32, (tm, tm), 0))
    before = _dot(jnp.where(earlier, 1.0, 0.0).astype(BF16), picked.astype(BF16)) + cnt_ref[...]
    cnt_ref[...] = cnt_ref[...] + jnp.sum(picked, axis=0, keepdims=True)

    ridx = jnp.zeros(logits.shape, jnp.int32)
    rgate = jnp.zeros(logits.shape, F32)
    for k in range(TOP_K):
        rank = jnp.sum(jnp.where(onehot[k], before, 0.0), axis=-1, keepdims=True).astype(jnp.int32)
        ridx = jnp.where(lane == k, sel[k], ridx)
        ridx = jnp.where(lane == TOP_K + k, rank, ridx)
        rgate = jnp.where(lane == k, ex[k] / tot, rgate)
    ridx_ref[...] = ridx
    rgate_ref[...] = rgate


def _merge(cat_a, cat_m, x, w_a, w_m, g_ffn, wr_hi, wr_lo, br, tm):
    t, d = x.shape
    row = lambda c: pl.BlockSpec((tm, c), lambda i: (i, 0))
    const = lambda shape: pl.BlockSpec(shape, lambda i: (0,) * len(shape))
    return pl.pallas_call(
        _merge_kernel,
        grid=(t // tm,),
        in_specs=[row(FOX_W), row(ML_W), row(d), const(w_a.shape), const(w_m.shape), const((1, d)),
                  const(wr_hi.shape), const(wr_lo.shape), const((1, V7X_LANES))],
        out_specs=[row(d), pl.BlockSpec((tm,) + _row_tile(d), lambda i: (i, 0, 0)), row(V7X_LANES),
                   row(V7X_LANES), const((1, V7X_LANES))],
        out_shape=[jax.ShapeDtypeStruct((t, d), F32), jax.ShapeDtypeStruct((t,) + _row_tile(d), F32),
                   jax.ShapeDtypeStruct((t, V7X_LANES), jnp.int32),
                   jax.ShapeDtypeStruct((t, V7X_LANES), F32),
                   jax.ShapeDtypeStruct((1, V7X_LANES), F32)],
        compiler_params=_params("arbitrary"),
        name="merge_router",
    )(cat_a, cat_m, x, w_a, w_m, g_ffn, wr_hi, wr_lo, br)


def _row_tile(d):
    return (d // V7X_LANES, V7X_LANES)


def _row_copy(src_ref, src_row, dst_ref, dst_row, sem):
    return pltpu.make_async_copy(src_ref.at[pl.ds(src_row, 1)], dst_ref.at[pl.ds(dst_row, 1)], sem)


def _scatter_rows(pos_ref, x_ref, xs_ref, sem):
    tm = x_ref.shape[0]

    def issue(g, _):
        for u in range(DMA_UNROLL):
            r = g * DMA_UNROLL + u
            for k in range(TOP_K):
                _row_copy(x_ref, r, xs_ref, pos_ref[0, 0, TOP_K * r + k], sem).start(priority=k % 2)
        return 0

    def drain(g, _):
        for _ in range(DMA_UNROLL * TOP_K):
            _row_copy(x_ref, 0, xs_ref, 0, sem).wait()
        return 0

    lax.fori_loop(0, tm // DMA_UNROLL, issue, 0)
    lax.fori_loop(0, tm // DMA_UNROLL, drain, 0)


def _dispatch_kernel(fs_ref, fl_ref, pos_p_ref, x_p_ref, pos_s_ref, x_s_ref, xs_ref, zero_ref, sem, *, ntp, nts):
    i = pl.program_id(0)

    @pl.when(i < ntp)
    def _():
        _scatter_rows(pos_p_ref, x_p_ref, xs_ref, sem)

    @pl.when((i >= ntp) & (i < ntp + nts))
    def _():
        _scatter_rows(pos_s_ref, x_s_ref, xs_ref, sem)

    @pl.when(i == ntp + nts)
    def _():
        zero_ref[...] = jnp.zeros_like(zero_ref)
        full = zero_ref.shape[0]
        bits = [1 << p for p in range(full.bit_length() - 2, -1, -1)]

        def zeros_to(start, size):
            return pltpu.make_async_copy(zero_ref.at[pl.ds(0, size)], xs_ref.at[pl.ds(start, size)], sem)

        def per_run(e, _, wait):
            start, n = fs_ref[e], fl_ref[e]
            whole = n // full

            def whole_copy(q, _):
                c = zeros_to(start + q * full, full)
                c.wait() if wait else c.start()
                return 0

            lax.fori_loop(0, whole, whole_copy, 0)
            for p in bits:
                @pl.when((n & p) != 0)
                def _():
                    c = zeros_to(start + (n & ~(2 * p - 1)), p)
                    c.wait() if wait else c.start()
            return 0

        lax.fori_loop(0, fs_ref.shape[0], functools.partial(per_run, wait=False), 0)
        lax.fori_loop(0, fs_ref.shape[0], functools.partial(per_run, wait=True), 0)


def _dispatch(fill_start, fill_len, pos3_p, x_p, pos3_s, x_s, n_rows, tm_p, tm_s):
    tp, sub, lanes = x_p.shape
    ntp, nts = tp // tm_p, x_s.shape[0] // tm_s
    clamp_p = lambda i, fs, fl: (jnp.minimum(i, ntp - 1), 0, 0)
    clamp_s = lambda i, fs, fl: (jnp.clip(i - ntp, 0, nts - 1), 0, 0)
    grid_spec = pltpu.PrefetchScalarGridSpec(
        num_scalar_prefetch=2,
        grid=(ntp + nts + 1,),
        in_specs=[pl.BlockSpec((1, 1, TOP_K * tm_p), clamp_p, memory_space=pltpu.SMEM),
                  pl.BlockSpec((tm_p, sub, lanes), clamp_p),
                  pl.BlockSpec((1, 1, TOP_K * tm_s), clamp_s, memory_space=pltpu.SMEM),
                  pl.BlockSpec((tm_s, sub, lanes), clamp_s)],
        out_specs=pl.BlockSpec(memory_space=pl.ANY),
        scratch_shapes=[pltpu.VMEM((FFN_ROWS, sub, lanes), F32), pltpu.SemaphoreType.DMA(())],
    )
    return pl.pallas_call(
        functools.partial(_dispatch_kernel, ntp=ntp, nts=nts),
        grid_spec=grid_spec,
        out_shape=jax.ShapeDtypeStruct((n_rows, sub, lanes), F32),
        compiler_params=_params("arbitrary"),
        name="moe_dispatch",
    )(fill_start, fill_len, pos3_p, x_p, pos3_s, x_s)


def _ffn_kernel(te_ref, nu_ref, x_ref, wgu_ref, bgu_ref, wd_ref, bd_ref, y_ref, wgu16_ref, wd16_ref):
    i = pl.program_id(0)
    d_ff = wd_ref.shape[1]

    @pl.when((i == 0) | (te_ref[i] != te_ref[jnp.maximum(i - 1, 0)]))
    def _():
        wgu16_ref[...] = wgu_ref[0].astype(BF16)
        wd16_ref[...] = wd_ref[0].astype(BF16)

    @pl.when(i < nu_ref[0])
    def _():
        x = x_ref[...].reshape(x_ref.shape[0], -1)
        gu = _dot(x.astype(BF16), wgu16_ref[...]) + bgu_ref[0]
        gate = jnp.minimum(gu[:, :d_ff], SWIGLU_LIMIT)
        up = jnp.clip(gu[:, d_ff:], -SWIGLU_LIMIT, SWIGLU_LIMIT)
        h = (up + 1.0) * (gate * _sigmoid(SWIGLU_ALPHA * gate))
        y = _dot(h.astype(BF16), wd16_ref[...]) + bd_ref[0]
        y_ref[...] = y.reshape(y_ref.shape)

    @pl.when(i >= nu_ref[0])
    def _():
        y_ref[...] = jnp.zeros_like(y_ref)


def _ffn(tile_expert, n_used, xs, w_gu, b_gu, w_d, b_d, tm):
    r = xs.shape[0]
    d_ff, d = w_d.shape[1:]
    tiles = pl.BlockSpec((tm,) + _row_tile(d), lambda i, te, nu: (i, 0, 0))
    grid_spec = pltpu.PrefetchScalarGridSpec(
        num_scalar_prefetch=2,
        grid=(r // tm,),
        in_specs=[tiles,
                  pl.BlockSpec((1, d, 2 * d_ff), lambda i, te, nu: (te[i], 0, 0)),
                  pl.BlockSpec((1, 1, 2 * d_ff), lambda i, te, nu: (te[i], 0, 0)),
                  pl.BlockSpec((1, d_ff, d), lambda i, te, nu: (te[i], 0, 0)),
                  pl.BlockSpec((1, 1, d), lambda i, te, nu: (te[i], 0, 0))],
        out_specs=tiles,
        scratch_shapes=[pltpu.VMEM((d, 2 * d_ff), BF16), pltpu.VMEM((d_ff, d), BF16)],
    )
    return pl.pallas_call(
        _ffn_kernel,
        grid_spec=grid_spec,
        out_shape=jax.ShapeDtypeStruct(xs.shape, F32),
        compiler_params=_params("arbitrary"),
        name="moe_ffn",
    )(tile_expert, n_used, xs, w_gu, b_gu, w_d, b_d)


def _combine_kernel(pos_ref, pos_next_ref, y1_ref, gate_ref, gfin_ref, ys_ref, o_ref, buf_ref, sem):
    tm = y1_ref.shape[0]
    i = pl.program_id(0)
    slot = i % 2

    def gather(p_ref, s):
        def issue(g, _):
            for u in range(DMA_UNROLL):
                r = g * DMA_UNROLL + u
                for k in range(TOP_K):
                    _row_copy(ys_ref, p_ref[0, 0, TOP_K * r + k], buf_ref.at[s, k], r,
                              sem.at[s]).start(priority=k % 2)
            return 0

        lax.fori_loop(0, tm // DMA_UNROLL, issue, 0)

    @pl.when(i == 0)
    def _():
        gather(pos_ref, 0)

    @pl.when(i + 1 < pl.num_programs(0))
    def _():
        gather(pos_next_ref, 1 - slot)

    def drain(g, _):
        for _ in range(DMA_UNROLL * TOP_K):
            _row_copy(ys_ref, 0, buf_ref.at[slot, 0], 0, sem.at[slot]).wait()
        return 0

    lax.fori_loop(0, tm // DMA_UNROLL, drain, 0)

    gate = gate_ref[...]
    moe = gate[:, 0:1] * buf_ref[slot, 0].reshape(tm, -1)
    for k in range(1, TOP_K):
        moe = moe + gate[:, k:k + 1] * buf_ref[slot, k].reshape(tm, -1)
    y = y1_ref[...] + moe
    o_ref[...] = (y * lax.rsqrt(jnp.mean(y * y, axis=-1, keepdims=True) + EPS)) * gfin_ref[...]


def _combine(pos3, y1, rgate, g_final, ys, tm):
    t, d = y1.shape
    nt = t // tm
    pos_spec = lambda f: pl.BlockSpec((1, 1, TOP_K * tm), f, memory_space=pltpu.SMEM)
    return pl.pallas_call(
        _combine_kernel,
        grid=(nt,),
        in_specs=[pos_spec(lambda i: (i, 0, 0)), pos_spec(lambda i: (jnp.minimum(i + 1, nt - 1), 0, 0)),
                  pl.BlockSpec((tm, d), lambda i: (i, 0)),
                  pl.BlockSpec((tm, V7X_LANES), lambda i: (i, 0)),
                  pl.BlockSpec((1, d), lambda i: (0, 0)),
                  pl.BlockSpec(memory_space=pl.ANY)],
        out_specs=pl.BlockSpec((tm, d), lambda i: (i, 0)),
        out_shape=jax.ShapeDtypeStruct((t, d), F32),
        scratch_shapes=[pltpu.VMEM((2, TOP_K, tm) + _row_tile(d), F32), pltpu.SemaphoreType.DMA((2,))],
        compiler_params=_params("arbitrary"),
        name="moe_combine",
    )(pos3, pos3, y1, rgate, g_final, ys)


def _mixer(x, hist, fox_cache, ml_state, w, tm, tq, tk, chunk):
    bsz, seq, _ = x.shape
    hist8 = jnp.pad(hist, ((0, 0), (V7X_SUBLANES - (CONV_W - 1), 0), (0, 0)))
    (fq, fk32, fv32, fk16, fv16, small, flogf, mq, mk, mv, mo, conv_new) = _project(
        x, hist8, w["g_mix"], w["wbig"], w["wsm"], w["bsm"], w["conv_w"], w["conv_b"], tm)

    if fox_cache is None:
        q_offset = 0
        assert seq % tk == 0
        ka0, ka1, va0, va1 = _fox_prep(small, [fk16], [fv16], math.gcd(tk, FOX_BLOCK))
    else:
        ck_c, cv_c, clogf_c = fox_cache
        q_offset = ck_c.shape[1]
        assert q_offset + seq <= tk
        clogf_c = jnp.pad(clogf_c, ((0, 0), (0, 0), (S_FF, V7X_LANES - S_FF - FOX_HEADS)))
        lf_all = jnp.pad(jnp.concatenate([clogf_c, small], axis=1), ((0, 0), (0, tk - q_offset - seq), (0, 0)))
        ka0, ka1, va0, va1 = _fox_prep(lf_all, [ck_c.reshape(bsz, q_offset, FOX_W), fk16],
                                       [cv_c.reshape(bsz, q_offset, FOX_W), fv16], tk)
    cat_a = _fox(fq, ka0, ka1, va0, va1, w["g_fox"], q_offset, tq, tk)

    c0, n0, m0 = ml_state
    m0 = jnp.pad(m0, ((0, 0), (S_MI, V7X_LANES - S_MI - ML_HEADS)))
    cat_m, c_new, n_new, m_new = _mlstm(mq, mk, mv, mo, small, c0, n0, m0, w["g_ml"], chunk)
    m_new = m_new[:, S_MI:S_MI + ML_HEADS]

    states = (fk32.reshape(bsz, seq, FOX_HEADS, FOX_HD), fv32.reshape(bsz, seq, FOX_HEADS, FOX_HD), flogf,
              c_new, n_new, m_new, conv_new)
    return cat_a, cat_m, states


def kernel(x_prompt, x_sample, cache_fox_k, cache_fox_v, cache_fox_logf, state_mlstm_C, state_mlstm_n,
           state_mlstm_m, state_mlstm_conv, norm_mix_g, w_in, b_fox_f, conv_w, conv_b, b_ml_i, b_ml_f,
           g_fox, g_ml, w_out, norm_ffn_g, w_router, b_router, w_gate_up, b_gate_up, w_down, b_down,
           norm_final_g):
    depth = w_in.shape[0]
    assert depth == 1, "the final norm is fused into the last layer's combine; only depth 1 is wired up"
    bp, sp, d = x_prompt.shape
    bs, ss, _ = x_sample.shape
    n_exp = w_router.shape[-1]
    d_ff = w_down.shape[2]
    yp, ys = x_prompt, x_sample
    p_st, s_st = [], []

    for l in range(depth):
        wl = w_in[l]
        w = {
            "g_mix": norm_mix_g[l][None, :],
            "wbig": jnp.concatenate([wl[:, O_FQ:O_FF], wl[:, O_MQ:O_MI]], axis=1).astype(BF16),
            "wsm": jnp.pad(jnp.concatenate([wl[:, O_FF:O_MQ], wl[:, O_MI:P_IN]], axis=1),
                           ((0, 0), (0, V7X_LANES - S_END))).astype(BF16),
            "bsm": jnp.pad(jnp.concatenate([b_fox_f[l], b_ml_i[l], b_ml_f[l]]), (0, V7X_LANES - S_END))[None, :],
            "conv_w": conv_w[l],
            "conv_b": conv_b[l][None, :],
            "g_fox": g_fox[l][None, :],
            "g_ml": g_ml[l][None, :],
        }
        w_a = w_out[l][:FOX_W].astype(BF16)
        w_m = w_out[l][FOX_W:].astype(BF16)
        g_ffn = norm_ffn_g[l][None, :]
        wr = jnp.pad(w_router[l], ((0, 0), (0, V7X_LANES - n_exp)))
        wr_hi = wr.astype(BF16)
        wr_lo = (wr - wr_hi.astype(F32)).astype(BF16)
        br = jnp.pad(b_router[l], (0, V7X_LANES - n_exp), constant_values=NEG)[None, :]

        zeros_state = (jnp.zeros((bp, ML_HEADS, ML_DV, ML_DK), F32), jnp.zeros((bp, ML_HEADS, ML_DK), F32),
                       jnp.zeros((bp, ML_HEADS), F32))
        cat_a_p, cat_m_p, st_p = _mixer(yp, jnp.zeros((bp, CONV_W - 1, QK_W), F32), None, zeros_state, w,
                                        min(PROJ_ROWS, sp), min(FOX_QBLOCK, sp), min(FOX_BLOCK, sp), CHUNK)
        p_st.append(st_p)
        past = cache_fox_k.shape[2]
        tk_s = -(-(past + ss) // V7X_LANES) * V7X_LANES
        cat_a_s, cat_m_s, st_s = _mixer(
            ys, state_mlstm_conv[l], (cache_fox_k[l], cache_fox_v[l], cache_fox_logf[l]),
            (state_mlstm_C[l].astype(F32), state_mlstm_n[l].astype(F32), state_mlstm_m[l].astype(F32)),
            w, ss, ss, tk_s, ss)
        s_st.append(st_s)

        tp, ts = bp * sp, bs * ss
        y1_p, xn_p, ridx_p, rgate_p, cnt_p = _merge(
            cat_a_p.reshape(tp, FOX_W), cat_m_p.reshape(tp, ML_W), yp.reshape(tp, d), w_a, w_m, g_ffn,
            wr_hi, wr_lo, br, min(MERGE_ROWS, tp))
        y1_s, xn_s, ridx_s, rgate_s, cnt_s = _merge(
            cat_a_s.reshape(ts, FOX_W), cat_m_s.reshape(ts, ML_W), ys.reshape(ts, d), w_a, w_m, g_ffn,
            wr_hi, wr_lo, br, min(MERGE_ROWS, ts))

        cnt_p = cnt_p[0, :n_exp].astype(jnp.int32)
        cnt_s = cnt_s[0, :n_exp].astype(jnp.int32)
        seg_rows = -(-(cnt_p + cnt_s) // FFN_ROWS) * FFN_ROWS
        seg_end = jnp.cumsum(seg_rows)
        seg_start = seg_end - seg_rows
        n_rows = -(-((tp + ts) * TOP_K + n_exp * (FFN_ROWS - 1)) // FFN_ROWS) * FFN_ROWS
        n_tiles = n_rows // FFN_ROWS
        tile_row0 = jnp.arange(n_tiles, dtype=jnp.int32) * FFN_ROWS
        tile_expert = jnp.minimum(jnp.sum((seg_end[None, :] <= tile_row0[:, None]).astype(jnp.int32), axis=1),
                                  n_exp - 1)
        n_used = (seg_end[-1:] // FFN_ROWS).astype(jnp.int32)
        e_p, rank_p = ridx_p[:, :TOP_K], ridx_p[:, TOP_K:2 * TOP_K]
        e_s, rank_s = ridx_s[:, :TOP_K], ridx_s[:, TOP_K:2 * TOP_K]
        pos_p = seg_start[e_p] + rank_p
        pos_s = seg_start[e_s] + cnt_p[e_s] + rank_s
        tm_p, tm_s = min(MOE_ROWS, tp), min(MOE_ROWS, ts)
        pos3_p = pos_p.reshape(tp // tm_p, 1, TOP_K * tm_p)
        pos3_s = pos_s.reshape(ts // tm_s, 1, TOP_K * tm_s)

        cnt = cnt_p + cnt_s
        fill_start = jnp.concatenate([seg_start + cnt, seg_end[-1:]]).astype(jnp.int32)
        fill_len = jnp.concatenate([seg_rows - cnt, n_rows - seg_end[-1:]]).astype(jnp.int32)
        xs = _dispatch(fill_start, fill_len, pos3_p, xn_p, pos3_s, xn_s, n_rows, tm_p, tm_s)
        ysort = _ffn(tile_expert, n_used, xs, w_gate_up[l], b_gate_up[l].reshape(n_exp, 1, 2 * d_ff),
                     w_down[l], b_down[l].reshape(n_exp, 1, d), FFN_ROWS)

        g_fin = norm_final_g[None, :]
        yp = _combine(pos3_p, y1_p, rgate_p, g_fin, ysort, tm_p).reshape(bp, sp, d)
        ys = _combine(pos3_s, y1_s, rgate_s, g_fin, ysort, tm_s).reshape(bs, ss, d)

    p_out = tuple(jnp.stack(a) for a in zip(*p_st))
    s_out = tuple(jnp.stack(a) for a in zip(*s_st))
    return (yp, ys) + p_out + s_out
```

```python
import functools
import math

import numpy as np

import jax
import jax.numpy as jnp
from jax import lax
from jax.experimental import pallas as pl
from jax.experimental.pallas import tpu as pltpu

F32 = jnp.float32
BF16 = jnp.bfloat16

FOX_HEADS = 8
FOX_HD = 64
FOX_W = FOX_HEADS * FOX_HD
ML_HEADS = 4
ML_DK = 128
ML_DV = 128
ML_W = ML_HEADS * ML_DV
QK_W = 2 * ML_HEADS * ML_DK
CONV_W = 4
CHUNK = 64
TOP_K = 4
SWIGLU_LIMIT = 7.0
SWIGLU_ALPHA = 1.702
EPS = 1e-6
NEG = -1e30
LOG2E = math.log2(math.e)

O_FQ = 0
O_FK = O_FQ + FOX_W
O_FV = O_FK + FOX_W
O_FF = O_FV + FOX_W
O_MQ = O_FF + FOX_HEADS
O_MK = O_MQ + ML_HEADS * ML_DK
O_MV = O_MK + ML_HEADS * ML_DK
O_MO = O_MV + ML_W
O_MI = O_MO + ML_W
O_MF = O_MI + ML_HEADS
P_IN = O_MF + ML_HEADS

B_FQ, B_FK, B_FV, B_QK, B_MV, B_MO, B_END = 0, 512, 1024, 1536, 2560, 3072, 3584
S_FF, S_MI, S_MF, S_END = 0, 8, 12, 16
BIAS_TERMS = 3

V7X_LANES = 128
V7X_SUBLANES = 8
V7X_VMEM_LIMIT_BYTES = 56 * 1024 * 1024

PROJ_ROWS = 512
FOX_BLOCK = 512
FOX_QBLOCK = 1024
FOX_SUB = 512
FOX_UNROLLS = (4, 2, 1)
MERGE_ROWS = 512
MOE_ROWS = 512
FFN_ROWS = 512
DMA_UNROLL = 8


def _params(*semantics):
    return pltpu.CompilerParams(dimension_semantics=semantics, vmem_limit_bytes=V7X_VMEM_LIMIT_BYTES)


def _dot(a, b):
    return jnp.dot(a, b, preferred_element_type=F32)


def _dot_nt(a, b):
    return lax.dot_general(a, b, (((1,), (1,)), ((), ())), preferred_element_type=F32)


def _mxu_transpose(eye, x):
    return _dot_nt(eye, x)


def _sigmoid(x):
    return 1.0 / (1.0 + jnp.exp(-x))


def _log_sigmoid(x):
    return jnp.minimum(x, 0.0) - jnp.log1p(jnp.exp(-jnp.abs(x)))


def _split3(x):
    t1 = x.astype(BF16)
    r1 = x - t1.astype(F32)
    t2 = r1.astype(BF16)
    t3 = (r1 - t2.astype(F32)).astype(BF16)
    return t1, t2, t3


def _lower_tri(n):
    r = lax.broadcasted_iota(jnp.int32, (n, n), 0)
    c = lax.broadcasted_iota(jnp.int32, (n, n), 1)
    return c <= r


def _proj_kernel(*refs, aug):
    x_ref, g_ref, wbig_ref, wsm_ref, bsm_ref, cw_ref, cb_ref, hist_ref = refs[:8]
    rest = refs[8:]
    if aug:
        place_ref, rest = rest[0], rest[1:]
        (fq_ref, fk32_ref, fv32_ref, ka0_ref, ka1_ref, va0_ref, va1_ref, sm_ref, flogf_ref, mq_ref, mk_ref,
         mv_ref, mo_ref, cnew_ref, halo_ref, carry_ref) = rest
    else:
        (fq_ref, fk32_ref, fv32_ref, fk16_ref, fv16_ref, sm_ref, flogf_ref, mq_ref, mk_ref, mv_ref, mo_ref,
         cnew_ref, halo_ref) = rest
    tm = x_ref.shape[1]

    @pl.when(pl.program_id(1) == 0)
    def _():
        halo_ref[...] = hist_ref[0]
        if aug:
            carry_ref[...] = jnp.zeros_like(carry_ref)

    x = x_ref[0]
    xn = (x * lax.rsqrt(jnp.mean(x * x, axis=-1, keepdims=True) + EPS)) * g_ref[...]
    xb = xn.astype(BF16)
    z = _dot(xb, wbig_ref[...])

    fq_ref[0] = (z[:, B_FQ:B_FK] * (FOX_HD ** -0.5 * LOG2E)).astype(BF16)
    fk = z[:, B_FK:B_FV]
    fv = z[:, B_FV:B_QK]
    fk32_ref[0] = fk
    fv32_ref[0] = fv
    mv_ref[0] = z[:, B_MV:B_MO].astype(BF16)
    mo_ref[0] = _sigmoid(z[:, B_MO:B_END]).astype(BF16)

    zs = _dot(xb, wsm_ref[...]) + bsm_ref[...]
    lane = lax.broadcasted_iota(jnp.int32, zs.shape, 1)
    is_forget = (lane < S_MI) | ((lane >= S_MF) & (lane < S_END))
    sm = jnp.where(is_forget, _log_sigmoid(zs), zs)
    sm_ref[0] = sm
    flogf_ref[0] = sm[:, S_FF:S_MI]

    if aug:
        rows = lax.broadcasted_iota(jnp.int32, sm.shape, 0)
        c = sm
        sh = 1
        while sh < tm:
            c = c + jnp.where(rows >= sh, pltpu.roll(c, sh, axis=0), 0.0)
            sh *= 2
        c = c + carry_ref[...]
        carry_ref[...] = c[tm - 1:tm, :]
        b1, b2, b3 = _split3(c * (-LOG2E))
        kb = _dot(b1, place_ref[0]) + (_dot(b2, place_ref[1]) + _dot(b3, place_ref[2]))
        first_head = lax.broadcasted_iota(jnp.int32, fk.shape, 1) % V7X_LANES < FOX_HD
        ka0_ref[0] = jnp.where(first_head, fk, kb).astype(BF16)
        ka1_ref[0] = jnp.where(first_head, kb, fk).astype(BF16)
        va0_ref[0] = jnp.where(first_head, fv, 1.0).astype(BF16)
        va1_ref[0] = jnp.where(first_head, 1.0, fv).astype(BF16)
    else:
        fk16_ref[0] = fk.astype(BF16)
        fv16_ref[0] = fv.astype(BF16)

    u = z[:, B_QK:B_MV]
    up = jnp.concatenate([halo_ref[...], u], axis=0)
    n = tm + V7X_SUBLANES
    first = V7X_SUBLANES - (CONV_W - 1)
    y = cb_ref[...] + cw_ref[0:1, :] * pltpu.roll(up, n - first, axis=0)[:tm]
    for j in range(1, CONV_W - 1):
        y = y + cw_ref[j:j + 1, :] * pltpu.roll(up, n - (first + j), axis=0)[:tm]
    y = y + cw_ref[CONV_W - 1:CONV_W, :] * u
    qk = y * _sigmoid(y)
    mq_ref[0] = qk[:, :QK_W // 2].astype(BF16)
    mk_ref[0] = (qk[:, QK_W // 2:] * (ML_DK ** -0.5)).astype(BF16)

    halo_ref[...] = u[tm - V7X_SUBLANES:, :]
    cnew_ref[0] = halo_ref[first:, :]


def _project(x, hist8, g, wbig, wsm, bsm, cw, cb, tm, aug):
    bsz, seq, d = x.shape
    grid = (bsz, seq // tm)
    row = lambda c: pl.BlockSpec((1, tm, c), lambda b, s: (b, s, 0))
    const = lambda shape: pl.BlockSpec(shape, lambda b, s: (0,) * len(shape))
    n_kv = 4 if aug else 2
    outs = ([(FOX_W, BF16), (FOX_W, F32), (FOX_W, F32)] + [(FOX_W, BF16)] * n_kv
            + [(V7X_LANES, F32), (FOX_HEADS, F32), (ML_W, BF16), (ML_W, BF16), (ML_W, BF16), (ML_W, BF16)])
    out_shape = [jax.ShapeDtypeStruct((bsz, seq, c), dt) for c, dt in outs]
    out_specs = [row(c) for c, _ in outs]
    out_shape.append(jax.ShapeDtypeStruct((bsz, CONV_W - 1, QK_W), F32))
    out_specs.append(pl.BlockSpec((1, CONV_W - 1, QK_W), lambda b, s: (b, 0, 0)))
    in_specs = [row(d), const((1, d)), const(wbig.shape), const(wsm.shape), const((1, V7X_LANES)),
                const((CONV_W, QK_W)), const((1, QK_W)),
                pl.BlockSpec((1, V7X_SUBLANES, QK_W), lambda b, s: (b, 0, 0))]
    args = [x, g, wbig, wsm, bsm, cw, cb, hist8]
    scratch = [pltpu.VMEM((V7X_SUBLANES, QK_W), F32)]
    if aug:
        place = _bias_placement()
        in_specs.append(const(place.shape))
        args.append(place)
        scratch.append(pltpu.VMEM((1, V7X_LANES), F32))
    return pl.pallas_call(
        functools.partial(_proj_kernel, aug=aug),
        grid=grid,
        in_specs=in_specs,
        out_specs=out_specs,
        out_shape=out_shape,
        scratch_shapes=scratch,
        compiler_params=_params("arbitrary", "arbitrary"),
        name="proj",
    )(*args)


def _bias_lane(head, term):
    return (head // 2) * V7X_LANES + (FOX_HD if head % 2 == 0 else 0) + term


def _bias_placement():
    place = np.zeros((BIAS_TERMS, V7X_LANES, FOX_W), np.float32)
    for h in range(FOX_HEADS):
        for t in range(BIAS_TERMS):
            place[t, S_FF + h, _bias_lane(h, t)] = 1.0
    return jnp.asarray(place, BF16)


def _fox_prep_kernel(lf_ref, *refs, n_parts):
    k_refs, v_refs = refs[:n_parts], refs[n_parts:2 * n_parts]
    place_ref, ka0_ref, ka1_ref, va0_ref, va1_ref, carry_ref = refs[2 * n_parts:]
    tm = lf_ref.shape[1]

    def rows(part_refs):
        parts = [r[0].astype(F32) for r in part_refs]
        missing = tm - sum(p.shape[0] for p in parts)
        if missing:
            parts.append(jnp.zeros((missing, FOX_W), F32))
        return parts[0] if len(parts) == 1 else jnp.concatenate(parts, axis=0)

    @pl.when(pl.program_id(1) == 0)
    def _():
        carry_ref[...] = jnp.zeros_like(carry_ref)

    tri = jnp.where(_lower_tri(tm), 1.0, 0.0).astype(BF16)
    t1, t2, t3 = _split3(lf_ref[0])
    c = carry_ref[...] + (_dot(tri, t1) + (_dot(tri, t2) + _dot(tri, t3)))
    carry_ref[...] = c[tm - 1:tm, :]
    b1, b2, b3 = _split3(c * (-LOG2E))
    kb = _dot(b1, place_ref[0]) + (_dot(b2, place_ref[1]) + _dot(b3, place_ref[2]))

    k = rows(k_refs)
    v = rows(v_refs)
    first_head = lax.broadcasted_iota(jnp.int32, k.shape, 1) % V7X_LANES < FOX_HD
    ka0_ref[0] = jnp.where(first_head, k, kb).astype(BF16)
    ka1_ref[0] = jnp.where(first_head, kb, k).astype(BF16)
    va0_ref[0] = jnp.where(first_head, v, 1.0).astype(BF16)
    va1_ref[0] = jnp.where(first_head, 1.0, v).astype(BF16)


def _fox_prep(lf, k_parts, v_parts, tm):
    bsz, lk, _ = lf.shape
    place = _bias_placement()
    wide = pl.BlockSpec((1, tm, FOX_W), lambda b, s: (b, s, 0))
    if len(k_parts) == 1:
        part_specs = [wide, wide]
    else:
        assert lk == tm, "several key/value parts are only stitched inside a single tile"
        part_specs = [pl.BlockSpec((1,) + p.shape[1:], lambda b, s: (b, 0, 0)) for p in k_parts + v_parts]
    sds = jax.ShapeDtypeStruct((bsz, lk, FOX_W), BF16)
    return pl.pallas_call(
        functools.partial(_fox_prep_kernel, n_parts=len(k_parts)),
        grid=(bsz, lk // tm),
        in_specs=[pl.BlockSpec((1, tm, V7X_LANES), lambda b, s: (b, s, 0))] + part_specs
        + [pl.BlockSpec(place.shape, lambda b, s: (0, 0, 0))],
        out_specs=[wide, wide, wide, wide],
        out_shape=[sds, sds, sds, sds],
        scratch_shapes=[pltpu.VMEM((1, V7X_LANES), F32)],
        compiler_params=_params("arbitrary", "arbitrary"),
        name="fox_prep",
    )(lf, *k_parts, *v_parts, place)


def _fox_kernel(q_ref, ka0_ref, ka1_ref, va0_ref, va1_ref, g_ref, o_ref, *, tq, tk, sub, q_offset):
    ts = min(tq, sub)
    n_sub = tq // ts
    diag_blocks = max(ts // tk, 1)
    k_refs = (ka0_ref, ka1_ref)
    v_refs = (va0_ref, va1_ref)
    lane = lax.broadcasted_iota(jnp.int32, (ts, V7X_LANES), 1)
    first_head = lane < FOX_HD
    q_lo = q_offset + pl.program_id(2) * tq
    n_common = (q_lo + 1) // tk
    single_block = ka0_ref.shape[1] == tk
    even_common = q_offset % (2 * tk) == 0 and tq % (2 * tk) == 0

    def chain_q(r, hh):
        q = q_ref[0, r * ts:(r + 1) * ts, :].astype(F32)
        if hh == 0:
            return jnp.where(first_head, q, jnp.where(lane < FOX_HD + BIAS_TERMS, 1.0, 0.0)).astype(BF16)
        return jnp.where(first_head, jnp.where(lane < BIAS_TERMS, 1.0, 0.0), q).astype(BF16)

    chains = [(r, hh) for r in range(n_sub) for hh in range(2)]
    qs = [chain_q(r, hh) for r, hh in chains]

    def scores(c, j):
        return _dot_nt(qs[c], k_refs[chains[c][1]][0, pl.ds(pl.multiple_of(j * tk, tk), tk), :])

    def update(state, c, j, s, masked):
        r, hh = chains[c]
        m, acc = state
        start = pl.multiple_of(j * tk, tk)
        if masked:
            qpos = q_lo + r * ts + lax.broadcasted_iota(jnp.int32, s.shape, 0)
            kpos = start + lax.broadcasted_iota(jnp.int32, s.shape, 1)
            s = jnp.where(kpos <= qpos, s, NEG)
        m_new = jnp.maximum(m, jnp.max(s, axis=-1, keepdims=True))
        p = jnp.exp2(s - m_new).astype(BF16)
        acc = jnp.exp2(m - m_new) * acc + _dot(p, v_refs[hh][0, pl.ds(start, tk), :])
        return m_new, acc

    def step(state, c, j, masked):
        return update(state, c, j, scores(c, j), masked)

    def common(j, states):
        ss = [scores(c, j) for c in range(len(chains))]
        return tuple(update(st, c, j, ss[c], False) for c, st in enumerate(states))

    def common_run(t, states, nblk):
        states = list(states)
        cur = [scores(c, nblk * t) for c in range(len(chains))]
        for e in range(nblk):
            nxt = []
            for c in range(len(chains)):
                states[c] = update(states[c], c, nblk * t + e, cur[c], False)
                if e + 1 < nblk:
                    nxt.append(scores(c, nblk * t + e + 1))
            cur = nxt
        return tuple(states)

    init = (jnp.full((ts, 1), NEG, F32), jnp.zeros((ts, V7X_LANES), F32))
    states = (init,) * len(chains)
    if not single_block:
        done = 0
        for nblk in FOX_UNROLLS:
            if nblk == 1 and even_common:
                continue
            trips = n_common // nblk
            states = lax.fori_loop(done // nblk, trips, functools.partial(common_run, nblk=nblk), states)
            done = trips * nblk
    states = list(states)
    for c, (r, _) in enumerate(chains):
        for e in range(r * diag_blocks):
            states[c] = step(states[c], c, n_common + e, False)
        for e in range(diag_blocks):
            states[c] = step(states[c], c, n_common + r * diag_blocks + e, True)

    for r in range(n_sub):
        out = jnp.zeros((ts, V7X_LANES), F32)
        for hh in range(2):
            acc = states[chains.index((r, hh))][1]
            own = first_head if hh == 0 else jnp.logical_not(first_head)
            denom_lane = FOX_HD if hh == 0 else 0
            o = jnp.where(own, acc / acc[:, denom_lane:denom_lane + 1], 0.0)
            out = out + o * lax.rsqrt(jnp.sum(o * o, axis=-1, keepdims=True) * (1.0 / FOX_HD) + EPS)
        o_ref[0, r * ts:(r + 1) * ts, :] = (out * g_ref[...]).astype(BF16)


def _fox(q, ka0, ka1, va0, va1, g_fox, q_offset, tq, tk):
    bsz, lq, _ = q.shape
    lk = ka0.shape[1]
    assert (q_offset % tk == 0 and tq % tk == 0) or (lq == tq <= tk and lk == tk), (q_offset, tq, tk, lq, lk)
    pairs = FOX_HEADS // 2
    grid = (bsz, pairs, lq // tq)
    kv_spec = pl.BlockSpec((1, lk, V7X_LANES), lambda b, p, i: (b, 0, p))
    q_spec = pl.BlockSpec((1, tq, V7X_LANES), lambda b, p, i: (b, i, p))
    return pl.pallas_call(
        functools.partial(_fox_kernel, tq=tq, tk=tk, sub=FOX_SUB, q_offset=q_offset),
        grid=grid,
        in_specs=[q_spec, kv_spec, kv_spec, kv_spec, kv_spec,
                  pl.BlockSpec((1, V7X_LANES), lambda b, p, i: (0, p))],
        out_specs=q_spec,
        out_shape=jax.ShapeDtypeStruct((bsz, lq, FOX_W), BF16),
        compiler_params=_params("arbitrary", "arbitrary", "arbitrary"),
        name="fox_attention",
    )(q, ka0, ka1, va0, va1, g_fox)


def _mlstm_kernel(q_ref, k_ref, v_ref, mo_ref, sm_ref, c0_ref, n0_ref, m0_ref, g_ref,
                  o_ref, c_ref, n_ref, m_ref):
    bsz, L, _ = q_ref.shape

    @pl.when(pl.program_id(0) == 0)
    def _():
        c_ref[...] = c0_ref[...]
        n_ref[...] = n0_ref[...]
        m_ref[...] = m0_ref[...]

    causal = _lower_tri(L)
    tri = jnp.where(causal, 1.0, 0.0).astype(BF16)
    eye = jnp.where(lax.broadcasted_iota(jnp.int32, (V7X_LANES, V7X_LANES), 0)
                    == lax.broadcasted_iota(jnp.int32, (V7X_LANES, V7X_LANES), 1), 1.0, 0.0).astype(BF16)
    rows = lax.broadcasted_iota(jnp.int32, (L, V7X_LANES), 0)
    lanes = lax.broadcasted_iota(jnp.int32, (L, V7X_LANES), 1)
    gate_lanes = (lanes >= S_MI) & (lanes < S_END)

    heads = [(b, h) for b in range(bsz) for h in range(ML_HEADS)]
    cols = lambda h: slice(h * ML_DK, (h + 1) * ML_DK)

    ones = jnp.ones((V7X_LANES, V7X_LANES), BF16)
    qk, qc, qn = {}, {}, {}
    for b, h in heads:
        qh = q_ref[b, :, cols(h)]
        qk[b, h] = _dot_nt(qh, k_ref[b, :, cols(h)])
        qc[b, h] = _dot_nt(qh, c_ref[b, h].astype(BF16))
        qn[b, h] = _dot((qh.astype(F32) * n_ref[b, h:h + 1, :]).astype(BF16), ones)

    gates = {}
    for b in range(bsz):
        sm = jnp.where(gate_lanes, sm_ref[b], 0.0)
        t1, t2, t3 = _split3(sm)
        bcum = _dot(tri, t1) + (_dot(tri, t2) + _dot(tri, t3))
        bcum = pltpu.roll(bcum, V7X_LANES - (S_MF - S_MI), axis=1)
        g = sm - bcum
        gmax = g
        sh = 1
        while sh < L:
            gmax = jnp.maximum(gmax, jnp.where(rows >= sh, pltpu.roll(gmax, sh, axis=0), -jnp.inf))
            sh *= 2
        m = m_ref[pl.ds(b, 1), :]
        u = jnp.maximum(m, gmax)
        a = jnp.exp(m - u)
        mt = bcum + u
        m_ref[pl.ds(b, 1), :] = mt[L - 1:L, :]
        g1, g2, g3 = _split3(g)
        g_rows = _mxu_transpose(eye, g1) + (_mxu_transpose(eye, g2) + _mxu_transpose(eye, g3))
        gates[b] = dict(u=u, a=a, em=jnp.exp(-mt), a_last=a[L - 1:L, :], wcol=jnp.exp(g - u[L - 1:L, :]),
                        g_rows=g_rows)

    for b, h in heads:
        gl = S_MI + h
        kh = k_ref[b, :, cols(h)]
        w_h = gates[b]["wcol"][:, gl:gl + 1]
        a_l = gates[b]["a_last"][:, gl:gl + 1]
        vw = (v_ref[b, :, cols(h)].astype(F32) * w_h).astype(BF16)
        c_ref[b, h] = a_l * c_ref[b, h] + _dot(_mxu_transpose(eye, vw).astype(BF16), kh)
        n_ref[b, h:h + 1, :] = (a_l * n_ref[b, h:h + 1, :]
                                + jnp.sum(kh.astype(F32) * w_h, axis=0, keepdims=True))

    for b, h in heads:
        gl = S_MI + h
        gb = gates[b]
        decay = jnp.exp(jnp.where(causal, gb["g_rows"][gl:gl + 1, :] - gb["u"][:, gl:gl + 1], NEG))
        s = qk[b, h] * decay
        a_h = gb["a"][:, gl:gl + 1]
        sb = s.astype(BF16)
        num = a_h * qc[b, h] + _dot(sb, v_ref[b, :, cols(h)])
        den = a_h * qn[b, h] + _dot(sb, ones[:L, :])
        hv = num / jnp.maximum(jnp.abs(den), gb["em"][:, gl:gl + 1])
        hn = hv * lax.rsqrt(_dot((hv * hv).astype(BF16), ones) * (1.0 / ML_DV) + EPS)
        o_ref[b, :, cols(h)] = (hn * g_ref[:, cols(h)] * mo_ref[b, :, cols(h)].astype(F32)).astype(BF16)


def _mlstm(mq, mk, mv, mo, small, c0, n0, m0, g_ml, L):
    bsz, seq, _ = mq.shape
    chunk = pl.BlockSpec((bsz, L, ML_W), lambda c: (0, c, 0))
    full = lambda shape: pl.BlockSpec(shape, lambda c: (0,) * len(shape))
    return pl.pallas_call(
        _mlstm_kernel,
        grid=(seq // L,),
        in_specs=[chunk, chunk, chunk, chunk,
                  pl.BlockSpec((bsz, L, V7X_LANES), lambda c: (0, c, 0)),
                  full(c0.shape), full(n0.shape), full(m0.shape), full((1, ML_W))],
        out_specs=[chunk, full(c0.shape), full(n0.shape), full(m0.shape)],
        out_shape=[jax.ShapeDtypeStruct((bsz, seq, ML_W), BF16),
                   jax.ShapeDtypeStruct(c0.shape, F32),
                   jax.ShapeDtypeStruct(n0.shape, F32),
                   jax.ShapeDtypeStruct(m0.shape, F32)],
        compiler_params=_params("arbitrary"),
        name="mlstm",
    )(mq, mk, mv, mo, small, c0, n0, m0, g_ml)


def _merge_kernel(ca_ref, cm_ref, x_ref, wa_ref, wm_ref, g_ref, wrh_ref, wrl_ref, br_ref,
                  y1_ref, xn_ref, ridx_ref, rgate_ref, cnt_ref):
    tm = x_ref.shape[0]

    @pl.when(pl.program_id(0) == 0)
    def _():
        cnt_ref[...] = jnp.zeros_like(cnt_ref)

    y1 = x_ref[...] + (_dot(ca_ref[...], wa_ref[...]) + _dot(cm_ref[...], wm_ref[...]))
    y1_ref[...] = y1
    xn = (y1 * lax.rsqrt(jnp.mean(y1 * y1, axis=-1, keepdims=True) + EPS)) * g_ref[...]
    xn_ref[...] = xn.reshape(xn_ref.shape)

    xh = xn.astype(BF16)
    xl = (xn - xh.astype(F32)).astype(BF16)
    logits = (_dot(xh, wrh_ref[...]) + (_dot(xl, wrh_ref[...]) + _dot(xh, wrl_ref[...]))) + br_ref[...]

    lane = lax.broadcasted_iota(jnp.int32, logits.shape, 1)
    vals, sel = [], []
    work = logits
    for _ in range(TOP_K):
        mx = jnp.max(work, axis=-1, keepdims=True)
        idx = jnp.min(jnp.where(work == mx, lane, V7X_LANES), axis=-1, keepdims=True)
        vals.append(mx)
        sel.append(idx)
        work = jnp.where(lane == idx, -jnp.inf, work)
    ex = [jnp.exp(v - vals[0]) for v in vals]
    tot = ex[0] + ex[1] + ex[2] + ex[3]

    onehot = [(lane == idx) for idx in sel]
    picked = jnp.where(onehot[0] | onehot[1] | onehot[2] | onehot[3], 1.0, 0.0)
    earlier = (lax.broadcasted_iota(jnp.int32, (tm, tm), 1) < lax.broadcasted_iota(jnp.int32, (tm, tm), 0))
    before = _dot(jnp.where(earlier, 1.0, 0.0).astype(BF16), picked.astype(BF16)) + cnt_ref[...]
    cnt_ref[...] = cnt_ref[...] + jnp.sum(picked, axis=0, keepdims=True)

    ridx = jnp.zeros(logits.shape, jnp.int32)
    rgate = jnp.zeros(logits.shape, F32)
    for k in range(TOP_K):
        rank = jnp.sum(jnp.where(onehot[k], before, 0.0), axis=-1, keepdims=True).astype(jnp.int32)
        ridx = jnp.where(lane == k, sel[k], ridx)
        ridx = jnp.where(lane == TOP_K + k, rank, ridx)
        rgate = jnp.where(lane == k, ex[k] / tot, rgate)
    ridx_ref[...] = ridx
    rgate_ref[...] = rgate


def _merge(cat_a, cat_m, x, w_a, w_m, g_ffn, wr_hi, wr_lo, br, tm):
    t, d = x.shape
    row = lambda c: pl.BlockSpec((tm, c), lambda i: (i, 0))
    const = lambda shape: pl.BlockSpec(shape, lambda i: (0,) * len(shape))
    return pl.pallas_call(
        _merge_kernel,
        grid=(t // tm,),
        in_specs=[row(FOX_W), row(ML_W), row(d), const(w_a.shape), const(w_m.shape), const((1, d)),
                  const(wr_hi.shape), const(wr_lo.shape), const((1, V7X_LANES))],
        out_specs=[row(d), pl.BlockSpec((tm,) + _row_tile(d), lambda i: (i, 0, 0)), row(V7X_LANES),
                   row(V7X_LANES), const((1, V7X_LANES))],
        out_shape=[jax.ShapeDtypeStruct((t, d), F32), jax.ShapeDtypeStruct((t,) + _row_tile(d), F32),
                   jax.ShapeDtypeStruct((t, V7X_LANES), jnp.int32),
                   jax.ShapeDtypeStruct((t, V7X_LANES), F32),
                   jax.ShapeDtypeStruct((1, V7X_LANES), F32)],
        compiler_params=_params("arbitrary"),
        name="merge_router",
    )(cat_a, cat_m, x, w_a, w_m, g_ffn, wr_hi, wr_lo, br)


def _row_tile(d):
    return (d // V7X_LANES, V7X_LANES)


def _row_copy(src_ref, src_row, dst_ref, dst_row, sem):
    return pltpu.make_async_copy(src_ref.at[pl.ds(src_row, 1)], dst_ref.at[pl.ds(dst_row, 1)], sem)


def _scatter_rows(pos_ref, x_ref, xs_ref, sem):
    tm = x_ref.shape[0]

    def issue(g, _):
        for u in range(DMA_UNROLL):
            r = g * DMA_UNROLL + u
            for k in range(TOP_K):
                _row_copy(x_ref, r, xs_ref, pos_ref[0, 0, TOP_K * r + k], sem).start(priority=k % 2)
        return 0

    def drain(g, _):
        for _ in range(DMA_UNROLL * TOP_K):
            _row_copy(x_ref, 0, xs_ref, 0, sem).wait()
        return 0

    lax.fori_loop(0, tm // DMA_UNROLL, issue, 0)
    lax.fori_loop(0, tm // DMA_UNROLL, drain, 0)


def _dispatch_kernel(fs_ref, fl_ref, pos_p_ref, x_p_ref, pos_s_ref, x_s_ref, xs_ref, zero_ref, sem, *, ntp, nts):
    i = pl.program_id(0)

    @pl.when(i < ntp)
    def _():
        _scatter_rows(pos_p_ref, x_p_ref, xs_ref, sem)

    @pl.when((i >= ntp) & (i < ntp + nts))
    def _():
        _scatter_rows(pos_s_ref, x_s_ref, xs_ref, sem)

    @pl.when(i == ntp + nts)
    def _():
        zero_ref[...] = jnp.zeros_like(zero_ref)
        full = zero_ref.shape[0]
        bits = [1 << p for p in range(full.bit_length() - 2, -1, -1)]

        def zeros_to(start, size):
            return pltpu.make_async_copy(zero_ref.at[pl.ds(0, size)], xs_ref.at[pl.ds(start, size)], sem)

        def per_run(e, _, wait):
            start, n = fs_ref[e], fl_ref[e]
            whole = n // full

            def whole_copy(q, _):
                c = zeros_to(start + q * full, full)
                c.wait() if wait else c.start()
                return 0

            lax.fori_loop(0, whole, whole_copy, 0)
            for p in bits:
                @pl.when((n & p) != 0)
                def _():
                    c = zeros_to(start + (n & ~(2 * p - 1)), p)
                    c.wait() if wait else c.start()
            return 0

        lax.fori_loop(0, fs_ref.shape[0], functools.partial(per_run, wait=False), 0)
        lax.fori_loop(0, fs_ref.shape[0], functools.partial(per_run, wait=True), 0)


def _dispatch(fill_start, fill_len, pos3_p, x_p, pos3_s, x_s, n_rows, tm_p, tm_s):
    tp, sub, lanes = x_p.shape
    ntp, nts = tp // tm_p, x_s.shape[0] // tm_s
    clamp_p = lambda i, fs, fl: (jnp.minimum(i, ntp - 1), 0, 0)
    clamp_s = lambda i, fs, fl: (jnp.clip(i - ntp, 0, nts - 1), 0, 0)
    grid_spec = pltpu.PrefetchScalarGridSpec(
        num_scalar_prefetch=2,
        grid=(ntp + nts + 1,),
        in_specs=[pl.BlockSpec((1, 1, TOP_K * tm_p), clamp_p, memory_space=pltpu.SMEM),
                  pl.BlockSpec((tm_p, sub, lanes), clamp_p),
                  pl.BlockSpec((1, 1, TOP_K * tm_s), clamp_s, memory_space=pltpu.SMEM),
                  pl.BlockSpec((tm_s, sub, lanes), clamp_s)],
        out_specs=pl.BlockSpec(memory_space=pl.ANY),
        scratch_shapes=[pltpu.VMEM((FFN_ROWS, sub, lanes), F32), pltpu.SemaphoreType.DMA(())],
    )
    return pl.pallas_call(
        functools.partial(_dispatch_kernel, ntp=ntp, nts=nts),
        grid_spec=grid_spec,
        out_shape=jax.ShapeDtypeStruct((n_rows, sub, lanes), F32),
        compiler_params=_params("arbitrary"),
        name="moe_dispatch",
    )(fill_start, fill_len, pos3_p, x_p, pos3_s, x_s)


def _ffn_kernel(te_ref, nu_ref, x_ref, wgu_ref, bgu_ref, wd_ref, bd_ref, y_ref, wgu16_ref, wd16_ref):
    i = pl.program_id(0)
    d_ff = wd_ref.shape[1]

    @pl.when((i == 0) | (te_ref[i] != te_ref[jnp.maximum(i - 1, 0)]))
    def _():
        wgu16_ref[...] = wgu_ref[0].astype(BF16)
        wd16_ref[...] = wd_ref[0].astype(BF16)

    @pl.when(i < nu_ref[0])
    def _():
        x = x_ref[...].reshape(x_ref.shape[0], -1)
        gu = _dot(x.astype(BF16), wgu16_ref[...]) + bgu_ref[0]
        gate = jnp.minimum(gu[:, :d_ff], SWIGLU_LIMIT)
        up = jnp.clip(gu[:, d_ff:], -SWIGLU_LIMIT, SWIGLU_LIMIT)
        h = (up + 1.0) * (gate * _sigmoid(SWIGLU_ALPHA * gate))
        y = _dot(h.astype(BF16), wd16_ref[...]) + bd_ref[0]
        y_ref[...] = y.reshape(y_ref.shape)

    @pl.when(i >= nu_ref[0])
    def _():
        y_ref[...] = jnp.zeros_like(y_ref)


def _ffn(tile_expert, n_used, xs, w_gu, b_gu, w_d, b_d, tm):
    r = xs.shape[0]
    d_ff, d = w_d.shape[1:]
    tiles = pl.BlockSpec((tm,) + _row_tile(d), lambda i, te, nu: (i, 0, 0))
    grid_spec = pltpu.PrefetchScalarGridSpec(
        num_scalar_prefetch=2,
        grid=(r // tm,),
        in_specs=[tiles,
                  pl.BlockSpec((1, d, 2 * d_ff), lambda i, te, nu: (te[i], 0, 0)),
                  pl.BlockSpec((1, 1, 2 * d_ff), lambda i, te, nu: (te[i], 0, 0)),
                  pl.BlockSpec((1, d_ff, d), lambda i, te, nu: (te[i], 0, 0)),
                  pl.BlockSpec((1, 1, d), lambda i, te, nu: (te[i], 0, 0))],
        out_specs=tiles,
        scratch_shapes=[pltpu.VMEM((d, 2 * d_ff), BF16), pltpu.VMEM((d_ff, d), BF16)],
    )
    return pl.pallas_call(
        _ffn_kernel,
        grid_spec=grid_spec,
        out_shape=jax.ShapeDtypeStruct(xs.shape, F32),
        compiler_params=_params("arbitrary"),
        name="moe_ffn",
    )(tile_expert, n_used, xs, w_gu, b_gu, w_d, b_d)


def _combine_kernel(pos_ref, pos_next_ref, y1_ref, gate_ref, gfin_ref, ys_ref, o_ref, buf_ref, sem):
    tm = y1_ref.shape[0]
    i = pl.program_id(0)
    slot = i % 2

    def gather(p_ref, s):
        def issue(g, _):
            for u in range(DMA_UNROLL):
                r = g * DMA_UNROLL + u
                for k in range(TOP_K):
                    _row_copy(ys_ref, p_ref[0, 0, TOP_K * r + k], buf_ref.at[s, k], r,
                              sem.at[s]).start(priority=k % 2)
            return 0

        lax.fori_loop(0, tm // DMA_UNROLL, issue, 0)

    @pl.when(i == 0)
    def _():
        gather(pos_ref, 0)

    @pl.when(i + 1 < pl.num_programs(0))
    def _():
        gather(pos_next_ref, 1 - slot)

    def drain(g, _):
        for _ in range(DMA_UNROLL * TOP_K):
            _row_copy(ys_ref, 0, buf_ref.at[slot, 0], 0, sem.at[slot]).wait()
        return 0

    lax.fori_loop(0, tm // DMA_UNROLL, drain, 0)

    gate = gate_ref[...]
    moe = gate[:, 0:1] * buf_ref[slot, 0].reshape(tm, -1)
    for k in range(1, TOP_K):
        moe = moe + gate[:, k:k + 1] * buf_ref[slot, k].reshape(tm, -1)
    y = y1_ref[...] + moe
    o_ref[...] = (y * lax.rsqrt(jnp.mean(y * y, axis=-1, keepdims=True) + EPS)) * gfin_ref[...]


def _combine(pos3, y1, rgate, g_final, ys, tm):
    t, d = y1.shape
    nt = t // tm
    pos_spec = lambda f: pl.BlockSpec((1, 1, TOP_K * tm), f, memory_space=pltpu.SMEM)
    return pl.pallas_call(
        _combine_kernel,
        grid=(nt,),
        in_specs=[pos_spec(lambda i: (i, 0, 0)), pos_spec(lambda i: (jnp.minimum(i + 1, nt - 1), 0, 0)),
                  pl.BlockSpec((tm, d), lambda i: (i, 0)),
                  pl.BlockSpec((tm, V7X_LANES), lambda i: (i, 0)),
                  pl.BlockSpec((1, d), lambda i: (0, 0)),
                  pl.BlockSpec(memory_space=pl.ANY)],
        out_specs=pl.BlockSpec((tm, d), lambda i: (i, 0)),
        out_shape=jax.ShapeDtypeStruct((t, d), F32),
        scratch_shapes=[pltpu.VMEM((2, TOP_K, tm) + _row_tile(d), F32), pltpu.SemaphoreType.DMA((2,))],
        compiler_params=_params("arbitrary"),
        name="moe_combine",
    )(pos3, pos3, y1, rgate, g_final, ys)


def _mixer(x, hist, fox_cache, ml_state, w, tm, tq, tk, chunk):
    bsz, seq, _ = x.shape
    hist8 = jnp.pad(hist, ((0, 0), (V7X_SUBLANES - (CONV_W - 1), 0), (0, 0)))
    proj = _project(x, hist8, w["g_mix"], w["wbig"], w["wsm"], w["bsm"], w["conv_w"], w["conv_b"], tm,
                    aug=fox_cache is None)

    if fox_cache is None:
        q_offset = 0
        assert seq % tk == 0
        (fq, fk32, fv32, ka0, ka1, va0, va1, small, flogf, mq, mk, mv, mo, conv_new) = proj
    else:
        (fq, fk32, fv32, fk16, fv16, small, flogf, mq, mk, mv, mo, conv_new) = proj
        ck_c, cv_c, clogf_c = fox_cache
        q_offset = ck_c.shape[1]
        assert q_offset + seq <= tk
        clogf_c = jnp.pad(clogf_c, ((0, 0), (0, 0), (S_FF, V7X_LANES - S_FF - FOX_HEADS)))
        lf_all = jnp.pad(jnp.concatenate([clogf_c, small], axis=1), ((0, 0), (0, tk - q_offset - seq), (0, 0)))
        ka0, ka1, va0, va1 = _fox_prep(lf_all, [ck_c.reshape(bsz, q_offset, FOX_W), fk16],
                                       [cv_c.reshape(bsz, q_offset, FOX_W), fv16], tk)
    cat_a = _fox(fq, ka0, ka1, va0, va1, w["g_fox"], q_offset, tq, tk)

    c0, n0, m0 = ml_state
    m0 = jnp.pad(m0, ((0, 0), (S_MI, V7X_LANES - S_MI - ML_HEADS)))
    cat_m, c_new, n_new, m_new = _mlstm(mq, mk, mv, mo, small, c0, n0, m0, w["g_ml"], chunk)
    m_new = m_new[:, S_MI:S_MI + ML_HEADS]

    states = (fk32.reshape(bsz, seq, FOX_HEADS, FOX_HD), fv32.reshape(bsz, seq, FOX_HEADS, FOX_HD), flogf,
              c_new, n_new, m_new, conv_new)
    return cat_a, cat_m, states


def kernel(x_prompt, x_sample, cache_fox_k, cache_fox_v, cache_fox_logf, state_mlstm_C, state_mlstm_n,
           state_mlstm_m, state_mlstm_conv, norm_mix_g, w_in, b_fox_f, conv_w, conv_b, b_ml_i, b_ml_f,
           g_fox, g_ml, w_out, norm_ffn_g, w_router, b_router, w_gate_up, b_gate_up, w_down, b_down,
           norm_final_g):
    depth = w_in.shape[0]
    assert depth == 1, "the final norm is fused into the last layer's combine; only depth 1 is wired up"
    bp, sp, d = x_prompt.shape
    bs, ss, _ = x_sample.shape
    n_exp = w_router.shape[-1]
    d_ff = w_down.shape[2]
    yp, ys = x_prompt, x_sample
    p_st, s_st = [], []

    for l in range(depth):
        wl = w_in[l]
        w = {
            "g_mix": norm_mix_g[l][None, :],
            "wbig": jnp.concatenate([wl[:, O_FQ:O_FF], wl[:, O_MQ:O_MI]], axis=1).astype(BF16),
            "wsm": jnp.pad(jnp.concatenate([wl[:, O_FF:O_MQ], wl[:, O_MI:P_IN]], axis=1),
                           ((0, 0), (0, V7X_LANES - S_END))).astype(BF16),
            "bsm": jnp.pad(jnp.concatenate([b_fox_f[l], b_ml_i[l], b_ml_f[l]]), (0, V7X_LANES - S_END))[None, :],
            "conv_w": conv_w[l],
            "conv_b": conv_b[l][None, :],
            "g_fox": g_fox[l][None, :],
            "g_ml": g_ml[l][None, :],
        }
        w_a = w_out[l][:FOX_W].astype(BF16)
        w_m = w_out[l][FOX_W:].astype(BF16)
        g_ffn = norm_ffn_g[l][None, :]
        wr = jnp.pad(w_router[l], ((0, 0), (0, V7X_LANES - n_exp)))
        wr_hi = wr.astype(BF16)
        wr_lo = (wr - wr_hi.astype(F32)).astype(BF16)
        br = jnp.pad(b_router[l], (0, V7X_LANES - n_exp), constant_values=NEG)[None, :]

        zeros_state = (jnp.zeros((bp, ML_HEADS, ML_DV, ML_DK), F32), jnp.zeros((bp, ML_HEADS, ML_DK), F32),
                       jnp.zeros((bp, ML_HEADS), F32))
        cat_a_p, cat_m_p, st_p = _mixer(yp, jnp.zeros((bp, CONV_W - 1, QK_W), F32), None, zeros_state, w,
                                        min(PROJ_ROWS, sp), min(FOX_QBLOCK, sp), min(FOX_BLOCK, sp), CHUNK)
        p_st.append(st_p)
        past = cache_fox_k.shape[2]
        tk_s = -(-(past + ss) // V7X_LANES) * V7X_LANES
        cat_a_s, cat_m_s, st_s = _mixer(
            ys, state_mlstm_conv[l], (cache_fox_k[l], cache_fox_v[l], cache_fox_logf[l]),
            (state_mlstm_C[l].astype(F32), state_mlstm_n[l].astype(F32), state_mlstm_m[l].astype(F32)),
            w, ss, ss, tk_s, ss)
        s_st.append(st_s)

        tp, ts = bp * sp, bs * ss
        y1_p, xn_p, ridx_p, rgate_p, cnt_p = _merge(
            cat_a_p.reshape(tp, FOX_W), cat_m_p.reshape(tp, ML_W), yp.reshape(tp, d), w_a, w_m, g_ffn,
            wr_hi, wr_lo, br, min(MERGE_ROWS, tp))
        y1_s, xn_s, ridx_s, rgate_s, cnt_s = _merge(
            cat_a_s.reshape(ts, FOX_W), cat_m_s.reshape(ts, ML_W), ys.reshape(ts, d), w_a, w_m, g_ffn,
            wr_hi, wr_lo, br, min(MERGE_ROWS, ts))

        cnt_p = cnt_p[0, :n_exp].astype(jnp.int32)
        cnt_s = cnt_s[0, :n_exp].astype(jnp.int32)
        seg_rows = -(-(cnt_p + cnt_s) // FFN_ROWS) * FFN_ROWS
        seg_end = jnp.cumsum(seg_rows)
        seg_start = seg_end - seg_rows
        n_rows = -(-((tp + ts) * TOP_K + n_exp * (FFN_ROWS - 1)) // FFN_ROWS) * FFN_ROWS
        n_tiles = n_rows // FFN_ROWS
        tile_row0 = jnp.arange(n_tiles, dtype=jnp.int32) * FFN_ROWS
        tile_expert = jnp.minimum(jnp.sum((seg_end[None, :] <= tile_row0[:, None]).astype(jnp.int32), axis=1),
                                  n_exp - 1)
        n_used = (seg_end[-1:] // FFN_ROWS).astype(jnp.int32)
        e_p, rank_p = ridx_p[:, :TOP_K].reshape(-1), ridx_p[:, TOP_K:2 * TOP_K].reshape(-1)
        e_s, rank_s = ridx_s[:, :TOP_K].reshape(-1), ridx_s[:, TOP_K:2 * TOP_K].reshape(-1)
        pos_p = seg_start[e_p] + rank_p
        pos_s = seg_start[e_s] + cnt_p[e_s] + rank_s
        tm_p, tm_s = min(MOE_ROWS, tp), min(MOE_ROWS, ts)
        pos3_p = pos_p.reshape(tp // tm_p, 1, TOP_K * tm_p)
        pos3_s = pos_s.reshape(ts // tm_s, 1, TOP_K * tm_s)

        cnt = cnt_p + cnt_s
        fill_start = jnp.concatenate([seg_start + cnt, seg_end[-1:]]).astype(jnp.int32)
        fill_len = jnp.concatenate([seg_rows - cnt, n_rows - seg_end[-1:]]).astype(jnp.int32)
        xs = _dispatch(fill_start, fill_len, pos3_p, xn_p, pos3_s, xn_s, n_rows, tm_p, tm_s)
        ysort = _ffn(tile_expert, n_used, xs, w_gate_up[l], b_gate_up[l].reshape(n_exp, 1, 2 * d_ff),
                     w_down[l], b_down[l].reshape(n_exp, 1, d), FFN_ROWS)

        g_fin = norm_final_g[None, :]
        yp = _combine(pos3_p, y1_p, rgate_p, g_fin, ysort, tm_p).reshape(bp, sp, d)
        ys = _combine(pos3_s, y1_s, rgate_s, g_fin, ysort, tm_s).reshape(bs, ss, d)

    p_out = tuple(jnp.stack(a) for a in zip(*p_st))
    s_out = tuple(jnp.stack(a) for a in zip(*s_st))
    return (yp, ys) + p_out + s_out
```

```python
import functools
import math

import numpy as np

import jax
import jax.numpy as jnp
from jax import lax
from jax.experimental import pallas as pl
from jax.experimental.pallas import tpu as pltpu

F32 = jnp.float32
BF16 = jnp.bfloat16

FOX_HEADS = 8
FOX_HD = 64
FOX_W = FOX_HEADS * FOX_HD
ML_HEADS = 4
ML_DK = 128
ML_DV = 128
ML_W = ML_HEADS * ML_DV
QK_W = 2 * ML_HEADS * ML_DK
CONV_W = 4
CHUNK = 64
TOP_K = 4
SWIGLU_LIMIT = 7.0
SWIGLU_ALPHA = 1.702
EPS = 1e-6
NEG = -1e30
LOG2E = math.log2(math.e)

O_FQ = 0
O_FK = O_FQ + FOX_W
O_FV = O_FK + FOX_W
O_FF = O_FV + FOX_W
O_MQ = O_FF + FOX_HEADS
O_MK = O_MQ + ML_HEADS * ML_DK
O_MV = O_MK + ML_HEADS * ML_DK
O_MO = O_MV + ML_W
O_MI = O_MO + ML_W
O_MF = O_MI + ML_HEADS
P_IN = O_MF + ML_HEADS

B_FQ, B_FK, B_FV, B_QK, B_MV, B_MO, B_END = 0, 512, 1024, 1536, 2560, 3072, 3584
S_FF, S_MI, S_MF, S_END = 0, 8, 12, 16
BIAS_TERMS = 3

V7X_LANES = 128
V7X_SUBLANES = 8
V7X_VMEM_LIMIT_BYTES = 56 * 1024 * 1024

PROJ_ROWS = 512
FOX_BLOCK = 512
FOX_QBLOCK = 1024
FOX_SUB = 512
FOX_UNROLLS = (4, 2, 1)
MERGE_ROWS = 512
MOE_ROWS = 512
FFN_ROWS = 512
DMA_UNROLL = 8


def _params(*semantics):
    return pltpu.CompilerParams(dimension_semantics=semantics, vmem_limit_bytes=V7X_VMEM_LIMIT_BYTES)


def _dot(a, b):
    return jnp.dot(a, b, preferred_element_type=F32)


def _dot_nt(a, b):
    return lax.dot_general(a, b, (((1,), (1,)), ((), ())), preferred_element_type=F32)


def _mxu_transpose(eye, x):
    return _dot_nt(eye, x)


def _sigmoid(x):
    return 1.0 / (1.0 + jnp.exp(-x))


def _log_sigmoid(x):
    return jnp.minimum(x, 0.0) - jnp.log1p(jnp.exp(-jnp.abs(x)))


def _split3(x):
    t1 = x.astype(BF16)
    r1 = x - t1.astype(F32)
    t2 = r1.astype(BF16)
    t3 = (r1 - t2.astype(F32)).astype(BF16)
    return t1, t2, t3


def _lower_tri(n):
    r = lax.broadcasted_iota(jnp.int32, (n, n), 0)
    c = lax.broadcasted_iota(jnp.int32, (n, n), 1)
    return c <= r


def _proj_kernel(*refs, aug):
    x_ref, g_ref, wbig_ref, wsm_ref, bsm_ref, cw_ref, cb_ref, hist_ref = refs[:8]
    rest = refs[8:]
    if aug:
        place_ref, rest = rest[0], rest[1:]
        (fq_ref, fk32_ref, fv32_ref, ka0_ref, ka1_ref, va0_ref, va1_ref, sm_ref, flogf_ref, mq_ref, mk_ref,
         mv_ref, mo_ref, cnew_ref, halo_ref, carry_ref) = rest
    else:
        (fq_ref, fk32_ref, fv32_ref, fk16_ref, fv16_ref, sm_ref, flogf_ref, mq_ref, mk_ref, mv_ref, mo_ref,
         cnew_ref, halo_ref) = rest
    tm = x_ref.shape[1]

    @pl.when(pl.program_id(1) == 0)
    def _():
        halo_ref[...] = hist_ref[0]
        if aug:
            carry_ref[...] = jnp.zeros_like(carry_ref)

    x = x_ref[0]
    xn = (x * lax.rsqrt(jnp.mean(x * x, axis=-1, keepdims=True) + EPS)) * g_ref[...]
    xb = xn.astype(BF16)
    zs = _dot(xb, wsm_ref[...]) + bsm_ref[...]
    z = _dot(xb, wbig_ref[...])

    fq_ref[0] = (z[:, B_FQ:B_FQ + FOX_W] * (FOX_HD ** -0.5 * LOG2E)).astype(BF16)
    fk = z[:, B_FK:B_FK + FOX_W]
    fv = z[:, B_FV:B_FV + FOX_W]
    fk32_ref[0] = fk
    fv32_ref[0] = fv
    mv_ref[0] = z[:, B_MV:B_MV + ML_W].astype(BF16)
    mo_ref[0] = _sigmoid(z[:, B_MO:B_MO + ML_W]).astype(BF16)

    lane = lax.broadcasted_iota(jnp.int32, zs.shape, 1)
    is_forget = (lane < S_MI) | ((lane >= S_MF) & (lane < S_END))
    sm = jnp.where(is_forget, _log_sigmoid(zs), zs)
    sm_ref[0] = sm
    flogf_ref[0] = sm[:, S_FF:S_MI]

    if aug:
        rows = lax.broadcasted_iota(jnp.int32, sm.shape, 0)
        c = sm
        sh = 1
        while sh < tm:
            c = c + jnp.where(rows >= sh, pltpu.roll(c, sh, axis=0), 0.0)
            sh *= 2
        c = c + carry_ref[...]
        carry_ref[...] = c[tm - 1:tm, :]
        b1, b2, b3 = _split3(c * (-LOG2E))
        kb = _dot(b1, place_ref[0]) + (_dot(b2, place_ref[1]) + _dot(b3, place_ref[2]))
        first_head = lax.broadcasted_iota(jnp.int32, fk.shape, 1) % V7X_LANES < FOX_HD
        ka0_ref[0] = jnp.where(first_head, fk, kb).astype(BF16)
        ka1_ref[0] = jnp.where(first_head, kb, fk).astype(BF16)
        va0_ref[0] = jnp.where(first_head, fv, 1.0).astype(BF16)
        va1_ref[0] = jnp.where(first_head, 1.0, fv).astype(BF16)
    else:
        fk16_ref[0] = fk.astype(BF16)
        fv16_ref[0] = fv.astype(BF16)

    u = z[:, B_QK:B_QK + QK_W]
    up = jnp.concatenate([halo_ref[...], u], axis=0)
    n = tm + V7X_SUBLANES
    first = V7X_SUBLANES - (CONV_W - 1)
    y = cb_ref[...] + cw_ref[0:1, :] * pltpu.roll(up, n - first, axis=0)[:tm]
    for j in range(1, CONV_W - 1):
        y = y + cw_ref[j:j + 1, :] * pltpu.roll(up, n - (first + j), axis=0)[:tm]
    y = y + cw_ref[CONV_W - 1:CONV_W, :] * u
    qk = y * _sigmoid(y)
    mq_ref[0] = qk[:, :QK_W // 2].astype(BF16)
    mk_ref[0] = (qk[:, QK_W // 2:] * (ML_DK ** -0.5)).astype(BF16)

    halo_ref[...] = u[tm - V7X_SUBLANES:, :]
    cnew_ref[0] = halo_ref[first:, :]


def _project(x, hist8, g, wbig, wsm, bsm, cw, cb, tm, aug):
    bsz, seq, d = x.shape
    grid = (bsz, seq // tm)
    row = lambda c: pl.BlockSpec((1, tm, c), lambda b, s: (b, s, 0))
    const = lambda shape: pl.BlockSpec(shape, lambda b, s: (0,) * len(shape))
    n_kv = 4 if aug else 2
    outs = ([(FOX_W, BF16), (FOX_W, F32), (FOX_W, F32)] + [(FOX_W, BF16)] * n_kv
            + [(V7X_LANES, F32), (FOX_HEADS, F32), (ML_W, BF16), (ML_W, BF16), (ML_W, BF16), (ML_W, BF16)])
    out_shape = [jax.ShapeDtypeStruct((bsz, seq, c), dt) for c, dt in outs]
    out_specs = [row(c) for c, _ in outs]
    out_shape.append(jax.ShapeDtypeStruct((bsz, CONV_W - 1, QK_W), F32))
    out_specs.append(pl.BlockSpec((1, CONV_W - 1, QK_W), lambda b, s: (b, 0, 0)))
    in_specs = [row(d), const((1, d)), const(wbig.shape), const(wsm.shape), const((1, V7X_LANES)),
                const((CONV_W, QK_W)), const((1, QK_W)),
                pl.BlockSpec((1, V7X_SUBLANES, QK_W), lambda b, s: (b, 0, 0))]
    args = [x, g, wbig, wsm, bsm, cw, cb, hist8]
    scratch = [pltpu.VMEM((V7X_SUBLANES, QK_W), F32)]
    if aug:
        place = _bias_placement()
        in_specs.append(const(place.shape))
        args.append(place)
        scratch.append(pltpu.VMEM((1, V7X_LANES), F32))
    return pl.pallas_call(
        functools.partial(_proj_kernel, aug=aug),
        grid=grid,
        in_specs=in_specs,
        out_specs=out_specs,
        out_shape=out_shape,
        scratch_shapes=scratch,
        compiler_params=_params("arbitrary", "arbitrary"),
        name="proj",
    )(*args)


def _bias_lane(head, term):
    return (head // 2) * V7X_LANES + (FOX_HD if head % 2 == 0 else 0) + term


def _bias_placement():
    place = np.zeros((BIAS_TERMS, V7X_LANES, FOX_W), np.float32)
    for h in range(FOX_HEADS):
        for t in range(BIAS_TERMS):
            place[t, S_FF + h, _bias_lane(h, t)] = 1.0
    return jnp.asarray(place, BF16)


def _fox_prep_kernel(lf_ref, *refs, n_parts):
    k_refs, v_refs = refs[:n_parts], refs[n_parts:2 * n_parts]
    place_ref, ka0_ref, ka1_ref, va0_ref, va1_ref, carry_ref = refs[2 * n_parts:]
    tm = lf_ref.shape[1]

    def rows(part_refs):
        parts = [r[0].astype(F32) for r in part_refs]
        missing = tm - sum(p.shape[0] for p in parts)
        if missing:
            parts.append(jnp.zeros((missing, FOX_W), F32))
        return parts[0] if len(parts) == 1 else jnp.concatenate(parts, axis=0)

    @pl.when(pl.program_id(1) == 0)
    def _():
        carry_ref[...] = jnp.zeros_like(carry_ref)

    tri = jnp.where(_lower_tri(tm), 1.0, 0.0).astype(BF16)
    t1, t2, t3 = _split3(lf_ref[0])
    c = carry_ref[...] + (_dot(tri, t1) + (_dot(tri, t2) + _dot(tri, t3)))
    carry_ref[...] = c[tm - 1:tm, :]
    b1, b2, b3 = _split3(c * (-LOG2E))
    kb = _dot(b1, place_ref[0]) + (_dot(b2, place_ref[1]) + _dot(b3, place_ref[2]))

    k = rows(k_refs)
    v = rows(v_refs)
    first_head = lax.broadcasted_iota(jnp.int32, k.shape, 1) % V7X_LANES < FOX_HD
    ka0_ref[0] = jnp.where(first_head, k, kb).astype(BF16)
    ka1_ref[0] = jnp.where(first_head, kb, k).astype(BF16)
    va0_ref[0] = jnp.where(first_head, v, 1.0).astype(BF16)
    va1_ref[0] = jnp.where(first_head, 1.0, v).astype(BF16)


def _fox_prep(lf, k_parts, v_parts, tm):
    bsz, lk, _ = lf.shape
    place = _bias_placement()
    wide = pl.BlockSpec((1, tm, FOX_W), lambda b, s: (b, s, 0))
    if len(k_parts) == 1:
        part_specs = [wide, wide]
    else:
        assert lk == tm, "several key/value parts are only stitched inside a single tile"
        part_specs = [pl.BlockSpec((1,) + p.shape[1:], lambda b, s: (b, 0, 0)) for p in k_parts + v_parts]
    sds = jax.ShapeDtypeStruct((bsz, lk, FOX_W), BF16)
    return pl.pallas_call(
        functools.partial(_fox_prep_kernel, n_parts=len(k_parts)),
        grid=(bsz, lk // tm),
        in_specs=[pl.BlockSpec((1, tm, V7X_LANES), lambda b, s: (b, s, 0))] + part_specs
        + [pl.BlockSpec(place.shape, lambda b, s: (0, 0, 0))],
        out_specs=[wide, wide, wide, wide],
        out_shape=[sds, sds, sds, sds],
        scratch_shapes=[pltpu.VMEM((1, V7X_LANES), F32)],
        compiler_params=_params("arbitrary", "arbitrary"),
        name="fox_prep",
    )(lf, *k_parts, *v_parts, place)


def _fox_kernel(q_ref, ka0_ref, ka1_ref, va0_ref, va1_ref, g_ref, o_ref, *, tq, tk, sub, q_offset):
    ts = min(tq, sub)
    n_sub = tq // ts
    diag_blocks = max(ts // tk, 1)
    k_refs = (ka0_ref, ka1_ref)
    v_refs = (va0_ref, va1_ref)
    lane = lax.broadcasted_iota(jnp.int32, (ts, V7X_LANES), 1)
    first_head = lane < FOX_HD
    q_lo = q_offset + pl.program_id(2) * tq
    n_common = (q_lo + 1) // tk
    single_block = ka0_ref.shape[1] == tk
    even_common = q_offset % (2 * tk) == 0 and tq % (2 * tk) == 0

    def chain_q(r, hh):
        q = q_ref[0, r * ts:(r + 1) * ts, :].astype(F32)
        if hh == 0:
            return jnp.where(first_head, q, jnp.where(lane < FOX_HD + BIAS_TERMS, 1.0, 0.0)).astype(BF16)
        return jnp.where(first_head, jnp.where(lane < BIAS_TERMS, 1.0, 0.0), q).astype(BF16)

    chains = [(r, hh) for r in range(n_sub) for hh in range(2)]
    qs = [chain_q(r, hh) for r, hh in chains]

    def scores(c, j):
        return _dot_nt(qs[c], k_refs[chains[c][1]][0, pl.ds(pl.multiple_of(j * tk, tk), tk), :])

    def update(state, c, j, s, masked):
        r, hh = chains[c]
        m, acc = state
        start = pl.multiple_of(j * tk, tk)
        if masked:
            qpos = q_lo + r * ts + lax.broadcasted_iota(jnp.int32, s.shape, 0)
            kpos = start + lax.broadcasted_iota(jnp.int32, s.shape, 1)
            s = jnp.where(kpos <= qpos, s, NEG)
        m_new = jnp.maximum(m, jnp.max(s, axis=-1, keepdims=True))
        p = jnp.exp2(s - m_new).astype(BF16)
        acc = jnp.exp2(m - m_new) * acc + _dot(p, v_refs[hh][0, pl.ds(start, tk), :])
        return m_new, acc

    def step(state, c, j, masked):
        return update(state, c, j, scores(c, j), masked)

    def common(j, states):
        ss = [scores(c, j) for c in range(len(chains))]
        return tuple(update(st, c, j, ss[c], False) for c, st in enumerate(states))

    def common_run(t, states, nblk):
        states = list(states)
        cur = [scores(c, nblk * t) for c in range(len(chains))]
        for e in range(nblk):
            nxt = []
            for c in range(len(chains)):
                states[c] = update(states[c], c, nblk * t + e, cur[c], False)
                if e + 1 < nblk:
                    nxt.append(scores(c, nblk * t + e + 1))
            cur = nxt
        return tuple(states)

    init = (jnp.full((ts, 1), NEG, F32), jnp.zeros((ts, V7X_LANES), F32))
    states = (init,) * len(chains)
    if not single_block:
        done = 0
        for nblk in FOX_UNROLLS:
            if nblk == 1 and even_common:
                continue
            trips = n_common // nblk
            states = lax.fori_loop(done // nblk, trips, functools.partial(common_run, nblk=nblk), states)
            done = trips * nblk
    states = list(states)
    for c, (r, _) in enumerate(chains):
        for e in range(r * diag_blocks):
            states[c] = step(states[c], c, n_common + e, False)
        for e in range(diag_blocks):
            states[c] = step(states[c], c, n_common + r * diag_blocks + e, True)

    for r in range(n_sub):
        out = jnp.zeros((ts, V7X_LANES), F32)
        for hh in range(2):
            acc = states[chains.index((r, hh))][1]
            own = first_head if hh == 0 else jnp.logical_not(first_head)
            denom_lane = FOX_HD if hh == 0 else 0
            o = jnp.where(own, acc / acc[:, denom_lane:denom_lane + 1], 0.0)
            out = out + o * lax.rsqrt(jnp.sum(o * o, axis=-1, keepdims=True) * (1.0 / FOX_HD) + EPS)
        o_ref[0, r * ts:(r + 1) * ts, :] = (out * g_ref[...]).astype(BF16)


def _fox(q, ka0, ka1, va0, va1, g_fox, q_offset, tq, tk):
    bsz, lq, _ = q.shape
    lk = ka0.shape[1]
    assert (q_offset % tk == 0 and tq % tk == 0) or (lq == tq <= tk and lk == tk), (q_offset, tq, tk, lq, lk)
    pairs = FOX_HEADS // 2
    grid = (bsz, pairs, lq // tq)
    kv_spec = pl.BlockSpec((1, lk, V7X_LANES), lambda b, p, i: (b, 0, p))
    q_spec = pl.BlockSpec((1, tq, V7X_LANES), lambda b, p, i: (b, i, p))
    return pl.pallas_call(
        functools.partial(_fox_kernel, tq=tq, tk=tk, sub=FOX_SUB, q_offset=q_offset),
        grid=grid,
        in_specs=[q_spec, kv_spec, kv_spec, kv_spec, kv_spec,
                  pl.BlockSpec((1, V7X_LANES), lambda b, p, i: (0, p))],
        out_specs=q_spec,
        out_shape=jax.ShapeDtypeStruct((bsz, lq, FOX_W), BF16),
        compiler_params=_params("arbitrary", "arbitrary", "arbitrary"),
        name="fox_attention",
    )(q, ka0, ka1, va0, va1, g_fox)


def _mlstm_kernel(q_ref, k_ref, v_ref, mo_ref, sm_ref, c0_ref, n0_ref, m0_ref, g_ref,
                  o_ref, c_ref, n_ref, m_ref):
    bsz, L, _ = q_ref.shape

    @pl.when(pl.program_id(0) == 0)
    def _():
        c_ref[...] = c0_ref[...]
        n_ref[...] = n0_ref[...]
        m_ref[...] = m0_ref[...]

    causal = _lower_tri(L)
    tri = jnp.where(causal, 1.0, 0.0).astype(BF16)
    eye = jnp.where(lax.broadcasted_iota(jnp.int32, (V7X_LANES, V7X_LANES), 0)
                    == lax.broadcasted_iota(jnp.int32, (V7X_LANES, V7X_LANES), 1), 1.0, 0.0).astype(BF16)
    rows = lax.broadcasted_iota(jnp.int32, (L, V7X_LANES), 0)
    lanes = lax.broadcasted_iota(jnp.int32, (L, V7X_LANES), 1)
    gate_lanes = (lanes >= S_MI) & (lanes < S_END)

    heads = [(b, h) for b in range(bsz) for h in range(ML_HEADS)]
    cols = lambda h: slice(h * ML_DK, (h + 1) * ML_DK)

    ones = jnp.ones((V7X_LANES, V7X_LANES), BF16)
    qk, qc, qn = {}, {}, {}
    for b, h in heads:
        qh = q_ref[b, :, cols(h)]
        qk[b, h] = _dot_nt(qh, k_ref[b, :, cols(h)])
        qc[b, h] = _dot_nt(qh, c_ref[b, h].astype(BF16))
        qn[b, h] = _dot((qh.astype(F32) * n_ref[b, h:h + 1, :]).astype(BF16), ones)

    gates = {}
    for b in range(bsz):
        sm = jnp.where(gate_lanes, sm_ref[b], 0.0)
        t1, t2, t3 = _split3(sm)
        bcum = _dot(tri, t1) + (_dot(tri, t2) + _dot(tri, t3))
        bcum = pltpu.roll(bcum, V7X_LANES - (S_MF - S_MI), axis=1)
        g = sm - bcum
        gmax = g
        sh = 1
        while sh < L:
            gmax = jnp.maximum(gmax, jnp.where(rows >= sh, pltpu.roll(gmax, sh, axis=0), -jnp.inf))
            sh *= 2
        m = m_ref[pl.ds(b, 1), :]
        u = jnp.maximum(m, gmax)
        a = jnp.exp(m - u)
        mt = bcum + u
        m_ref[pl.ds(b, 1), :] = mt[L - 1:L, :]
        g1, g2, g3 = _split3(g)
        g_rows = _mxu_transpose(eye, g1) + (_mxu_transpose(eye, g2) + _mxu_transpose(eye, g3))
        gates[b] = dict(u=u, a=a, em=jnp.exp(-mt), a_last=a[L - 1:L, :], wcol=jnp.exp(g - u[L - 1:L, :]),
                        g_rows=g_rows)

    for b, h in heads:
        gl = S_MI + h
        kh = k_ref[b, :, cols(h)]
        w_h = gates[b]["wcol"][:, gl:gl + 1]
        a_l = gates[b]["a_last"][:, gl:gl + 1]
        vw = (v_ref[b, :, cols(h)].astype(F32) * w_h).astype(BF16)
        c_ref[b, h] = a_l * c_ref[b, h] + _dot(_mxu_transpose(eye, vw).astype(BF16), kh)
        n_ref[b, h:h + 1, :] = (a_l * n_ref[b, h:h + 1, :]
                                + jnp.sum(kh.astype(F32) * w_h, axis=0, keepdims=True))

    for b, h in heads:
        gl = S_MI + h
        gb = gates[b]
        decay = jnp.exp(jnp.where(causal, gb["g_rows"][gl:gl + 1, :] - gb["u"][:, gl:gl + 1], NEG))
        s = qk[b, h] * decay
        a_h = gb["a"][:, gl:gl + 1]
        sb = s.astype(BF16)
        num = a_h * qc[b, h] + _dot(sb, v_ref[b, :, cols(h)])
        den = a_h * qn[b, h] + _dot(sb, ones[:L, :])
        hv = num / jnp.maximum(jnp.abs(den), gb["em"][:, gl:gl + 1])
        hn = hv * lax.rsqrt(_dot((hv * hv).astype(BF16), ones) * (1.0 / ML_DV) + EPS)
        o_ref[b, :, cols(h)] = (hn * g_ref[:, cols(h)] * mo_ref[b, :, cols(h)].astype(F32)).astype(BF16)


def _mlstm(mq, mk, mv, mo, small, c0, n0, m0, g_ml, L):
    bsz, seq, _ = mq.shape
    chunk = pl.BlockSpec((bsz, L, ML_W), lambda c: (0, c, 0))
    full = lambda shape: pl.BlockSpec(shape, lambda c: (0,) * len(shape))
    return pl.pallas_call(
        _mlstm_kernel,
        grid=(seq // L,),
        in_specs=[chunk, chunk, chunk, chunk,
                  pl.BlockSpec((bsz, L, V7X_LANES), lambda c: (0, c, 0)),
                  full(c0.shape), full(n0.shape), full(m0.shape), full((1, ML_W))],
        out_specs=[chunk, full(c0.shape), full(n0.shape), full(m0.shape)],
        out_shape=[jax.ShapeDtypeStruct((bsz, seq, ML_W), BF16),
                   jax.ShapeDtypeStruct(c0.shape, F32),
                   jax.ShapeDtypeStruct(n0.shape, F32),
                   jax.ShapeDtypeStruct(m0.shape, F32)],
        compiler_params=_params("arbitrary"),
        name="mlstm",
    )(mq, mk, mv, mo, small, c0, n0, m0, g_ml)


def _merge_kernel(ca_ref, cm_ref, x_ref, wa_ref, wm_ref, g_ref, wrh_ref, wrl_ref, br_ref,
                  y1_ref, xn_ref, ridx_ref, rgate_ref, cnt_ref):
    tm = x_ref.shape[0]

    @pl.when(pl.program_id(0) == 0)
    def _():
        cnt_ref[...] = jnp.zeros_like(cnt_ref)

    y1 = x_ref[...] + (_dot(ca_ref[...], wa_ref[...]) + _dot(cm_ref[...], wm_ref[...]))
    y1_ref[...] = y1
    xn = (y1 * lax.rsqrt(jnp.mean(y1 * y1, axis=-1, keepdims=True) + EPS)) * g_ref[...]
    xn_ref[...] = xn.reshape(xn_ref.shape)

    xh = xn.astype(BF16)
    xl = (xn - xh.astype(F32)).astype(BF16)
    logits = (_dot(xh, wrh_ref[...]) + (_dot(xl, wrh_ref[...]) + _dot(xh, wrl_ref[...]))) + br_ref[...]

    lane = lax.broadcasted_iota(jnp.int32, logits.shape, 1)
    vals, sel = [], []
    work = logits
    for _ in range(TOP_K):
        mx = jnp.max(work, axis=-1, keepdims=True)
        idx = jnp.min(jnp.where(work == mx, lane, V7X_LANES), axis=-1, keepdims=True)
        vals.append(mx)
        sel.append(idx)
        work = jnp.where(lane == idx, -jnp.inf, work)
    ex = [jnp.exp(v - vals[0]) for v in vals]
    tot = ex[0] + ex[1] + ex[2] + ex[3]

    onehot = [(lane == idx) for idx in sel]
    picked = jnp.where(onehot[0] | onehot[1] | onehot[2] | onehot[3], 1.0, 0.0)
    earlier = (lax.broadcasted_iota(jnp.int32, (tm, tm), 1) < lax.broadcasted_iota(jnp.int32, (tm, tm), 0))
    before = _dot(jnp.where(earlier, 1.0, 0.0).astype(BF16), picked.astype(BF16)) + cnt_ref[...]
    cnt_ref[...] = cnt_ref[...] + jnp.sum(picked, axis=0, keepdims=True)

    ridx = jnp.zeros(logits.shape, jnp.int32)
    rgate = jnp.zeros(logits.shape, F32)
    for k in range(TOP_K):
        rank = jnp.sum(jnp.where(onehot[k], before, 0.0), axis=-1, keepdims=True).astype(jnp.int32)
        ridx = jnp.where(lane == k, sel[k], ridx)
        ridx = jnp.where(lane == TOP_K + k, rank, ridx)
        rgate = jnp.where(lane == k, ex[k] / tot, rgate)
    ridx_ref[...] = ridx
    rgate_ref[...] = rgate


def _merge(cat_a, cat_m, x, w_a, w_m, g_ffn, wr_hi, wr_lo, br, tm):
    t, d = x.shape
    row = lambda c: pl.BlockSpec((tm, c), lambda i: (i, 0))
    const = lambda shape: pl.BlockSpec(shape, lambda i: (0,) * len(shape))
    return pl.pallas_call(
        _merge_kernel,
        grid=(t // tm,),
        in_specs=[row(FOX_W), row(ML_W), row(d), const(w_a.shape), const(w_m.shape), const((1, d)),
                  const(wr_hi.shape), const(wr_lo.shape), const((1, V7X_LANES))],
        out_specs=[row(d), pl.BlockSpec((tm,) + _row_tile(d), lambda i: (i, 0, 0)), row(V7X_LANES),
                   row(V7X_LANES), const((1, V7X_LANES))],
        out_shape=[jax.ShapeDtypeStruct((t, d), F32), jax.ShapeDtypeStruct((t,) + _row_tile(d), F32),
                   jax.ShapeDtypeStruct((t, V7X_LANES), jnp.int32),
                   jax.ShapeDtypeStruct((t, V7X_LANES), F32),
                   jax.ShapeDtypeStruct((1, V7X_LANES), F32)],
        compiler_params=_params("arbitrary"),
        name="merge_router",
    )(cat_a, cat_m, x, w_a, w_m, g_ffn, wr_hi, wr_lo, br)


def _row_tile(d):
    return (d // V7X_LANES, V7X_LANES)


def _row_copy(src_ref, src_row, dst_ref, dst_row, sem):
    return pltpu.make_async_copy(src_ref.at[pl.ds(src_row, 1)], dst_ref.at[pl.ds(dst_row, 1)], sem)


def _scatter_rows(pos_ref, x_ref, xs_ref, sem):
    tm = x_ref.shape[0]

    def issue(g, _):
        for u in range(DMA_UNROLL):
            r = g * DMA_UNROLL + u
            for k in range(TOP_K):
                _row_copy(x_ref, r, xs_ref, pos_ref[0, 0, TOP_K * r + k], sem).start(priority=k % 2)
        return 0

    def drain(g, _):
        for _ in range(DMA_UNROLL * TOP_K):
            _row_copy(x_ref, 0, xs_ref, 0, sem).wait()
        return 0

    lax.fori_loop(0, tm // DMA_UNROLL, issue, 0)
    lax.fori_loop(0, tm // DMA_UNROLL, drain, 0)


def _dispatch_kernel(fs_ref, fl_ref, pos_p_ref, x_p_ref, pos_s_ref, x_s_ref, xs_ref, zero_ref, sem, *, ntp, nts):
    i = pl.program_id(0)

    @pl.when(i < ntp)
    def _():
        _scatter_rows(pos_p_ref, x_p_ref, xs_ref, sem)

    @pl.when((i >= ntp) & (i < ntp + nts))
    def _():
        _scatter_rows(pos_s_ref, x_s_ref, xs_ref, sem)

    @pl.when(i == ntp + nts)
    def _():
        zero_ref[...] = jnp.zeros_like(zero_ref)
        full = zero_ref.shape[0]
        bits = [1 << p for p in range(full.bit_length() - 2, -1, -1)]

        def zeros_to(start, size):
            return pltpu.make_async_copy(zero_ref.at[pl.ds(0, size)], xs_ref.at[pl.ds(start, size)], sem)

        def per_run(e, _, wait):
            start, n = fs_ref[e], fl_ref[e]
            whole = n // full

            def whole_copy(q, _):
                c = zeros_to(start + q * full, full)
                c.wait() if wait else c.start()
                return 0

            lax.fori_loop(0, whole, whole_copy, 0)
            for p in bits:
                @pl.when((n & p) != 0)
                def _():
                    c = zeros_to(start + (n & ~(2 * p - 1)), p)
                    c.wait() if wait else c.start()
            return 0

        lax.fori_loop(0, fs_ref.shape[0], functools.partial(per_run, wait=False), 0)
        lax.fori_loop(0, fs_ref.shape[0], functools.partial(per_run, wait=True), 0)


def _dispatch(fill_start, fill_len, pos3_p, x_p, pos3_s, x_s, n_rows, tm_p, tm_s):
    tp, sub, lanes = x_p.shape
    ntp, nts = tp // tm_p, x_s.shape[0] // tm_s
    clamp_p = lambda i, fs, fl: (jnp.minimum(i, ntp - 1), 0, 0)
    clamp_s = lambda i, fs, fl: (jnp.clip(i - ntp, 0, nts - 1), 0, 0)
    grid_spec = pltpu.PrefetchScalarGridSpec(
        num_scalar_prefetch=2,
        grid=(ntp + nts + 1,),
        in_specs=[pl.BlockSpec((1, 1, TOP_K * tm_p), clamp_p, memory_space=pltpu.SMEM),
                  pl.BlockSpec((tm_p, sub, lanes), clamp_p),
                  pl.BlockSpec((1, 1, TOP_K * tm_s), clamp_s, memory_space=pltpu.SMEM),
                  pl.BlockSpec((tm_s, sub, lanes), clamp_s)],
        out_specs=pl.BlockSpec(memory_space=pl.ANY),
        scratch_shapes=[pltpu.VMEM((FFN_ROWS, sub, lanes), F32), pltpu.SemaphoreType.DMA(())],
    )
    return pl.pallas_call(
        functools.partial(_dispatch_kernel, ntp=ntp, nts=nts),
        grid_spec=grid_spec,
        out_shape=jax.ShapeDtypeStruct((n_rows, sub, lanes), F32),
        compiler_params=_params("arbitrary"),
        name="moe_dispatch",
    )(fill_start, fill_len, pos3_p, x_p, pos3_s, x_s)


def _ffn_kernel(te_ref, nu_ref, x_ref, wgu_ref, bgu_ref, wd_ref, bd_ref, y_ref, wgu16_ref, wd16_ref):
    i = pl.program_id(0)
    d_ff = wd_ref.shape[1]

    @pl.when((i == 0) | (te_ref[i] != te_ref[jnp.maximum(i - 1, 0)]))
    def _():
        wgu16_ref[...] = wgu_ref[0].astype(BF16)
        wd16_ref[...] = wd_ref[0].astype(BF16)

    @pl.when(i < nu_ref[0])
    def _():
        x = x_ref[...].reshape(x_ref.shape[0], -1)
        gu = _dot(x.astype(BF16), wgu16_ref[...]) + bgu_ref[0]
        gate = jnp.minimum(gu[:, :d_ff], SWIGLU_LIMIT)
        up = jnp.clip(gu[:, d_ff:], -SWIGLU_LIMIT, SWIGLU_LIMIT)
        h = (up + 1.0) * (gate * _sigmoid(SWIGLU_ALPHA * gate))
        y = _dot(h.astype(BF16), wd16_ref[...]) + bd_ref[0]
        y_ref[...] = y.reshape(y_ref.shape)

    @pl.when(i >= nu_ref[0])
    def _():
        y_ref[...] = jnp.zeros_like(y_ref)


def _ffn(tile_expert, n_used, xs, w_gu, b_gu, w_d, b_d, tm):
    r = xs.shape[0]
    d_ff, d = w_d.shape[1:]
    tiles = pl.BlockSpec((tm,) + _row_tile(d), lambda i, te, nu: (i, 0, 0))
    grid_spec = pltpu.PrefetchScalarGridSpec(
        num_scalar_prefetch=2,
        grid=(r // tm,),
        in_specs=[tiles,
                  pl.BlockSpec((1, d, 2 * d_ff), lambda i, te, nu: (te[i], 0, 0)),
                  pl.BlockSpec((1, 1, 2 * d_ff), lambda i, te, nu: (te[i], 0, 0)),
                  pl.BlockSpec((1, d_ff, d), lambda i, te, nu: (te[i], 0, 0)),
                  pl.BlockSpec((1, 1, d), lambda i, te, nu: (te[i], 0, 0))],
        out_specs=tiles,
        scratch_shapes=[pltpu.VMEM((d, 2 * d_ff), BF16), pltpu.VMEM((d_ff, d), BF16)],
    )
    return pl.pallas_call(
        _ffn_kernel,
        grid_spec=grid_spec,
        out_shape=jax.ShapeDtypeStruct(xs.shape, F32),
        compiler_params=_params("arbitrary"),
        name="moe_ffn",
    )(tile_expert, n_used, xs, w_gu, b_gu, w_d, b_d)


def _combine_kernel(pos_ref, pos_next_ref, y1_ref, gate_ref, gfin_ref, ys_ref, o_ref, buf_ref, sem):
    tm = y1_ref.shape[0]
    i = pl.program_id(0)
    slot = i % 2

    def gather(p_ref, s):
        def issue(g, _):
            for u in range(DMA_UNROLL):
                r = g * DMA_UNROLL + u
                for k in range(TOP_K):
                    _row_copy(ys_ref, p_ref[0, 0, TOP_K * r + k], buf_ref.at[s, k], r,
                              sem.at[s]).start(priority=k % 2)
            return 0

        lax.fori_loop(0, tm // DMA_UNROLL, issue, 0)

    @pl.when(i == 0)
    def _():
        gather(pos_ref, 0)

    @pl.when(i + 1 < pl.num_programs(0))
    def _():
        gather(pos_next_ref, 1 - slot)

    def drain(g, _):
        for _ in range(DMA_UNROLL * TOP_K):
            _row_copy(ys_ref, 0, buf_ref.at[slot, 0], 0, sem.at[slot]).wait()
        return 0

    lax.fori_loop(0, tm // DMA_UNROLL, drain, 0)

    gate = gate_ref[...]
    moe = gate[:, 0:1] * buf_ref[slot, 0].reshape(tm, -1)
    for k in range(1, TOP_K):
        moe = moe + gate[:, k:k + 1] * buf_ref[slot, k].reshape(tm, -1)
    y = y1_ref[...] + moe
    o_ref[...] = (y * lax.rsqrt(jnp.mean(y * y, axis=-1, keepdims=True) + EPS)) * gfin_ref[...]


def _combine(pos3, y1, rgate, g_final, ys, tm):
    t, d = y1.shape
    nt = t // tm
    pos_spec = lambda f: pl.BlockSpec((1, 1, TOP_K * tm), f, memory_space=pltpu.SMEM)
    return pl.pallas_call(
        _combine_kernel,
        grid=(nt,),
        in_specs=[pos_spec(lambda i: (i, 0, 0)), pos_spec(lambda i: (jnp.minimum(i + 1, nt - 1), 0, 0)),
                  pl.BlockSpec((tm, d), lambda i: (i, 0)),
                  pl.BlockSpec((tm, V7X_LANES), lambda i: (i, 0)),
                  pl.BlockSpec((1, d), lambda i: (0, 0)),
                  pl.BlockSpec(memory_space=pl.ANY)],
        out_specs=pl.BlockSpec((tm, d), lambda i: (i, 0)),
        out_shape=jax.ShapeDtypeStruct((t, d), F32),
        scratch_shapes=[pltpu.VMEM((2, TOP_K, tm) + _row_tile(d), F32), pltpu.SemaphoreType.DMA((2,))],
        compiler_params=_params("arbitrary"),
        name="moe_combine",
    )(pos3, pos3, y1, rgate, g_final, ys)


def _mixer(x, hist, fox_cache, ml_state, w, tm, tq, tk, chunk):
    bsz, seq, _ = x.shape
    hist8 = jnp.pad(hist, ((0, 0), (V7X_SUBLANES - (CONV_W - 1), 0), (0, 0)))
    proj = _project(x, hist8, w["g_mix"], w["wbig"], w["wsm"], w["bsm"], w["conv_w"], w["conv_b"], tm,
                    aug=fox_cache is None)

    if fox_cache is None:
        q_offset = 0
        assert seq % tk == 0
        (fq, fk32, fv32, ka0, ka1, va0, va1, small, flogf, mq, mk, mv, mo, conv_new) = proj
    else:
        (fq, fk32, fv32, fk16, fv16, small, flogf, mq, mk, mv, mo, conv_new) = proj
        ck_c, cv_c, clogf_c = fox_cache
        q_offset = ck_c.shape[1]
        assert q_offset + seq <= tk
        clogf_c = jnp.pad(clogf_c, ((0, 0), (0, 0), (S_FF, V7X_LANES - S_FF - FOX_HEADS)))
        lf_all = jnp.pad(jnp.concatenate([clogf_c, small], axis=1), ((0, 0), (0, tk - q_offset - seq), (0, 0)))
        ka0, ka1, va0, va1 = _fox_prep(lf_all, [ck_c.reshape(bsz, q_offset, FOX_W), fk16],
                                       [cv_c.reshape(bsz, q_offset, FOX_W), fv16], tk)
    cat_a = _fox(fq, ka0, ka1, va0, va1, w["g_fox"], q_offset, tq, tk)

    c0, n0, m0 = ml_state
    m0 = jnp.pad(m0, ((0, 0), (S_MI, V7X_LANES - S_MI - ML_HEADS)))
    cat_m, c_new, n_new, m_new = _mlstm(mq, mk, mv, mo, small, c0, n0, m0, w["g_ml"], chunk)
    m_new = m_new[:, S_MI:S_MI + ML_HEADS]

    states = (fk32.reshape(bsz, seq, FOX_HEADS, FOX_HD), fv32.reshape(bsz, seq, FOX_HEADS, FOX_HD), flogf,
              c_new, n_new, m_new, conv_new)
    return cat_a, cat_m, states


def kernel(x_prompt, x_sample, cache_fox_k, cache_fox_v, cache_fox_logf, state_mlstm_C, state_mlstm_n,
           state_mlstm_m, state_mlstm_conv, norm_mix_g, w_in, b_fox_f, conv_w, conv_b, b_ml_i, b_ml_f,
           g_fox, g_ml, w_out, norm_ffn_g, w_router, b_router, w_gate_up, b_gate_up, w_down, b_down,
           norm_final_g):
    depth = w_in.shape[0]
    assert depth == 1, "the final norm is fused into the last layer's combine; only depth 1 is wired up"
    bp, sp, d = x_prompt.shape
    bs, ss, _ = x_sample.shape
    n_exp = w_router.shape[-1]
    d_ff = w_down.shape[2]
    yp, ys = x_prompt, x_sample
    p_st, s_st = [], []

    for l in range(depth):
        wl = w_in[l]
        w = {
            "g_mix": norm_mix_g[l][None, :],
            "wbig": jnp.concatenate([wl[:, O_FQ:O_FF], wl[:, O_MQ:O_MI]], axis=1).astype(BF16),
            "wsm": jnp.pad(jnp.concatenate([wl[:, O_FF:O_MQ], wl[:, O_MI:P_IN]], axis=1),
                           ((0, 0), (0, V7X_LANES - S_END))).astype(BF16),
            "bsm": jnp.pad(jnp.concatenate([b_fox_f[l], b_ml_i[l], b_ml_f[l]]), (0, V7X_LANES - S_END))[None, :],
            "conv_w": conv_w[l],
            "conv_b": conv_b[l][None, :],
            "g_fox": g_fox[l][None, :],
            "g_ml": g_ml[l][None, :],
        }
        w_a = w_out[l][:FOX_W].astype(BF16)
        w_m = w_out[l][FOX_W:].astype(BF16)
        g_ffn = norm_ffn_g[l][None, :]
        wr = jnp.pad(w_router[l], ((0, 0), (0, V7X_LANES - n_exp)))
        wr_hi = wr.astype(BF16)
        wr_lo = (wr - wr_hi.astype(F32)).astype(BF16)
        br = jnp.pad(b_router[l], (0, V7X_LANES - n_exp), constant_values=NEG)[None, :]

        zeros_state = (jnp.zeros((bp, ML_HEADS, ML_DV, ML_DK), F32), jnp.zeros((bp, ML_HEADS, ML_DK), F32),
                       jnp.zeros((bp, ML_HEADS), F32))
        cat_a_p, cat_m_p, st_p = _mixer(yp, jnp.zeros((bp, CONV_W - 1, QK_W), F32), None, zeros_state, w,
                                        min(PROJ_ROWS, sp), min(FOX_QBLOCK, sp), min(FOX_BLOCK, sp), CHUNK)
        p_st.append(st_p)
        past = cache_fox_k.shape[2]
        tk_s = -(-(past + ss) // V7X_LANES) * V7X_LANES
        cat_a_s, cat_m_s, st_s = _mixer(
            ys, state_mlstm_conv[l], (cache_fox_k[l], cache_fox_v[l], cache_fox_logf[l]),
            (state_mlstm_C[l].astype(F32), state_mlstm_n[l].astype(F32), state_mlstm_m[l].astype(F32)),
            w, ss, ss, tk_s, ss)
        s_st.append(st_s)

        tp, ts = bp * sp, bs * ss
        y1_p, xn_p, ridx_p, rgate_p, cnt_p = _merge(
            cat_a_p.reshape(tp, FOX_W), cat_m_p.reshape(tp, ML_W), yp.reshape(tp, d), w_a, w_m, g_ffn,
            wr_hi, wr_lo, br, min(MERGE_ROWS, tp))
        y1_s, xn_s, ridx_s, rgate_s, cnt_s = _merge(
            cat_a_s.reshape(ts, FOX_W), cat_m_s.reshape(ts, ML_W), ys.reshape(ts, d), w_a, w_m, g_ffn,
            wr_hi, wr_lo, br, min(MERGE_ROWS, ts))

        cnt_p = cnt_p[0, :n_exp].astype(jnp.int32)
        cnt_s = cnt_s[0, :n_exp].astype(jnp.int32)
        seg_rows = -(-(cnt_p + cnt_s) // FFN_ROWS) * FFN_ROWS
        seg_end = jnp.cumsum(seg_rows)
        seg_start = seg_end - seg_rows
        n_rows = -(-((tp + ts) * TOP_K + n_exp * (FFN_ROWS - 1)) // FFN_ROWS) * FFN_ROWS
        n_tiles = n_rows // FFN_ROWS
        tile_row0 = jnp.arange(n_tiles, dtype=jnp.int32) * FFN_ROWS
        tile_expert = jnp.minimum(jnp.sum((seg_end[None, :] <= tile_row0[:, None]).astype(jnp.int32), axis=1),
                                  n_exp - 1)
        n_used = (seg_end[-1:] // FFN_ROWS).astype(jnp.int32)
        e_p, rank_p = ridx_p[:, :TOP_K], ridx_p[:, TOP_K:2 * TOP_K]
        e_s, rank_s = ridx_s[:, :TOP_K], ridx_s[:, TOP_K:2 * TOP_K]
        pos_p = seg_start[e_p] + rank_p
        pos_s = seg_start[e_s] + cnt_p[e_s] + rank_s
        tm_p, tm_s = min(MOE_ROWS, tp), min(MOE_ROWS, ts)
        pos3_p = pos_p.reshape(tp // tm_p, 1, TOP_K * tm_p)
        pos3_s = pos_s.reshape(ts // tm_s, 1, TOP_K * tm_s)

        cnt = cnt_p + cnt_s
        fill_start = jnp.concatenate([seg_start + cnt, seg_end[-1:]]).astype(jnp.int32)
        fill_len = jnp.concatenate([seg_rows - cnt, n_rows - seg_end[-1:]]).astype(jnp.int32)
        xs = _dispatch(fill_start, fill_len, pos3_p, xn_p, pos3_s, xn_s, n_rows, tm_p, tm_s)
        ysort = _ffn(tile_expert, n_used, xs, w_gate_up[l], b_gate_up[l].reshape(n_exp, 1, 2 * d_ff),
                     w_down[l], b_down[l].reshape(n_exp, 1, d), FFN_ROWS)

        g_fin = norm_final_g[None, :]
        yp = _combine(pos3_p, y1_p, rgate_p, g_fin, ysort, tm_p).reshape(bp, sp, d)
        ys = _combine(pos3_s, y1_s, rgate_s, g_fin, ysort, tm_s).reshape(bs, ss, d)

    p_out = tuple(jnp.stack(a) for a in zip(*p_st))
    s_out = tuple(jnp.stack(a) for a in zip(*s_st))
    return (yp, ys) + p_out + s_out
```

```python
import functools
import math

import numpy as np

import jax
import jax.numpy as jnp
from jax import lax
from jax.experimental import pallas as pl
from jax.experimental.pallas import tpu as pltpu

F32 = jnp.float32
BF16 = jnp.bfloat16

FOX_HEADS = 8
FOX_HD = 64
FOX_W = FOX_HEADS * FOX_HD
ML_HEADS = 4
ML_DK = 128
ML_DV = 128
ML_W = ML_HEADS * ML_DV
QK_W = 2 * ML_HEADS * ML_DK
CONV_W = 4
CHUNK = 64
TOP_K = 4
SWIGLU_LIMIT = 7.0
SWIGLU_ALPHA = 1.702
EPS = 1e-6
NEG = -1e30
LOG2E = math.log2(math.e)

O_FQ = 0
O_FK = O_FQ + FOX_W
O_FV = O_FK + FOX_W
O_FF = O_FV + FOX_W
O_MQ = O_FF + FOX_HEADS
O_MK = O_MQ + ML_HEADS * ML_DK
O_MV = O_MK + ML_HEADS * ML_DK
O_MO = O_MV + ML_W
O_MI = O_MO + ML_W
O_MF = O_MI + ML_HEADS
P_IN = O_MF + ML_HEADS

B_FQ, B_FK, B_FV, B_QK, B_MV, B_MO, B_END = 0, 512, 1024, 1536, 2560, 3072, 3584
S_FF, S_MI, S_MF, S_END = 0, 8, 12, 16
BIAS_TERMS = 3

V7X_LANES = 128
V7X_SUBLANES = 8
V7X_VMEM_LIMIT_BYTES = 56 * 1024 * 1024

PROJ_ROWS = 512
FOX_BLOCK = 512
FOX_QBLOCK = 1024
FOX_SUB = 512
FOX_UNROLLS = (4, 2, 1)
MERGE_ROWS = 512
MOE_ROWS = 512
FFN_ROWS = 512
DMA_UNROLL = 8


def _params(*semantics):
    return pltpu.CompilerParams(dimension_semantics=semantics, vmem_limit_bytes=V7X_VMEM_LIMIT_BYTES)


def _dot(a, b):
    return jnp.dot(a, b, preferred_element_type=F32)


def _dot_nt(a, b):
    return lax.dot_general(a, b, (((1,), (1,)), ((), ())), preferred_element_type=F32)


def _mxu_transpose(eye, x):
    return _dot_nt(eye, x)


def _sigmoid(x):
    return 1.0 / (1.0 + jnp.exp(-x))


def _log_sigmoid(x):
    return jnp.minimum(x, 0.0) - jnp.log1p(jnp.exp(-jnp.abs(x)))


def _split3(x):
    t1 = x.astype(BF16)
    r1 = x - t1.astype(F32)
    t2 = r1.astype(BF16)
    t3 = (r1 - t2.astype(F32)).astype(BF16)
    return t1, t2, t3


def _lower_tri(n):
    r = lax.broadcasted_iota(jnp.int32, (n, n), 0)
    c = lax.broadcasted_iota(jnp.int32, (n, n), 1)
    return c <= r


def _proj_kernel(*refs, aug):
    x_ref, g_ref, wbig_ref, wsm_ref, bsm_ref, cw_ref, cb_ref, hist_ref = refs[:8]
    rest = refs[8:]
    if aug:
        place_ref, rest = rest[0], rest[1:]
        (fq_ref, fk32_ref, fv32_ref, ka0_ref, ka1_ref, va0_ref, va1_ref, sm_ref, flogf_ref, mq_ref, mk_ref,
         mv_ref, mo_ref, cnew_ref, halo_ref, carry_ref) = rest
    else:
        (fq_ref, fk32_ref, fv32_ref, fk16_ref, fv16_ref, sm_ref, flogf_ref, mq_ref, mk_ref, mv_ref, mo_ref,
         cnew_ref, halo_ref) = rest
    tm = x_ref.shape[1]

    @pl.when(pl.program_id(1) == 0)
    def _():
        halo_ref[...] = hist_ref[0]
        if aug:
            carry_ref[...] = jnp.zeros_like(carry_ref)

    x = x_ref[0]
    xn = (x * lax.rsqrt(jnp.mean(x * x, axis=-1, keepdims=True) + EPS)) * g_ref[...]
    xb = xn.astype(BF16)
    zs = _dot(xb, wsm_ref[...]) + bsm_ref[...]
    z = _dot(xb, wbig_ref[...])

    fq_ref[0] = (z[:, B_FQ:B_FQ + FOX_W] * (FOX_HD ** -0.5 * LOG2E)).astype(BF16)
    fk = z[:, B_FK:B_FK + FOX_W]
    fv = z[:, B_FV:B_FV + FOX_W]
    fk32_ref[0] = fk
    fv32_ref[0] = fv
    mv_ref[0] = z[:, B_MV:B_MV + ML_W].astype(BF16)
    mo_ref[0] = _sigmoid(z[:, B_MO:B_MO + ML_W]).astype(BF16)

    lane = lax.broadcasted_iota(jnp.int32, zs.shape, 1)
    is_forget = (lane < S_MI) | ((lane >= S_MF) & (lane < S_END))
    sm = jnp.where(is_forget, _log_sigmoid(zs), zs)
    sm_ref[0] = sm
    flogf_ref[0] = sm[:, S_FF:S_MI]

    if aug:
        rows = lax.broadcasted_iota(jnp.int32, sm.shape, 0)
        c = sm
        sh = 1
        while sh < tm:
            c = c + jnp.where(rows >= sh, pltpu.roll(c, sh, axis=0), 0.0)
            sh *= 2
        c = c + carry_ref[...]
        carry_ref[...] = c[tm - 1:tm, :]
        b1, b2, b3 = _split3(c * (-LOG2E))
        kb = _dot(b1, place_ref[0]) + (_dot(b2, place_ref[1]) + _dot(b3, place_ref[2]))
        first_head = lax.broadcasted_iota(jnp.int32, fk.shape, 1) % V7X_LANES < FOX_HD
        ka0_ref[0] = jnp.where(first_head, fk, kb).astype(BF16)
        ka1_ref[0] = jnp.where(first_head, kb, fk).astype(BF16)
        va0_ref[0] = jnp.where(first_head, fv, 1.0).astype(BF16)
        va1_ref[0] = jnp.where(first_head, 1.0, fv).astype(BF16)
    else:
        fk16_ref[0] = fk.astype(BF16)
        fv16_ref[0] = fv.astype(BF16)

    u = z[:, B_QK:B_QK + QK_W]
    up = jnp.concatenate([halo_ref[...], u], axis=0)
    n = tm + V7X_SUBLANES
    first = V7X_SUBLANES - (CONV_W - 1)
    y = cb_ref[...] + cw_ref[0:1, :] * pltpu.roll(up, n - first, axis=0)[:tm]
    for j in range(1, CONV_W - 1):
        y = y + cw_ref[j:j + 1, :] * pltpu.roll(up, n - (first + j), axis=0)[:tm]
    y = y + cw_ref[CONV_W - 1:CONV_W, :] * u
    qk = y * _sigmoid(y)
    mq_ref[0] = qk[:, :QK_W // 2].astype(BF16)
    mk_ref[0] = (qk[:, QK_W // 2:] * (ML_DK ** -0.5)).astype(BF16)

    halo_ref[...] = u[tm - V7X_SUBLANES:, :]
    cnew_ref[0] = halo_ref[first:, :]


def _project(x, hist8, g, wbig, wsm, bsm, cw, cb, tm, aug):
    bsz, seq, d = x.shape
    grid = (bsz, seq // tm)
    row = lambda c: pl.BlockSpec((1, tm, c), lambda b, s: (b, s, 0))
    const = lambda shape: pl.BlockSpec(shape, lambda b, s: (0,) * len(shape))
    n_kv = 4 if aug else 2
    outs = ([(FOX_W, BF16), (FOX_W, F32), (FOX_W, F32)] + [(FOX_W, BF16)] * n_kv
            + [(V7X_LANES, F32), (FOX_HEADS, F32), (ML_W, BF16), (ML_W, BF16), (ML_W, BF16), (ML_W, BF16)])
    out_shape = [jax.ShapeDtypeStruct((bsz, seq, c), dt) for c, dt in outs]
    out_specs = [row(c) for c, _ in outs]
    out_shape.append(jax.ShapeDtypeStruct((bsz, CONV_W - 1, QK_W), F32))
    out_specs.append(pl.BlockSpec((1, CONV_W - 1, QK_W), lambda b, s: (b, 0, 0)))
    in_specs = [row(d), const((1, d)), const(wbig.shape), const(wsm.shape), const((1, V7X_LANES)),
                const((CONV_W, QK_W)), const((1, QK_W)),
                pl.BlockSpec((1, V7X_SUBLANES, QK_W), lambda b, s: (b, 0, 0))]
    args = [x, g, wbig, wsm, bsm, cw, cb, hist8]
    scratch = [pltpu.VMEM((V7X_SUBLANES, QK_W), F32)]
    if aug:
        place = _bias_placement()
        in_specs.append(const(place.shape))
        args.append(place)
        scratch.append(pltpu.VMEM((1, V7X_LANES), F32))
    return pl.pallas_call(
        functools.partial(_proj_kernel, aug=aug),
        grid=grid,
        in_specs=in_specs,
        out_specs=out_specs,
        out_shape=out_shape,
        scratch_shapes=scratch,
        compiler_params=_params("arbitrary", "arbitrary"),
        name="proj",
    )(*args)


def _bias_lane(head, term):
    return (head // 2) * V7X_LANES + (FOX_HD if head % 2 == 0 else 0) + term


def _bias_placement():
    place = np.zeros((BIAS_TERMS, V7X_LANES, FOX_W), np.float32)
    for h in range(FOX_HEADS):
        for t in range(BIAS_TERMS):
            place[t, S_FF + h, _bias_lane(h, t)] = 1.0
    return jnp.asarray(place, BF16)


def _fox_prep_kernel(lf_ref, *refs, n_parts):
    k_refs, v_refs = refs[:n_parts], refs[n_parts:2 * n_parts]
    place_ref, ka0_ref, ka1_ref, va0_ref, va1_ref, carry_ref = refs[2 * n_parts:]
    tm = lf_ref.shape[1]

    def rows(part_refs):
        parts = [r[0].astype(F32) for r in part_refs]
        missing = tm - sum(p.shape[0] for p in parts)
        if missing:
            parts.append(jnp.zeros((missing, FOX_W), F32))
        return parts[0] if len(parts) == 1 else jnp.concatenate(parts, axis=0)

    @pl.when(pl.program_id(1) == 0)
    def _():
        carry_ref[...] = jnp.zeros_like(carry_ref)

    tri = jnp.where(_lower_tri(tm), 1.0, 0.0).astype(BF16)
    t1, t2, t3 = _split3(lf_ref[0])
    c = carry_ref[...] + (_dot(tri, t1) + (_dot(tri, t2) + _dot(tri, t3)))
    carry_ref[...] = c[tm - 1:tm, :]
    b1, b2, b3 = _split3(c * (-LOG2E))
    kb = _dot(b1, place_ref[0]) + (_dot(b2, place_ref[1]) + _dot(b3, place_ref[2]))

    k = rows(k_refs)
    v = rows(v_refs)
    first_head = lax.broadcasted_iota(jnp.int32, k.shape, 1) % V7X_LANES < FOX_HD
    ka0_ref[0] = jnp.where(first_head, k, kb).astype(BF16)
    ka1_ref[0] = jnp.where(first_head, kb, k).astype(BF16)
    va0_ref[0] = jnp.where(first_head, v, 1.0).astype(BF16)
    va1_ref[0] = jnp.where(first_head, 1.0, v).astype(BF16)


def _fox_prep(lf, k_parts, v_parts, tm):
    bsz, lk, _ = lf.shape
    place = _bias_placement()
    wide = pl.BlockSpec((1, tm, FOX_W), lambda b, s: (b, s, 0))
    if len(k_parts) == 1:
        part_specs = [wide, wide]
    else:
        assert lk == tm, "several key/value parts are only stitched inside a single tile"
        part_specs = [pl.BlockSpec((1,) + p.shape[1:], lambda b, s: (b, 0, 0)) for p in k_parts + v_parts]
    sds = jax.ShapeDtypeStruct((bsz, lk, FOX_W), BF16)
    return pl.pallas_call(
        functools.partial(_fox_prep_kernel, n_parts=len(k_parts)),
        grid=(bsz, lk // tm),
        in_specs=[pl.BlockSpec((1, tm, V7X_LANES), lambda b, s: (b, s, 0))] + part_specs
        + [pl.BlockSpec(place.shape, lambda b, s: (0, 0, 0))],
        out_specs=[wide, wide, wide, wide],
        out_shape=[sds, sds, sds, sds],
        scratch_shapes=[pltpu.VMEM((1, V7X_LANES), F32)],
        compiler_params=_params("arbitrary", "arbitrary"),
        name="fox_prep",
    )(lf, *k_parts, *v_parts, place)


def _fox_kernel(q_ref, ka0_ref, ka1_ref, va0_ref, va1_ref, g_ref, o_ref, *, tq, tk, sub, q_offset):
    ts = min(tq, sub)
    n_sub = tq // ts
    diag_blocks = max(ts // tk, 1)
    k_refs = (ka0_ref, ka1_ref)
    v_refs = (va0_ref, va1_ref)
    lane = lax.broadcasted_iota(jnp.int32, (ts, V7X_LANES), 1)
    first_head = lane < FOX_HD
    q_lo = q_offset + pl.program_id(2) * tq
    n_common = (q_lo + 1) // tk
    single_block = ka0_ref.shape[1] == tk
    even_common = q_offset % (2 * tk) == 0 and tq % (2 * tk) == 0

    def chain_q(r, hh):
        q = q_ref[0, r * ts:(r + 1) * ts, :].astype(F32)
        if hh == 0:
            return jnp.where(first_head, q, jnp.where(lane < FOX_HD + BIAS_TERMS, 1.0, 0.0)).astype(BF16)
        return jnp.where(first_head, jnp.where(lane < BIAS_TERMS, 1.0, 0.0), q).astype(BF16)

    chains = [(r, hh) for r in range(n_sub) for hh in range(2)]
    qs = [chain_q(r, hh) for r, hh in chains]

    def scores(c, j):
        return _dot_nt(qs[c], k_refs[chains[c][1]][0, pl.ds(pl.multiple_of(j * tk, tk), tk), :])

    def update(state, c, j, s, masked):
        r, hh = chains[c]
        m, acc = state
        start = pl.multiple_of(j * tk, tk)
        if masked:
            qpos = q_lo + r * ts + lax.broadcasted_iota(jnp.int32, s.shape, 0)
            kpos = start + lax.broadcasted_iota(jnp.int32, s.shape, 1)
            s = jnp.where(kpos <= qpos, s, NEG)
        m_new = jnp.maximum(m, jnp.max(s, axis=-1, keepdims=True))
        p = jnp.exp2(s - m_new).astype(BF16)
        acc = jnp.exp2(m - m_new) * acc + _dot(p, v_refs[hh][0, pl.ds(start, tk), :])
        return m_new, acc

    def step(state, c, j, masked):
        return update(state, c, j, scores(c, j), masked)

    def common(j, states):
        ss = [scores(c, j) for c in range(len(chains))]
        return tuple(update(st, c, j, ss[c], False) for c, st in enumerate(states))

    def common_run(t, states, nblk):
        states = list(states)
        cur = [scores(c, nblk * t) for c in range(len(chains))]
        for e in range(nblk):
            nxt = []
            for c in range(len(chains)):
                states[c] = update(states[c], c, nblk * t + e, cur[c], False)
                if e + 1 < nblk:
                    nxt.append(scores(c, nblk * t + e + 1))
            cur = nxt
        return tuple(states)

    init = (jnp.full((ts, 1), NEG, F32), jnp.zeros((ts, V7X_LANES), F32))
    states = (init,) * len(chains)
    if not single_block:
        done = 0
        for nblk in FOX_UNROLLS:
            if nblk == 1 and even_common:
                continue
            trips = n_common // nblk
            states = lax.fori_loop(done // nblk, trips, functools.partial(common_run, nblk=nblk), states)
            done = trips * nblk
    states = list(states)
    for c, (r, _) in enumerate(chains):
        for e in range(r * diag_blocks):
            states[c] = step(states[c], c, n_common + e, False)
        for e in range(diag_blocks):
            states[c] = step(states[c], c, n_common + r * diag_blocks + e, True)

    for r in range(n_sub):
        out = jnp.zeros((ts, V7X_LANES), F32)
        for hh in range(2):
            acc = states[chains.index((r, hh))][1]
            own = first_head if hh == 0 else jnp.logical_not(first_head)
            denom_lane = FOX_HD if hh == 0 else 0
            o = jnp.where(own, acc / acc[:, denom_lane:denom_lane + 1], 0.0)
            out = out + o * lax.rsqrt(jnp.sum(o * o, axis=-1, keepdims=True) * (1.0 / FOX_HD) + EPS)
        o_ref[0, r * ts:(r + 1) * ts, :] = (out * g_ref[...]).astype(BF16)


def _fox(q, ka0, ka1, va0, va1, g_fox, q_offset, tq, tk):
    bsz, lq, _ = q.shape
    lk = ka0.shape[1]
    assert (q_offset % tk == 0 and tq % tk == 0) or (lq == tq <= tk and lk == tk), (q_offset, tq, tk, lq, lk)
    pairs = FOX_HEADS // 2
    grid = (bsz, pairs, lq // tq)
    kv_spec = pl.BlockSpec((1, lk, V7X_LANES), lambda b, p, i: (b, 0, p))
    q_spec = pl.BlockSpec((1, tq, V7X_LANES), lambda b, p, i: (b, i, p))
    return pl.pallas_call(
        functools.partial(_fox_kernel, tq=tq, tk=tk, sub=FOX_SUB, q_offset=q_offset),
        grid=grid,
        in_specs=[q_spec, kv_spec, kv_spec, kv_spec, kv_spec,
                  pl.BlockSpec((1, V7X_LANES), lambda b, p, i: (0, p))],
        out_specs=q_spec,
        out_shape=jax.ShapeDtypeStruct((bsz, lq, FOX_W), BF16),
        compiler_params=_params("arbitrary", "arbitrary", "arbitrary"),
        name="fox_attention",
    )(q, ka0, ka1, va0, va1, g_fox)


def _mlstm_kernel(q_ref, k_ref, v_ref, mo_ref, sm_ref, c0_ref, n0_ref, m0_ref, g_ref,
                  o_ref, c_ref, n_ref, m_ref):
    bsz, L, _ = q_ref.shape

    @pl.when(pl.program_id(0) == 0)
    def _():
        c_ref[...] = c0_ref[...]
        n_ref[...] = n0_ref[...]
        m_ref[...] = m0_ref[...]

    causal = _lower_tri(L)
    tri = jnp.where(causal, 1.0, 0.0).astype(BF16)
    eye = jnp.where(lax.broadcasted_iota(jnp.int32, (V7X_LANES, V7X_LANES), 0)
                    == lax.broadcasted_iota(jnp.int32, (V7X_LANES, V7X_LANES), 1), 1.0, 0.0).astype(BF16)
    rows = lax.broadcasted_iota(jnp.int32, (L, V7X_LANES), 0)
    lanes = lax.broadcasted_iota(jnp.int32, (L, V7X_LANES), 1)
    gate_lanes = (lanes >= S_MI) & (lanes < S_END)

    heads = [(b, h) for b in range(bsz) for h in range(ML_HEADS)]
    cols = lambda h: slice(h * ML_DK, (h + 1) * ML_DK)

    sms, bcums = {}, {}
    for b in range(bsz):
        sms[b] = jnp.where(gate_lanes, sm_ref[b], 0.0)
        t1, t2, t3 = _split3(sms[b])
        bcums[b] = _dot(tri, t1) + (_dot(tri, t2) + _dot(tri, t3))

    ones = jnp.ones((V7X_LANES, V7X_LANES), BF16)
    qk, qc, qn = {}, {}, {}
    for b, h in heads:
        qh = q_ref[b, :, cols(h)]
        qk[b, h] = _dot_nt(qh, k_ref[b, :, cols(h)])
        qc[b, h] = _dot_nt(qh, c_ref[b, h].astype(BF16))
        qn[b, h] = _dot((qh.astype(F32) * n_ref[b, h:h + 1, :]).astype(BF16), ones)

    gates = {}
    for b in range(bsz):
        sm = sms[b]
        bcum = pltpu.roll(bcums[b], V7X_LANES - (S_MF - S_MI), axis=1)
        g = sm - bcum
        gmax = g
        sh = 1
        while sh < L:
            gmax = jnp.maximum(gmax, jnp.where(rows >= sh, pltpu.roll(gmax, sh, axis=0), -jnp.inf))
            sh *= 2
        m = m_ref[pl.ds(b, 1), :]
        u = jnp.maximum(m, gmax)
        a = jnp.exp(m - u)
        mt = bcum + u
        m_ref[pl.ds(b, 1), :] = mt[L - 1:L, :]
        g1, g2, g3 = _split3(g)
        g_rows = _mxu_transpose(eye, g1) + (_mxu_transpose(eye, g2) + _mxu_transpose(eye, g3))
        gates[b] = dict(u=u, a=a, em=jnp.exp(-mt), a_last=a[L - 1:L, :], wcol=jnp.exp(g - u[L - 1:L, :]),
                        g_rows=g_rows)

    for b, h in heads:
        gl = S_MI + h
        kh = k_ref[b, :, cols(h)]
        w_h = gates[b]["wcol"][:, gl:gl + 1]
        a_l = gates[b]["a_last"][:, gl:gl + 1]
        vw = (v_ref[b, :, cols(h)].astype(F32) * w_h).astype(BF16)
        c_ref[b, h] = a_l * c_ref[b, h] + _dot(_mxu_transpose(eye, vw).astype(BF16), kh)
        n_ref[b, h:h + 1, :] = (a_l * n_ref[b, h:h + 1, :]
                                + jnp.sum(kh.astype(F32) * w_h, axis=0, keepdims=True))

    for b, h in heads:
        gl = S_MI + h
        gb = gates[b]
        decay = jnp.exp(jnp.where(causal, gb["g_rows"][gl:gl + 1, :] - gb["u"][:, gl:gl + 1], NEG))
        s = qk[b, h] * decay
        a_h = gb["a"][:, gl:gl + 1]
        sb = s.astype(BF16)
        num = a_h * qc[b, h] + _dot(sb, v_ref[b, :, cols(h)])
        den = a_h * qn[b, h] + _dot(sb, ones[:L, :])
        hv = num / jnp.maximum(jnp.abs(den), gb["em"][:, gl:gl + 1])
        hn = hv * lax.rsqrt(_dot((hv * hv).astype(BF16), ones) * (1.0 / ML_DV) + EPS)
        o_ref[b, :, cols(h)] = (hn * g_ref[:, cols(h)] * mo_ref[b, :, cols(h)].astype(F32)).astype(BF16)


def _mlstm(mq, mk, mv, mo, small, c0, n0, m0, g_ml, L):
    bsz, seq, _ = mq.shape
    chunk = pl.BlockSpec((bsz, L, ML_W), lambda c: (0, c, 0))
    full = lambda shape: pl.BlockSpec(shape, lambda c: (0,) * len(shape))
    return pl.pallas_call(
        _mlstm_kernel,
        grid=(seq // L,),
        in_specs=[chunk, chunk, chunk, chunk,
                  pl.BlockSpec((bsz, L, V7X_LANES), lambda c: (0, c, 0)),
                  full(c0.shape), full(n0.shape), full(m0.shape), full((1, ML_W))],
        out_specs=[chunk, full(c0.shape), full(n0.shape), full(m0.shape)],
        out_shape=[jax.ShapeDtypeStruct((bsz, seq, ML_W), BF16),
                   jax.ShapeDtypeStruct(c0.shape, F32),
                   jax.ShapeDtypeStruct(n0.shape, F32),
                   jax.ShapeDtypeStruct(m0.shape, F32)],
        compiler_params=_params("arbitrary"),
        name="mlstm",
    )(mq, mk, mv, mo, small, c0, n0, m0, g_ml)


def _merge_kernel(ca_ref, cm_ref, x_ref, wa_ref, wm_ref, g_ref, wrh_ref, wrl_ref, br_ref,
                  y1_ref, xn_ref, ridx_ref, rgate_ref, cnt_ref):
    tm = x_ref.shape[0]

    @pl.when(pl.program_id(0) == 0)
    def _():
        cnt_ref[...] = jnp.zeros_like(cnt_ref)

    y1 = x_ref[...] + (_dot(ca_ref[...], wa_ref[...]) + _dot(cm_ref[...], wm_ref[...]))
    y1_ref[...] = y1
    xn = (y1 * lax.rsqrt(jnp.mean(y1 * y1, axis=-1, keepdims=True) + EPS)) * g_ref[...]
    xn_ref[...] = xn.reshape(xn_ref.shape)

    xh = xn.astype(BF16)
    xl = (xn - xh.astype(F32)).astype(BF16)
    logits = (_dot(xh, wrh_ref[...]) + (_dot(xl, wrh_ref[...]) + _dot(xh, wrl_ref[...]))) + br_ref[...]

    lane = lax.broadcasted_iota(jnp.int32, logits.shape, 1)
    vals, sel = [], []
    work = logits
    for _ in range(TOP_K):
        mx = jnp.max(work, axis=-1, keepdims=True)
        idx = jnp.min(jnp.where(work == mx, lane, V7X_LANES), axis=-1, keepdims=True)
        vals.append(mx)
        sel.append(idx)
        work = jnp.where(lane == idx, -jnp.inf, work)
    ex = [jnp.exp(v - vals[0]) for v in vals]
    tot = ex[0] + ex[1] + ex[2] + ex[3]

    onehot = [(lane == idx) for idx in sel]
    picked = jnp.where(onehot[0] | onehot[1] | onehot[2] | onehot[3], 1.0, 0.0)
    earlier = (lax.broadcasted_iota(jnp.int32, (tm, tm), 1) < lax.broadcasted_iota(jnp.int32, (tm, tm), 0))
    before = _dot(jnp.where(earlier, 1.0, 0.0).astype(BF16), picked.astype(BF16)) + cnt_ref[...]
    cnt_ref[...] = cnt_ref[...] + jnp.sum(picked, axis=0, keepdims=True)

    ridx = jnp.zeros(logits.shape, jnp.int32)
    rgate = jnp.zeros(logits.shape, F32)
    for k in range(TOP_K):
        rank = jnp.sum(jnp.where(onehot[k], before, 0.0), axis=-1, keepdims=True).astype(jnp.int32)
        ridx = jnp.where(lane == k, sel[k], ridx)
        ridx = jnp.where(lane == TOP_K + k, rank, ridx)
        rgate = jnp.where(lane == k, ex[k] / tot, rgate)
    ridx_ref[...] = ridx
    rgate_ref[...] = rgate


def _merge(cat_a, cat_m, x, w_a, w_m, g_ffn, wr_hi, wr_lo, br, tm):
    t, d = x.shape
    row = lambda c: pl.BlockSpec((tm, c), lambda i: (i, 0))
    const = lambda shape: pl.BlockSpec(shape, lambda i: (0,) * len(shape))
    return pl.pallas_call(
        _merge_kernel,
        grid=(t // tm,),
        in_specs=[row(FOX_W), row(ML_W), row(d), const(w_a.shape), const(w_m.shape), const((1, d)),
                  const(wr_hi.shape), const(wr_lo.shape), const((1, V7X_LANES))],
        out_specs=[row(d), pl.BlockSpec((tm,) + _row_tile(d), lambda i: (i, 0, 0)), row(V7X_LANES),
                   row(V7X_LANES), const((1, V7X_LANES))],
        out_shape=[jax.ShapeDtypeStruct((t, d), F32), jax.ShapeDtypeStruct((t,) + _row_tile(d), F32),
                   jax.ShapeDtypeStruct((t, V7X_LANES), jnp.int32),
                   jax.ShapeDtypeStruct((t, V7X_LANES), F32),
                   jax.ShapeDtypeStruct((1, V7X_LANES), F32)],
        compiler_params=_params("arbitrary"),
        name="merge_router",
    )(cat_a, cat_m, x, w_a, w_m, g_ffn, wr_hi, wr_lo, br)


def _row_tile(d):
    return (d // V7X_LANES, V7X_LANES)


def _row_copy(src_ref, src_row, dst_ref, dst_row, sem):
    return pltpu.make_async_copy(src_ref.at[pl.ds(src_row, 1)], dst_ref.at[pl.ds(dst_row, 1)], sem)


def _scatter_rows(pos_ref, x_ref, xs_ref, sem):
    tm = x_ref.shape[0]

    def issue(g, _):
        for u in range(DMA_UNROLL):
            r = g * DMA_UNROLL + u
            for k in range(TOP_K):
                _row_copy(x_ref, r, xs_ref, pos_ref[0, 0, TOP_K * r + k], sem).start(priority=k % 2)
        return 0

    def drain(g, _):
        for _ in range(DMA_UNROLL * TOP_K):
            _row_copy(x_ref, 0, xs_ref, 0, sem).wait()
        return 0

    lax.fori_loop(0, tm // DMA_UNROLL, issue, 0)
    lax.fori_loop(0, tm // DMA_UNROLL, drain, 0)


def _dispatch_kernel(fs_ref, fl_ref, pos_p_ref, x_p_ref, pos_s_ref, x_s_ref, xs_ref, zero_ref, sem, *, ntp, nts):
    i = pl.program_id(0)

    @pl.when(i < ntp)
    def _():
        _scatter_rows(pos_p_ref, x_p_ref, xs_ref, sem)

    @pl.when((i >= ntp) & (i < ntp + nts))
    def _():
        _scatter_rows(pos_s_ref, x_s_ref, xs_ref, sem)

    @pl.when(i == ntp + nts)
    def _():
        zero_ref[...] = jnp.zeros_like(zero_ref)
        full = zero_ref.shape[0]
        bits = [1 << p for p in range(full.bit_length() - 2, -1, -1)]

        def zeros_to(start, size):
            return pltpu.make_async_copy(zero_ref.at[pl.ds(0, size)], xs_ref.at[pl.ds(start, size)], sem)

        def per_run(e, _, wait):
            start, n = fs_ref[e], fl_ref[e]
            whole = n // full

            def whole_copy(q, _):
                c = zeros_to(start + q * full, full)
                c.wait() if wait else c.start()
                return 0

            lax.fori_loop(0, whole, whole_copy, 0)
            for p in bits:
                @pl.when((n & p) != 0)
                def _():
                    c = zeros_to(start + (n & ~(2 * p - 1)), p)
                    c.wait() if wait else c.start()
            return 0

        lax.fori_loop(0, fs_ref.shape[0], functools.partial(per_run, wait=False), 0)
        lax.fori_loop(0, fs_ref.shape[0], functools.partial(per_run, wait=True), 0)


def _dispatch(fill_start, fill_len, pos3_p, x_p, pos3_s, x_s, n_rows, tm_p, tm_s):
    tp, sub, lanes = x_p.shape
    ntp, nts = tp // tm_p, x_s.shape[0] // tm_s
    clamp_p = lambda i, fs, fl: (jnp.minimum(i, ntp - 1), 0, 0)
    clamp_s = lambda i, fs, fl: (jnp.clip(i - ntp, 0, nts - 1), 0, 0)
    grid_spec = pltpu.PrefetchScalarGridSpec(
        num_scalar_prefetch=2,
        grid=(ntp + nts + 1,),
        in_specs=[pl.BlockSpec((1, 1, TOP_K * tm_p), clamp_p, memory_space=pltpu.SMEM),
                  pl.BlockSpec((tm_p, sub, lanes), clamp_p),
                  pl.BlockSpec((1, 1, TOP_K * tm_s), clamp_s, memory_space=pltpu.SMEM),
                  pl.BlockSpec((tm_s, sub, lanes), clamp_s)],
        out_specs=pl.BlockSpec(memory_space=pl.ANY),
        scratch_shapes=[pltpu.VMEM((FFN_ROWS, sub, lanes), F32), pltpu.SemaphoreType.DMA(())],
    )
    return pl.pallas_call(
        functools.partial(_dispatch_kernel, ntp=ntp, nts=nts),
        grid_spec=grid_spec,
        out_shape=jax.ShapeDtypeStruct((n_rows, sub, lanes), F32),
        compiler_params=_params("arbitrary"),
        name="moe_dispatch",
    )(fill_start, fill_len, pos3_p, x_p, pos3_s, x_s)


def _ffn_kernel(te_ref, nu_ref, x_ref, wgu_ref, bgu_ref, wd_ref, bd_ref, y_ref, wgu16_ref, wd16_ref):
    i = pl.program_id(0)
    d_ff = wd_ref.shape[1]

    @pl.when((i == 0) | (te_ref[i] != te_ref[jnp.maximum(i - 1, 0)]))
    def _():
        wgu16_ref[...] = wgu_ref[0].astype(BF16)
        wd16_ref[...] = wd_ref[0].astype(BF16)

    @pl.when(i < nu_ref[0])
    def _():
        x = x_ref[...].reshape(x_ref.shape[0], -1)
        gu = _dot(x.astype(BF16), wgu16_ref[...]) + bgu_ref[0]
        gate = jnp.minimum(gu[:, :d_ff], SWIGLU_LIMIT)
        up = jnp.clip(gu[:, d_ff:], -SWIGLU_LIMIT, SWIGLU_LIMIT)
        h = (up + 1.0) * (gate * _sigmoid(SWIGLU_ALPHA * gate))
        y = _dot(h.astype(BF16), wd16_ref[...]) + bd_ref[0]
        y_ref[...] = y.reshape(y_ref.shape)

    @pl.when(i >= nu_ref[0])
    def _():
        y_ref[...] = jnp.zeros_like(y_ref)


def _ffn(tile_expert, n_used, xs, w_gu, b_gu, w_d, b_d, tm):
    r = xs.shape[0]
    d_ff, d = w_d.shape[1:]
    tiles = pl.BlockSpec((tm,) + _row_tile(d), lambda i, te, nu: (i, 0, 0))
    grid_spec = pltpu.PrefetchScalarGridSpec(
        num_scalar_prefetch=2,
        grid=(r // tm,),
        in_specs=[tiles,
                  pl.BlockSpec((1, d, 2 * d_ff), lambda i, te, nu: (te[i], 0, 0)),
                  pl.BlockSpec((1, 1, 2 * d_ff), lambda i, te, nu: (te[i], 0, 0)),
                  pl.BlockSpec((1, d_ff, d), lambda i, te, nu: (te[i], 0, 0)),
                  pl.BlockSpec((1, 1, d), lambda i, te, nu: (te[i], 0, 0))],
        out_specs=tiles,
        scratch_shapes=[pltpu.VMEM((d, 2 * d_ff), BF16), pltpu.VMEM((d_ff, d), BF16)],
    )
    return pl.pallas_call(
        _ffn_kernel,
        grid_spec=grid_spec,
        out_shape=jax.ShapeDtypeStruct(xs.shape, F32),
        compiler_params=_params("arbitrary"),
        name="moe_ffn",
    )(tile_expert, n_used, xs, w_gu, b_gu, w_d, b_d)


def _combine_kernel(pos_ref, pos_next_ref, y1_ref, gate_ref, gfin_ref, ys_ref, o_ref, buf_ref, sem):
    tm = y1_ref.shape[0]
    i = pl.program_id(0)
    slot = i % 2

    def gather(p_ref, s):
        def issue(g, _):
            for u in range(DMA_UNROLL):
                r = g * DMA_UNROLL + u
                for k in range(TOP_K):
                    _row_copy(ys_ref, p_ref[0, 0, TOP_K * r + k], buf_ref.at[s, k], r,
                              sem.at[s]).start(priority=k % 2)
            return 0

        lax.fori_loop(0, tm // DMA_UNROLL, issue, 0)

    @pl.when(i == 0)
    def _():
        gather(pos_ref, 0)

    @pl.when(i + 1 < pl.num_programs(0))
    def _():
        gather(pos_next_ref, 1 - slot)

    def drain(g, _):
        for _ in range(DMA_UNROLL * TOP_K):
            _row_copy(ys_ref, 0, buf_ref.at[slot, 0], 0, sem.at[slot]).wait()
        return 0

    lax.fori_loop(0, tm // DMA_UNROLL, drain, 0)

    gate = gate_ref[...]
    moe = gate[:, 0:1] * buf_ref[slot, 0].reshape(tm, -1)
    for k in range(1, TOP_K):
        moe = moe + gate[:, k:k + 1] * buf_ref[slot, k].reshape(tm, -1)
    y = y1_ref[...] + moe
    o_ref[...] = (y * lax.rsqrt(jnp.mean(y * y, axis=-1, keepdims=True) + EPS)) * gfin_ref[...]


def _combine(pos3, y1, rgate, g_final, ys, tm):
    t, d = y1.shape
    nt = t // tm
    pos_spec = lambda f: pl.BlockSpec((1, 1, TOP_K * tm), f, memory_space=pltpu.SMEM)
    return pl.pallas_call(
        _combine_kernel,
        grid=(nt,),
        in_specs=[pos_spec(lambda i: (i, 0, 0)), pos_spec(lambda i: (jnp.minimum(i + 1, nt - 1), 0, 0)),
                  pl.BlockSpec((tm, d), lambda i: (i, 0)),
                  pl.BlockSpec((tm, V7X_LANES), lambda i: (i, 0)),
                  pl.BlockSpec((1, d), lambda i: (0, 0)),
                  pl.BlockSpec(memory_space=pl.ANY)],
        out_specs=pl.BlockSpec((tm, d), lambda i: (i, 0)),
        out_shape=jax.ShapeDtypeStruct((t, d), F32),
        scratch_shapes=[pltpu.VMEM((2, TOP_K, tm) + _row_tile(d), F32), pltpu.SemaphoreType.DMA((2,))],
        compiler_params=_params("arbitrary"),
        name="moe_combine",
    )(pos3, pos3, y1, rgate, g_final, ys)


def _mixer(x, hist, fox_cache, ml_state, w, tm, tq, tk, chunk):
    bsz, seq, _ = x.shape
    hist8 = jnp.pad(hist, ((0, 0), (V7X_SUBLANES - (CONV_W - 1), 0), (0, 0)))
    proj = _project(x, hist8, w["g_mix"], w["wbig"], w["wsm"], w["bsm"], w["conv_w"], w["conv_b"], tm,
                    aug=fox_cache is None)

    if fox_cache is None:
        q_offset = 0
        assert seq % tk == 0
        (fq, fk32, fv32, ka0, ka1, va0, va1, small, flogf, mq, mk, mv, mo, conv_new) = proj
    else:
        (fq, fk32, fv32, fk16, fv16, small, flogf, mq, mk, mv, mo, conv_new) = proj
        ck_c, cv_c, clogf_c = fox_cache
        q_offset = ck_c.shape[1]
        assert q_offset + seq <= tk
        clogf_c = jnp.pad(clogf_c, ((0, 0), (0, 0), (S_FF, V7X_LANES - S_FF - FOX_HEADS)))
        lf_all = jnp.pad(jnp.concatenate([clogf_c, small], axis=1), ((0, 0), (0, tk - q_offset - seq), (0, 0)))
        ka0, ka1, va0, va1 = _fox_prep(lf_all, [ck_c.reshape(bsz, q_offset, FOX_W), fk16],
                                       [cv_c.reshape(bsz, q_offset, FOX_W), fv16], tk)
    cat_a = _fox(fq, ka0, ka1, va0, va1, w["g_fox"], q_offset, tq, tk)

    c0, n0, m0 = ml_state
    m0 = jnp.pad(m0, ((0, 0), (S_MI, V7X_LANES - S_MI - ML_HEADS)))
    cat_m, c_new, n_new, m_new = _mlstm(mq, mk, mv, mo, small, c0, n0, m0, w["g_ml"], chunk)
    m_new = m_new[:, S_MI:S_MI + ML_HEADS]

    states = (fk32.reshape(bsz, seq, FOX_HEADS, FOX_HD), fv32.reshape(bsz, seq, FOX_HEADS, FOX_HD), flogf,
              c_new, n_new, m_new, conv_new)
    return cat_a, cat_m, states


def kernel(x_prompt, x_sample, cache_fox_k, cache_fox_v, cache_fox_logf, state_mlstm_C, state_mlstm_n,
           state_mlstm_m, state_mlstm_conv, norm_mix_g, w_in, b_fox_f, conv_w, conv_b, b_ml_i, b_ml_f,
           g_fox, g_ml, w_out, norm_ffn_g, w_router, b_router, w_gate_up, b_gate_up, w_down, b_down,
           norm_final_g):
    depth = w_in.shape[0]
    assert depth == 1, "the final norm is fused into the last layer's combine; only depth 1 is wired up"
    bp, sp, d = x_prompt.shape
    bs, ss, _ = x_sample.shape
    n_exp = w_router.shape[-1]
    d_ff = w_down.shape[2]
    yp, ys = x_prompt, x_sample
    p_st, s_st = [], []

    for l in range(depth):
        wl = w_in[l]
        w = {
            "g_mix": norm_mix_g[l][None, :],
            "wbig": jnp.concatenate([wl[:, O_FQ:O_FF], wl[:, O_MQ:O_MI]], axis=1).astype(BF16),
            "wsm": jnp.pad(jnp.concatenate([wl[:, O_FF:O_MQ], wl[:, O_MI:P_IN]], axis=1),
                           ((0, 0), (0, V7X_LANES - S_END))).astype(BF16),
            "bsm": jnp.pad(jnp.concatenate([b_fox_f[l], b_ml_i[l], b_ml_f[l]]), (0, V7X_LANES - S_END))[None, :],
            "conv_w": conv_w[l],
            "conv_b": conv_b[l][None, :],
            "g_fox": g_fox[l][None, :],
            "g_ml": g_ml[l][None, :],
        }
        w_a = w_out[l][:FOX_W].astype(BF16)
        w_m = w_out[l][FOX_W:].astype(BF16)
        g_ffn = norm_ffn_g[l][None, :]
        wr = jnp.pad(w_router[l], ((0, 0), (0, V7X_LANES - n_exp)))
        wr_hi = wr.astype(BF16)
        wr_lo = (wr - wr_hi.astype(F32)).astype(BF16)
        br = jnp.pad(b_router[l], (0, V7X_LANES - n_exp), constant_values=NEG)[None, :]

        zeros_state = (jnp.zeros((bp, ML_HEADS, ML_DV, ML_DK), F32), jnp.zeros((bp, ML_HEADS, ML_DK), F32),
                       jnp.zeros((bp, ML_HEADS), F32))
        cat_a_p, cat_m_p, st_p = _mixer(yp, jnp.zeros((bp, CONV_W - 1, QK_W), F32), None, zeros_state, w,
                                        min(PROJ_ROWS, sp), min(FOX_QBLOCK, sp), min(FOX_BLOCK, sp), CHUNK)
        p_st.append(st_p)
        past = cache_fox_k.shape[2]
        tk_s = -(-(past + ss) // V7X_LANES) * V7X_LANES
        cat_a_s, cat_m_s, st_s = _mixer(
            ys, state_mlstm_conv[l], (cache_fox_k[l], cache_fox_v[l], cache_fox_logf[l]),
            (state_mlstm_C[l].astype(F32), state_mlstm_n[l].astype(F32), state_mlstm_m[l].astype(F32)),
            w, ss, ss, tk_s, ss)
        s_st.append(st_s)

        tp, ts = bp * sp, bs * ss
        y1_p, xn_p, ridx_p, rgate_p, cnt_p = _merge(
            cat_a_p.reshape(tp, FOX_W), cat_m_p.reshape(tp, ML_W), yp.reshape(tp, d), w_a, w_m, g_ffn,
            wr_hi, wr_lo, br, min(MERGE_ROWS, tp))
        y1_s, xn_s, ridx_s, rgate_s, cnt_s = _merge(
            cat_a_s.reshape(ts, FOX_W), cat_m_s.reshape(ts, ML_W), ys.reshape(ts, d), w_a, w_m, g_ffn,
            wr_hi, wr_lo, br, min(MERGE_ROWS, ts))

        cnt_p = cnt_p[0, :n_exp].astype(jnp.int32)
        cnt_s = cnt_s[0, :n_exp].astype(jnp.int32)
        seg_rows = -(-(cnt_p + cnt_s) // FFN_ROWS) * FFN_ROWS
        seg_end = jnp.cumsum(seg_rows)
        seg_start = seg_end - seg_rows
        n_rows = -(-((tp + ts) * TOP_K + n_exp * (FFN_ROWS - 1)) // FFN_ROWS) * FFN_ROWS
        n_tiles = n_rows // FFN_ROWS
        tile_row0 = jnp.arange(n_tiles, dtype=jnp.int32) * FFN_ROWS
        tile_expert = jnp.minimum(jnp.sum((seg_end[None, :] <= tile_row0[:, None]).astype(jnp.int32), axis=1),
                                  n_exp - 1)
        n_used = (seg_end[-1:] // FFN_ROWS).astype(jnp.int32)
        e_p, rank_p = ridx_p[:, :TOP_K], ridx_p[:, TOP_K:2 * TOP_K]
        e_s, rank_s = ridx_s[:, :TOP_K], ridx_s[:, TOP_K:2 * TOP_K]
        pos_p = seg_start[e_p] + rank_p
        pos_s = seg_start[e_s] + cnt_p[e_s] + rank_s
        tm_p, tm_s = min(MOE_ROWS, tp), min(MOE_ROWS, ts)
        pos3_p = pos_p.reshape(tp // tm_p, 1, TOP_K * tm_p)
        pos3_s = pos_s.reshape(ts // tm_s, 1, TOP_K * tm_s)

        cnt = cnt_p + cnt_s
        fill_start = jnp.concatenate([seg_start + cnt, seg_end[-1:]]).astype(jnp.int32)
        fill_len = jnp.concatenate([seg_rows - cnt, n_rows - seg_end[-1:]]).astype(jnp.int32)
        xs = _dispatch(fill_start, fill_len, pos3_p, xn_p, pos3_s, xn_s, n_rows, tm_p, tm_s)
        ysort = _ffn(tile_expert, n_used, xs, w_gate_up[l], b_gate_up[l].reshape(n_exp, 1, 2 * d_ff),
                     w_down[l], b_down[l].reshape(n_exp, 1, d), FFN_ROWS)

        g_fin = norm_final_g[None, :]
        yp = _combine(pos3_p, y1_p, rgate_p, g_fin, ysort, tm_p).reshape(bp, sp, d)
        ys = _combine(pos3_s, y1_s, rgate_s, g_fin, ysort, tm_s).reshape(bs, ss, d)

    p_out = tuple(jnp.stack(a) for a in zip(*p_st))
    s_out = tuple(jnp.stack(a) for a in zip(*s_st))
    return (yp, ys) + p_out + s_out
```

```python
import functools
import math

import numpy as np

import jax
import jax.numpy as jnp
from jax import lax
from jax.experimental import pallas as pl
from jax.experimental.pallas import tpu as pltpu

F32 = jnp.float32
BF16 = jnp.bfloat16

FOX_HEADS = 8
FOX_HD = 64
FOX_W = FOX_HEADS * FOX_HD
ML_HEADS = 4
ML_DK = 128
ML_DV = 128
ML_W = ML_HEADS * ML_DV
QK_W = 2 * ML_HEADS * ML_DK
CONV_W = 4
CHUNK = 64
TOP_K = 4
SWIGLU_LIMIT = 7.0
SWIGLU_ALPHA = 1.702
EPS = 1e-6
NEG = -1e30
LOG2E = math.log2(math.e)

O_FQ = 0
O_FK = O_FQ + FOX_W
O_FV = O_FK + FOX_W
O_FF = O_FV + FOX_W
O_MQ = O_FF + FOX_HEADS
O_MK = O_MQ + ML_HEADS * ML_DK
O_MV = O_MK + ML_HEADS * ML_DK
O_MO = O_MV + ML_W
O_MI = O_MO + ML_W
O_MF = O_MI + ML_HEADS
P_IN = O_MF + ML_HEADS

B_FQ, B_FK, B_FV, B_QK, B_MV, B_MO, B_END = 0, 512, 1024, 1536, 2560, 3072, 3584
S_FF, S_MI, S_MF, S_END = 0, 8, 12, 16
BIAS_TERMS = 3

V7X_LANES = 128
V7X_SUBLANES = 8
V7X_VMEM_LIMIT_BYTES = 56 * 1024 * 1024

PROJ_ROWS = 512
FOX_BLOCK = 512
FOX_QBLOCK = 1024
FOX_SUB = 512
FOX_UNROLLS = (4, 2, 1)
MERGE_ROWS = 512
MOE_ROWS = 512
FFN_ROWS = 512
DMA_UNROLL = 8


def _params(*semantics):
    return pltpu.CompilerParams(dimension_semantics=semantics, vmem_limit_bytes=V7X_VMEM_LIMIT_BYTES)


def _dot(a, b):
    return jnp.dot(a, b, preferred_element_type=F32)


def _dot_nt(a, b):
    return lax.dot_general(a, b, (((1,), (1,)), ((), ())), preferred_element_type=F32)


def _mxu_transpose(eye, x):
    return _dot_nt(eye, x)


def _sigmoid(x):
    return 1.0 / (1.0 + jnp.exp(-x))


def _log_sigmoid(x):
    return jnp.minimum(x, 0.0) - jnp.log1p(jnp.exp(-jnp.abs(x)))


def _split3(x):
    t1 = x.astype(BF16)
    r1 = x - t1.astype(F32)
    t2 = r1.astype(BF16)
    t3 = (r1 - t2.astype(F32)).astype(BF16)
    return t1, t2, t3


def _lower_tri(n):
    r = lax.broadcasted_iota(jnp.int32, (n, n), 0)
    c = lax.broadcasted_iota(jnp.int32, (n, n), 1)
    return c <= r


def _proj_kernel(*refs, aug):
    x_ref, g_ref, wbig_ref, wsm_ref, bsm_ref, cw_ref, cb_ref, hist_ref = refs[:8]
    rest = refs[8:]
    if aug:
        place_ref, rest = rest[0], rest[1:]
        (fq_ref, fk32_ref, fv32_ref, ka0_ref, ka1_ref, va0_ref, va1_ref, sm_ref, flogf_ref, mq_ref, mk_ref,
         mv_ref, mo_ref, cnew_ref, halo_ref, carry_ref) = rest
    else:
        (fq_ref, fk32_ref, fv32_ref, fk16_ref, fv16_ref, sm_ref, flogf_ref, mq_ref, mk_ref, mv_ref, mo_ref,
         cnew_ref, halo_ref) = rest
    tm = x_ref.shape[1]

    @pl.when(pl.program_id(1) == 0)
    def _():
        halo_ref[...] = hist_ref[0]
        if aug:
            carry_ref[...] = jnp.zeros_like(carry_ref)

    x = x_ref[0]
    xn = (x * lax.rsqrt(jnp.mean(x * x, axis=-1, keepdims=True) + EPS)) * g_ref[...]
    xb = xn.astype(BF16)
    zs = _dot(xb, wsm_ref[...]) + bsm_ref[...]
    z = _dot(xb, wbig_ref[...])

    fq_ref[0] = (z[:, B_FQ:B_FQ + FOX_W] * (FOX_HD ** -0.5 * LOG2E)).astype(BF16)
    fk = z[:, B_FK:B_FK + FOX_W]
    fv = z[:, B_FV:B_FV + FOX_W]
    fk32_ref[0] = fk
    fv32_ref[0] = fv
    mv_ref[0] = z[:, B_MV:B_MV + ML_W].astype(BF16)
    mo_ref[0] = _sigmoid(z[:, B_MO:B_MO + ML_W]).astype(BF16)

    lane = lax.broadcasted_iota(jnp.int32, zs.shape, 1)
    is_forget = (lane < S_MI) | ((lane >= S_MF) & (lane < S_END))
    sm = jnp.where(is_forget, _log_sigmoid(zs), zs)
    sm_ref[0] = sm
    flogf_ref[0] = sm[:, S_FF:S_MI]

    if aug:
        rows = lax.broadcasted_iota(jnp.int32, sm.shape, 0)
        c = sm
        sh = 1
        while sh < tm:
            c = c + jnp.where(rows >= sh, pltpu.roll(c, sh, axis=0), 0.0)
            sh *= 2
        c = c + carry_ref[...]
        carry_ref[...] = c[tm - 1:tm, :]
        b1, b2, b3 = _split3(c * (-LOG2E))
        kb = _dot(b1, place_ref[0]) + (_dot(b2, place_ref[1]) + _dot(b3, place_ref[2]))
        first_head = lax.broadcasted_iota(jnp.int32, fk.shape, 1) % V7X_LANES < FOX_HD
        ka0_ref[0] = jnp.where(first_head, fk, kb).astype(BF16)
        ka1_ref[0] = jnp.where(first_head, kb, fk).astype(BF16)
        va0_ref[0] = jnp.where(first_head, fv, 1.0).astype(BF16)
        va1_ref[0] = jnp.where(first_head, 1.0, fv).astype(BF16)
    else:
        fk16_ref[0] = fk.astype(BF16)
        fv16_ref[0] = fv.astype(BF16)

    u = z[:, B_QK:B_QK + QK_W]
    up = jnp.concatenate([halo_ref[...], u], axis=0)
    n = tm + V7X_SUBLANES
    first = V7X_SUBLANES - (CONV_W - 1)
    y = cb_ref[...] + cw_ref[0:1, :] * pltpu.roll(up, n - first, axis=0)[:tm]
    for j in range(1, CONV_W - 1):
        y = y + cw_ref[j:j + 1, :] * pltpu.roll(up, n - (first + j), axis=0)[:tm]
    y = y + cw_ref[CONV_W - 1:CONV_W, :] * u
    qk = y * _sigmoid(y)
    mq_ref[0] = qk[:, :QK_W // 2].astype(BF16)
    mk_ref[0] = (qk[:, QK_W // 2:] * (ML_DK ** -0.5)).astype(BF16)

    halo_ref[...] = u[tm - V7X_SUBLANES:, :]
    cnew_ref[0] = halo_ref[first:, :]


def _project(x, hist8, g, wbig, wsm, bsm, cw, cb, tm, aug):
    bsz, seq, d = x.shape
    grid = (bsz, seq // tm)
    row = lambda c: pl.BlockSpec((1, tm, c), lambda b, s: (b, s, 0))
    const = lambda shape: pl.BlockSpec(shape, lambda b, s: (0,) * len(shape))
    n_kv = 4 if aug else 2
    outs = ([(FOX_W, BF16), (FOX_W, F32), (FOX_W, F32)] + [(FOX_W, BF16)] * n_kv
            + [(V7X_LANES, F32), (FOX_HEADS, F32), (ML_W, BF16), (ML_W, BF16), (ML_W, BF16), (ML_W, BF16)])
    out_shape = [jax.ShapeDtypeStruct((bsz, seq, c), dt) for c, dt in outs]
    out_specs = [row(c) for c, _ in outs]
    out_shape.append(jax.ShapeDtypeStruct((bsz, CONV_W - 1, QK_W), F32))
    out_specs.append(pl.BlockSpec((1, CONV_W - 1, QK_W), lambda b, s: (b, 0, 0)))
    in_specs = [row(d), const((1, d)), const(wbig.shape), const(wsm.shape), const((1, V7X_LANES)),
                const((CONV_W, QK_W)), const((1, QK_W)),
                pl.BlockSpec((1, V7X_SUBLANES, QK_W), lambda b, s: (b, 0, 0))]
    args = [x, g, wbig, wsm, bsm, cw, cb, hist8]
    scratch = [pltpu.VMEM((V7X_SUBLANES, QK_W), F32)]
    if aug:
        place = _bias_placement()
        in_specs.append(const(place.shape))
        args.append(place)
        scratch.append(pltpu.VMEM((1, V7X_LANES), F32))
    return pl.pallas_call(
        functools.partial(_proj_kernel, aug=aug),
        grid=grid,
        in_specs=in_specs,
        out_specs=out_specs,
        out_shape=out_shape,
        scratch_shapes=scratch,
        compiler_params=_params("arbitrary", "arbitrary"),
        name="proj",
    )(*args)


def _bias_lane(head, term):
    return (head // 2) * V7X_LANES + (FOX_HD if head % 2 == 0 else 0) + term


def _bias_placement():
    place = np.zeros((BIAS_TERMS, V7X_LANES, FOX_W), np.float32)
    for h in range(FOX_HEADS):
        for t in range(BIAS_TERMS):
            place[t, S_FF + h, _bias_lane(h, t)] = 1.0
    return jnp.asarray(place, BF16)


def _fox_prep_kernel(lf_ref, *refs, n_parts):
    k_refs, v_refs = refs[:n_parts], refs[n_parts:2 * n_parts]
    place_ref, ka0_ref, ka1_ref, va0_ref, va1_ref, carry_ref = refs[2 * n_parts:]
    tm = lf_ref.shape[1]

    def rows(part_refs):
        parts = [r[0].astype(F32) for r in part_refs]
        missing = tm - sum(p.shape[0] for p in parts)
        if missing:
            parts.append(jnp.zeros((missing, FOX_W), F32))
        return parts[0] if len(parts) == 1 else jnp.concatenate(parts, axis=0)

    @pl.when(pl.program_id(1) == 0)
    def _():
        carry_ref[...] = jnp.zeros_like(carry_ref)

    tri = jnp.where(_lower_tri(tm), 1.0, 0.0).astype(BF16)
    t1, t2, t3 = _split3(lf_ref[0])
    c = carry_ref[...] + (_dot(tri, t1) + (_dot(tri, t2) + _dot(tri, t3)))
    carry_ref[...] = c[tm - 1:tm, :]
    b1, b2, b3 = _split3(c * (-LOG2E))
    kb = _dot(b1, place_ref[0]) + (_dot(b2, place_ref[1]) + _dot(b3, place_ref[2]))

    k = rows(k_refs)
    v = rows(v_refs)
    first_head = lax.broadcasted_iota(jnp.int32, k.shape, 1) % V7X_LANES < FOX_HD
    ka0_ref[0] = jnp.where(first_head, k, kb).astype(BF16)
    ka1_ref[0] = jnp.where(first_head, kb, k).astype(BF16)
    va0_ref[0] = jnp.where(first_head, v, 1.0).astype(BF16)
    va1_ref[0] = jnp.where(first_head, 1.0, v).astype(BF16)


def _fox_prep(lf, k_parts, v_parts, tm):
    bsz, lk, _ = lf.shape
    place = _bias_placement()
    wide = pl.BlockSpec((1, tm, FOX_W), lambda b, s: (b, s, 0))
    if len(k_parts) == 1:
        part_specs = [wide, wide]
    else:
        assert lk == tm, "several key/value parts are only stitched inside a single tile"
        part_specs = [pl.BlockSpec((1,) + p.shape[1:], lambda b, s: (b, 0, 0)) for p in k_parts + v_parts]
    sds = jax.ShapeDtypeStruct((bsz, lk, FOX_W), BF16)
    return pl.pallas_call(
        functools.partial(_fox_prep_kernel, n_parts=len(k_parts)),
        grid=(bsz, lk // tm),
        in_specs=[pl.BlockSpec((1, tm, V7X_LANES), lambda b, s: (b, s, 0))] + part_specs
        + [pl.BlockSpec(place.shape, lambda b, s: (0, 0, 0))],
        out_specs=[wide, wide, wide, wide],
        out_shape=[sds, sds, sds, sds],
        scratch_shapes=[pltpu.VMEM((1, V7X_LANES), F32)],
        compiler_params=_params("arbitrary", "arbitrary"),
        name="fox_prep",
    )(lf, *k_parts, *v_parts, place)


def _fox_kernel(q_ref, ka0_ref, ka1_ref, va0_ref, va1_ref, g_ref, o_ref, *, tq, tk, sub, q_offset):
    ts = min(tq, sub)
    n_sub = tq // ts
    diag_blocks = max(ts // tk, 1)
    k_refs = (ka0_ref, ka1_ref)
    v_refs = (va0_ref, va1_ref)
    lane = lax.broadcasted_iota(jnp.int32, (ts, V7X_LANES), 1)
    first_head = lane < FOX_HD
    q_lo = q_offset + pl.program_id(2) * tq
    n_common = (q_lo + 1) // tk
    single_block = ka0_ref.shape[1] == tk
    even_common = q_offset % (2 * tk) == 0 and tq % (2 * tk) == 0

    def chain_q(r, hh):
        q = q_ref[0, r * ts:(r + 1) * ts, :].astype(F32)
        if hh == 0:
            return jnp.where(first_head, q, jnp.where(lane < FOX_HD + BIAS_TERMS, 1.0, 0.0)).astype(BF16)
        return jnp.where(first_head, jnp.where(lane < BIAS_TERMS, 1.0, 0.0), q).astype(BF16)

    chains = [(r, hh) for r in range(n_sub) for hh in range(2)]
    qs = [chain_q(r, hh) for r, hh in chains]

    def scores(c, j):
        return _dot_nt(qs[c], k_refs[chains[c][1]][0, pl.ds(pl.multiple_of(j * tk, tk), tk), :])

    def update(state, c, j, s, masked):
        r, hh = chains[c]
        m, acc = state
        start = pl.multiple_of(j * tk, tk)
        if masked:
            qpos = q_lo + r * ts + lax.broadcasted_iota(jnp.int32, s.shape, 0)
            kpos = start + lax.broadcasted_iota(jnp.int32, s.shape, 1)
            s = jnp.where(kpos <= qpos, s, NEG)
        m_new = jnp.maximum(m, jnp.max(s, axis=-1, keepdims=True))
        p = jnp.exp2(s - m_new).astype(BF16)
        acc = jnp.exp2(m - m_new) * acc + _dot(p, v_refs[hh][0, pl.ds(start, tk), :])
        return m_new, acc

    def step(state, c, j, masked):
        return update(state, c, j, scores(c, j), masked)

    def common_run(t, states, nblk):
        states = list(states)
        cur = [scores(c, nblk * t) for c in range(len(chains))]
        for e in range(nblk):
            nxt = []
            for c in range(len(chains)):
                states[c] = update(states[c], c, nblk * t + e, cur[c], False)
                if e + 1 < nblk:
                    nxt.append(scores(c, nblk * t + e + 1))
            cur = nxt
        return tuple(states)

    init = (jnp.full((ts, 1), NEG, F32), jnp.zeros((ts, V7X_LANES), F32))
    states = (init,) * len(chains)
    if not single_block:
        done = 0
        for nblk in FOX_UNROLLS:
            if nblk == 1 and even_common:
                continue
            trips = n_common // nblk
            states = lax.fori_loop(done // nblk, trips, functools.partial(common_run, nblk=nblk), states)
            done = trips * nblk
    states = list(states)
    for c, (r, _) in enumerate(chains):
        for e in range(r * diag_blocks):
            states[c] = step(states[c], c, n_common + e, False)
        for e in range(diag_blocks):
            states[c] = step(states[c], c, n_common + r * diag_blocks + e, True)

    for r in range(n_sub):
        out = jnp.zeros((ts, V7X_LANES), F32)
        for hh in range(2):
            acc = states[chains.index((r, hh))][1]
            own = first_head if hh == 0 else jnp.logical_not(first_head)
            denom_lane = FOX_HD if hh == 0 else 0
            o = jnp.where(own, acc / acc[:, denom_lane:denom_lane + 1], 0.0)
            out = out + o * lax.rsqrt(jnp.sum(o * o, axis=-1, keepdims=True) * (1.0 / FOX_HD) + EPS)
        o_ref[0, r * ts:(r + 1) * ts, :] = (out * g_ref[...]).astype(BF16)


def _fox(q, ka0, ka1, va0, va1, g_fox, q_offset, tq, tk):
    bsz, lq, _ = q.shape
    lk = ka0.shape[1]
    assert (q_offset % tk == 0 and tq % tk == 0) or (lq == tq <= tk and lk == tk), (q_offset, tq, tk, lq, lk)
    pairs = FOX_HEADS // 2
    grid = (bsz, pairs, lq // tq)
    kv_spec = pl.BlockSpec((1, lk, V7X_LANES), lambda b, p, i: (b, 0, p))
    q_spec = pl.BlockSpec((1, tq, V7X_LANES), lambda b, p, i: (b, i, p))
    return pl.pallas_call(
        functools.partial(_fox_kernel, tq=tq, tk=tk, sub=FOX_SUB, q_offset=q_offset),
        grid=grid,
        in_specs=[q_spec, kv_spec, kv_spec, kv_spec, kv_spec,
                  pl.BlockSpec((1, V7X_LANES), lambda b, p, i: (0, p))],
        out_specs=q_spec,
        out_shape=jax.ShapeDtypeStruct((bsz, lq, FOX_W), BF16),
        compiler_params=_params("arbitrary", "arbitrary", "arbitrary"),
        name="fox_attention",
    )(q, ka0, ka1, va0, va1, g_fox)


def _mlstm_kernel(q_ref, k_ref, v_ref, mo_ref, sm_ref, c0_ref, n0_ref, m0_ref, g_ref,
                  o_ref, c_ref, n_ref, m_ref):
    bsz, L, _ = q_ref.shape

    @pl.when(pl.program_id(0) == 0)
    def _():
        c_ref[...] = c0_ref[...]
        n_ref[...] = n0_ref[...]
        m_ref[...] = m0_ref[...]

    causal = _lower_tri(L)
    tri = jnp.where(causal, 1.0, 0.0).astype(BF16)
    eye = jnp.where(lax.broadcasted_iota(jnp.int32, (V7X_LANES, V7X_LANES), 0)
                    == lax.broadcasted_iota(jnp.int32, (V7X_LANES, V7X_LANES), 1), 1.0, 0.0).astype(BF16)
    rows = lax.broadcasted_iota(jnp.int32, (L, V7X_LANES), 0)
    lanes = lax.broadcasted_iota(jnp.int32, (L, V7X_LANES), 1)
    gate_lanes = (lanes >= S_MI) & (lanes < S_END)

    heads = [(b, h) for b in range(bsz) for h in range(ML_HEADS)]
    cols = lambda h: slice(h * ML_DK, (h + 1) * ML_DK)

    sms, bcums = {}, {}
    for b in range(bsz):
        sms[b] = jnp.where(gate_lanes, sm_ref[b], 0.0)
        t1, t2, t3 = _split3(sms[b])
        bcums[b] = _dot(tri, t1) + (_dot(tri, t2) + _dot(tri, t3))

    ones = jnp.ones((V7X_LANES, V7X_LANES), BF16)
    qk, qc, qn = {}, {}, {}
    for b, h in heads:
        qh = q_ref[b, :, cols(h)]
        qk[b, h] = _dot_nt(qh, k_ref[b, :, cols(h)])
        qc[b, h] = _dot_nt(qh, c_ref[b, h].astype(BF16))
        qn[b, h] = _dot((qh.astype(F32) * n_ref[b, h:h + 1, :]).astype(BF16), ones)

    gates = {}
    for b in range(bsz):
        sm = sms[b]
        bcum = pltpu.roll(bcums[b], V7X_LANES - (S_MF - S_MI), axis=1)
        g = sm - bcum
        gmax = g
        sh = 1
        while sh < L:
            gmax = jnp.maximum(gmax, jnp.where(rows >= sh, pltpu.roll(gmax, sh, axis=0), -jnp.inf))
            sh *= 2
        m = m_ref[pl.ds(b, 1), :]
        u = jnp.maximum(m, gmax)
        a = jnp.exp(m - u)
        mt = bcum + u
        m_ref[pl.ds(b, 1), :] = mt[L - 1:L, :]
        g1, g2, g3 = _split3(g)
        g_rows = _mxu_transpose(eye, g1) + (_mxu_transpose(eye, g2) + _mxu_transpose(eye, g3))
        gates[b] = dict(u=u, a=a, em=jnp.exp(-mt), a_last=a[L - 1:L, :], wcol=jnp.exp(g - u[L - 1:L, :]),
                        g_rows=g_rows)

    for b, h in heads:
        gl = S_MI + h
        kh = k_ref[b, :, cols(h)]
        w_h = gates[b]["wcol"][:, gl:gl + 1]
        a_l = gates[b]["a_last"][:, gl:gl + 1]
        vw = (v_ref[b, :, cols(h)].astype(F32) * w_h).astype(BF16)
        c_ref[b, h] = a_l * c_ref[b, h] + _dot(_mxu_transpose(eye, vw).astype(BF16), kh)
        n_ref[b, h:h + 1, :] = (a_l * n_ref[b, h:h + 1, :]
                                + jnp.sum(kh.astype(F32) * w_h, axis=0, keepdims=True))

    for b, h in heads:
        gl = S_MI + h
        gb = gates[b]
        decay = jnp.exp(jnp.where(causal, gb["g_rows"][gl:gl + 1, :] - gb["u"][:, gl:gl + 1], NEG))
        s = qk[b, h] * decay
        a_h = gb["a"][:, gl:gl + 1]
        sb = s.astype(BF16)
        num = a_h * qc[b, h] + _dot(sb, v_ref[b, :, cols(h)])
        den = a_h * qn[b, h] + _dot(sb, ones[:L, :])
        hv = num / jnp.maximum(jnp.abs(den), gb["em"][:, gl:gl + 1])
        hn = hv * lax.rsqrt(_dot((hv * hv).astype(BF16), ones) * (1.0 / ML_DV) + EPS)
        o_ref[b, :, cols(h)] = (hn * g_ref[:, cols(h)] * mo_ref[b, :, cols(h)].astype(F32)).astype(BF16)


def _mlstm(mq, mk, mv, mo, small, c0, n0, m0, g_ml, L):
    bsz, seq, _ = mq.shape
    chunk = pl.BlockSpec((bsz, L, ML_W), lambda c: (0, c, 0))
    full = lambda shape: pl.BlockSpec(shape, lambda c: (0,) * len(shape))
    return pl.pallas_call(
        _mlstm_kernel,
        grid=(seq // L,),
        in_specs=[chunk, chunk, chunk, chunk,
                  pl.BlockSpec((bsz, L, V7X_LANES), lambda c: (0, c, 0)),
                  full(c0.shape), full(n0.shape), full(m0.shape), full((1, ML_W))],
        out_specs=[chunk, full(c0.shape), full(n0.shape), full(m0.shape)],
        out_shape=[jax.ShapeDtypeStruct((bsz, seq, ML_W), BF16),
                   jax.ShapeDtypeStruct(c0.shape, F32),
                   jax.ShapeDtypeStruct(n0.shape, F32),
                   jax.ShapeDtypeStruct(m0.shape, F32)],
        compiler_params=_params("arbitrary"),
        name="mlstm",
    )(mq, mk, mv, mo, small, c0, n0, m0, g_ml)


def _merge_kernel(ca_ref, cm_ref, x_ref, wa_ref, wm_ref, g_ref, wrh_ref, wrl_ref, br_ref,
                  y1_ref, xn_ref, ridx_ref, rgate_ref, cnt_ref):
    tm = x_ref.shape[0]

    @pl.when(pl.program_id(0) == 0)
    def _():
        cnt_ref[...] = jnp.zeros_like(cnt_ref)

    y1 = x_ref[...] + (_dot(ca_ref[...], wa_ref[...]) + _dot(cm_ref[...], wm_ref[...]))
    y1_ref[...] = y1
    xn = (y1 * lax.rsqrt(jnp.mean(y1 * y1, axis=-1, keepdims=True) + EPS)) * g_ref[...]
    xn_ref[...] = xn.reshape(xn_ref.shape)

    xh = xn.astype(BF16)
    xl = (xn - xh.astype(F32)).astype(BF16)
    logits = (_dot(xh, wrh_ref[...]) + (_dot(xl, wrh_ref[...]) + _dot(xh, wrl_ref[...]))) + br_ref[...]

    lane = lax.broadcasted_iota(jnp.int32, logits.shape, 1)
    vals, sel = [], []
    work = logits
    for _ in range(TOP_K):
        mx = jnp.max(work, axis=-1, keepdims=True)
        idx = jnp.min(jnp.where(work == mx, lane, V7X_LANES), axis=-1, keepdims=True)
        vals.append(mx)
        sel.append(idx)
        work = jnp.where(lane == idx, -jnp.inf, work)
    ex = [jnp.exp(v - vals[0]) for v in vals]
    tot = ex[0] + ex[1] + ex[2] + ex[3]

    onehot = [(lane == idx) for idx in sel]
    picked = jnp.where(onehot[0] | onehot[1] | onehot[2] | onehot[3], 1.0, 0.0)
    earlier = (lax.broadcasted_iota(jnp.int32, (tm, tm), 1) < lax.broadcasted_iota(jnp.int32, (tm, tm), 0))
    before = _dot(jnp.where(earlier, 1.0, 0.0).astype(BF16), picked.astype(BF16)) + cnt_ref[...]
    cnt_ref[...] = cnt_ref[...] + jnp.sum(picked, axis=0, keepdims=True)

    ridx = jnp.zeros(logits.shape, jnp.int32)
    rgate = jnp.zeros(logits.shape, F32)
    for k in range(TOP_K):
        rank = jnp.sum(jnp.where(onehot[k], before, 0.0), axis=-1, keepdims=True).astype(jnp.int32)
        ridx = jnp.where(lane == k, sel[k], ridx)
        ridx = jnp.where(lane == TOP_K + k, rank, ridx)
        rgate = jnp.where(lane == k, ex[k] / tot, rgate)
    ridx_ref[...] = ridx
    rgate_ref[...] = rgate


def _merge(cat_a, cat_m, x, w_a, w_m, g_ffn, wr_hi, wr_lo, br, tm):
    t, d = x.shape
    row = lambda c: pl.BlockSpec((tm, c), lambda i: (i, 0))
    const = lambda shape: pl.BlockSpec(shape, lambda i: (0,) * len(shape))
    return pl.pallas_call(
        _merge_kernel,
        grid=(t // tm,),
        in_specs=[row(FOX_W), row(ML_W), row(d), const(w_a.shape), const(w_m.shape), const((1, d)),
                  const(wr_hi.shape), const(wr_lo.shape), const((1, V7X_LANES))],
        out_specs=[row(d), pl.BlockSpec((tm,) + _row_tile(d), lambda i: (i, 0, 0)), row(V7X_LANES),
                   row(V7X_LANES), const((1, V7X_LANES))],
        out_shape=[jax.ShapeDtypeStruct((t, d), F32), jax.ShapeDtypeStruct((t,) + _row_tile(d), F32),
                   jax.ShapeDtypeStruct((t, V7X_LANES), jnp.int32),
                   jax.ShapeDtypeStruct((t, V7X_LANES), F32),
                   jax.ShapeDtypeStruct((1, V7X_LANES), F32)],
        compiler_params=_params("arbitrary"),
        name="merge_router",
    )(cat_a, cat_m, x, w_a, w_m, g_ffn, wr_hi, wr_lo, br)


def _row_tile(d):
    return (d // V7X_LANES, V7X_LANES)


def _row_copy(src_ref, src_row, dst_ref, dst_row, sem):
    return pltpu.make_async_copy(src_ref.at[pl.ds(src_row, 1)], dst_ref.at[pl.ds(dst_row, 1)], sem)


def _scatter_rows(pos_ref, x_ref, xs_ref, sem):
    tm = x_ref.shape[0]

    def issue(g, _):
        for u in range(DMA_UNROLL):
            r = g * DMA_UNROLL + u
            for k in range(TOP_K):
                _row_copy(x_ref, r, xs_ref, pos_ref[0, 0, TOP_K * r + k], sem).start(priority=k % 2)
        return 0

    def drain(g, _):
        for _ in range(DMA_UNROLL * TOP_K):
            _row_copy(x_ref, 0, xs_ref, 0, sem).wait()
        return 0

    lax.fori_loop(0, tm // DMA_UNROLL, issue, 0)
    lax.fori_loop(0, tm // DMA_UNROLL, drain, 0)


def _dispatch_kernel(fs_ref, fl_ref, pos_p_ref, x_p_ref, pos_s_ref, x_s_ref, xs_ref, zero_ref, sem, *, ntp, nts):
    i = pl.program_id(0)

    @pl.when(i < ntp)
    def _():
        _scatter_rows(pos_p_ref, x_p_ref, xs_ref, sem)

    @pl.when((i >= ntp) & (i < ntp + nts))
    def _():
        _scatter_rows(pos_s_ref, x_s_ref, xs_ref, sem)

    @pl.when(i == ntp + nts)
    def _():
        zero_ref[...] = jnp.zeros_like(zero_ref)
        full = zero_ref.shape[0]
        bits = [1 << p for p in range(full.bit_length() - 2, -1, -1)]

        def zeros_to(start, size):
            return pltpu.make_async_copy(zero_ref.at[pl.ds(0, size)], xs_ref.at[pl.ds(start, size)], sem)

        def per_run(e, _, wait):
            start, n = fs_ref[e], fl_ref[e]
            whole = n // full

            def whole_copy(q, _):
                c = zeros_to(start + q * full, full)
                c.wait() if wait else c.start()
                return 0

            lax.fori_loop(0, whole, whole_copy, 0)
            for p in bits:
                @pl.when((n & p) != 0)
                def _():
                    c = zeros_to(start + (n & ~(2 * p - 1)), p)
                    c.wait() if wait else c.start()
            return 0

        lax.fori_loop(0, fs_ref.shape[0], functools.partial(per_run, wait=False), 0)
        lax.fori_loop(0, fs_ref.shape[0], functools.partial(per_run, wait=True), 0)


def _dispatch(fill_start, fill_len, pos3_p, x_p, pos3_s, x_s, n_rows, tm_p, tm_s):
    tp, sub, lanes = x_p.shape
    ntp, nts = tp // tm_p, x_s.shape[0] // tm_s
    clamp_p = lambda i, fs, fl: (jnp.minimum(i, ntp - 1), 0, 0)
    clamp_s = lambda i, fs, fl: (jnp.clip(i - ntp, 0, nts - 1), 0, 0)
    grid_spec = pltpu.PrefetchScalarGridSpec(
        num_scalar_prefetch=2,
        grid=(ntp + nts + 1,),
        in_specs=[pl.BlockSpec((1, 1, TOP_K * tm_p), clamp_p, memory_space=pltpu.SMEM),
                  pl.BlockSpec((tm_p, sub, lanes), clamp_p),
                  pl.BlockSpec((1, 1, TOP_K * tm_s), clamp_s, memory_space=pltpu.SMEM),
                  pl.BlockSpec((tm_s, sub, lanes), clamp_s)],
        out_specs=pl.BlockSpec(memory_space=pl.ANY),
        scratch_shapes=[pltpu.VMEM((FFN_ROWS, sub, lanes), F32), pltpu.SemaphoreType.DMA(())],
    )
    return pl.pallas_call(
        functools.partial(_dispatch_kernel, ntp=ntp, nts=nts),
        grid_spec=grid_spec,
        out_shape=jax.ShapeDtypeStruct((n_rows, sub, lanes), F32),
        compiler_params=_params("arbitrary"),
        name="moe_dispatch",
    )(fill_start, fill_len, pos3_p, x_p, pos3_s, x_s)


def _ffn_kernel(te_ref, nu_ref, x_ref, wgu_ref, bgu_ref, wd_ref, bd_ref, y_ref, wgu16_ref, wd16_ref):
    i = pl.program_id(0)
    d_ff = wd_ref.shape[1]

    @pl.when((i == 0) | (te_ref[i] != te_ref[jnp.maximum(i - 1, 0)]))
    def _():
        wgu16_ref[...] = wgu_ref[0].astype(BF16)
        wd16_ref[...] = wd_ref[0].astype(BF16)

    @pl.when(i < nu_ref[0])
    def _():
        x = x_ref[...].reshape(x_ref.shape[0], -1)
        gu = _dot(x.astype(BF16), wgu16_ref[...]) + bgu_ref[0]
        gate = jnp.minimum(gu[:, :d_ff], SWIGLU_LIMIT)
        up = jnp.clip(gu[:, d_ff:], -SWIGLU_LIMIT, SWIGLU_LIMIT)
        h = (up + 1.0) * (gate * _sigmoid(SWIGLU_ALPHA * gate))
        y = _dot(h.astype(BF16), wd16_ref[...]) + bd_ref[0]
        y_ref[...] = y.reshape(y_ref.shape)

    @pl.when(i >= nu_ref[0])
    def _():
        y_ref[...] = jnp.zeros_like(y_ref)


def _ffn(tile_expert, n_used, xs, w_gu, b_gu, w_d, b_d, tm):
    r = xs.shape[0]
    d_ff, d = w_d.shape[1:]
    tiles = pl.BlockSpec((tm,) + _row_tile(d), lambda i, te, nu: (i, 0, 0))
    grid_spec = pltpu.PrefetchScalarGridSpec(
        num_scalar_prefetch=2,
        grid=(r // tm,),
        in_specs=[tiles,
                  pl.BlockSpec((1, d, 2 * d_ff), lambda i, te, nu: (te[i], 0, 0)),
                  pl.BlockSpec((1, 1, 2 * d_ff), lambda i, te, nu: (te[i], 0, 0)),
                  pl.BlockSpec((1, d_ff, d), lambda i, te, nu: (te[i], 0, 0)),
                  pl.BlockSpec((1, 1, d), lambda i, te, nu: (te[i], 0, 0))],
        out_specs=tiles,
        scratch_shapes=[pltpu.VMEM((d, 2 * d_ff), BF16), pltpu.VMEM((d_ff, d), BF16)],
    )
    return pl.pallas_call(
        _ffn_kernel,
        grid_spec=grid_spec,
        out_shape=jax.ShapeDtypeStruct(xs.shape, F32),
        compiler_params=_params("arbitrary"),
        name="moe_ffn",
    )(tile_expert, n_used, xs, w_gu, b_gu, w_d, b_d)


def _combine_kernel(pos_ref, pos_next_ref, y1_ref, gate_ref, gfin_ref, ys_ref, o_ref, buf_ref, sem):
    tm = y1_ref.shape[0]
    i = pl.program_id(0)
    slot = i % 2

    def gather(p_ref, s):
        def issue(g, _):
            for u in range(DMA_UNROLL):
                r = g * DMA_UNROLL + u
                for k in range(TOP_K):
                    _row_copy(ys_ref, p_ref[0, 0, TOP_K * r + k], buf_ref.at[s, k], r,
                              sem.at[s]).start(priority=k % 2)
            return 0

        lax.fori_loop(0, tm // DMA_UNROLL, issue, 0)

    @pl.when(i == 0)
    def _():
        gather(pos_ref, 0)

    @pl.when(i + 1 < pl.num_programs(0))
    def _():
        gather(pos_next_ref, 1 - slot)

    def drain(g, _):
        for _ in range(DMA_UNROLL * TOP_K):
            _row_copy(ys_ref, 0, buf_ref.at[slot, 0], 0, sem.at[slot]).wait()
        return 0

    lax.fori_loop(0, tm // DMA_UNROLL, drain, 0)

    gate = gate_ref[...]
    moe = gate[:, 0:1] * buf_ref[slot, 0].reshape(tm, -1)
    for k in range(1, TOP_K):
        moe = moe + gate[:, k:k + 1] * buf_ref[slot, k].reshape(tm, -1)
    y = y1_ref[...] + moe
    o_ref[...] = (y * lax.rsqrt(jnp.mean(y * y, axis=-1, keepdims=True) + EPS)) * gfin_ref[...]


def _combine(pos3, y1, rgate, g_final, ys, tm):
    t, d = y1.shape
    nt = t // tm
    pos_spec = lambda f: pl.BlockSpec((1, 1, TOP_K * tm), f, memory_space=pltpu.SMEM)
    return pl.pallas_call(
        _combine_kernel,
        grid=(nt,),
        in_specs=[pos_spec(lambda i: (i, 0, 0)), pos_spec(lambda i: (jnp.minimum(i + 1, nt - 1), 0, 0)),
                  pl.BlockSpec((tm, d), lambda i: (i, 0)),
                  pl.BlockSpec((tm, V7X_LANES), lambda i: (i, 0)),
                  pl.BlockSpec((1, d), lambda i: (0, 0)),
                  pl.BlockSpec(memory_space=pl.ANY)],
        out_specs=pl.BlockSpec((tm, d), lambda i: (i, 0)),
        out_shape=jax.ShapeDtypeStruct((t, d), F32),
        scratch_shapes=[pltpu.VMEM((2, TOP_K, tm) + _row_tile(d), F32), pltpu.SemaphoreType.DMA((2,))],
        compiler_params=_params("arbitrary"),
        name="moe_combine",
    )(pos3, pos3, y1, rgate, g_final, ys)


def _mixer(x, hist, fox_cache, ml_state, w, tm, tq, tk, chunk):
    bsz, seq, _ = x.shape
    hist8 = jnp.pad(hist, ((0, 0), (V7X_SUBLANES - (CONV_W - 1), 0), (0, 0)))
    proj = _project(x, hist8, w["g_mix"], w["wbig"], w["wsm"], w["bsm"], w["conv_w"], w["conv_b"], tm,
                    aug=fox_cache is None)

    if fox_cache is None:
        q_offset = 0
        assert seq % tk == 0
        (fq, fk32, fv32, ka0, ka1, va0, va1, small, flogf, mq, mk, mv, mo, conv_new) = proj
    else:
        (fq, fk32, fv32, fk16, fv16, small, flogf, mq, mk, mv, mo, conv_new) = proj
        ck_c, cv_c, clogf_c = fox_cache
        q_offset = ck_c.shape[1]
        assert q_offset + seq <= tk
        clogf_c = jnp.pad(clogf_c, ((0, 0), (0, 0), (S_FF, V7X_LANES - S_FF - FOX_HEADS)))
        lf_all = jnp.pad(jnp.concatenate([clogf_c, small], axis=1), ((0, 0), (0, tk - q_offset - seq), (0, 0)))
        ka0, ka1, va0, va1 = _fox_prep(lf_all, [ck_c.reshape(bsz, q_offset, FOX_W), fk16],
                                       [cv_c.reshape(bsz, q_offset, FOX_W), fv16], tk)
    cat_a = _fox(fq, ka0, ka1, va0, va1, w["g_fox"], q_offset, tq, tk)

    c0, n0, m0 = ml_state
    m0 = jnp.pad(m0, ((0, 0), (S_MI, V7X_LANES - S_MI - ML_HEADS)))
    cat_m, c_new, n_new, m_new = _mlstm(mq, mk, mv, mo, small, c0, n0, m0, w["g_ml"], chunk)
    m_new = m_new[:, S_MI:S_MI + ML_HEADS]

    states = (fk32.reshape(bsz, seq, FOX_HEADS, FOX_HD), fv32.reshape(bsz, seq, FOX_HEADS, FOX_HD), flogf,
              c_new, n_new, m_new, conv_new)
    return cat_a, cat_m, states


def kernel(x_prompt, x_sample, cache_fox_k, cache_fox_v, cache_fox_logf, state_mlstm_C, state_mlstm_n,
           state_mlstm_m, state_mlstm_conv, norm_mix_g, w_in, b_fox_f, conv_w, conv_b, b_ml_i, b_ml_f,
           g_fox, g_ml, w_out, norm_ffn_g, w_router, b_router, w_gate_up, b_gate_up, w_down, b_down,
           norm_final_g):
    depth = w_in.shape[0]
    assert depth == 1, "the final norm is fused into the last layer's combine; only depth 1 is wired up"
    bp, sp, d = x_prompt.shape
    bs, ss, _ = x_sample.shape
    n_exp = w_router.shape[-1]
    d_ff = w_down.shape[2]
    yp, ys = x_prompt, x_sample
    p_st, s_st = [], []

    for l in range(depth):
        wl = w_in[l]
        w = {
            "g_mix": norm_mix_g[l][None, :],
            "wbig": jnp.concatenate([wl[:, O_FQ:O_FF], wl[:, O_MQ:O_MI]], axis=1).astype(BF16),
            "wsm": jnp.pad(jnp.concatenate([wl[:, O_FF:O_MQ], wl[:, O_MI:P_IN]], axis=1),
                           ((0, 0), (0, V7X_LANES - S_END))).astype(BF16),
            "bsm": jnp.pad(jnp.concatenate([b_fox_f[l], b_ml_i[l], b_ml_f[l]]), (0, V7X_LANES - S_END))[None, :],
            "conv_w": conv_w[l],
            "conv_b": conv_b[l][None, :],
            "g_fox": g_fox[l][None, :],
            "g_ml": g_ml[l][None, :],
        }
        w_a = w_out[l][:FOX_W].astype(BF16)
        w_m = w_out[l][FOX_W:].astype(BF16)
        g_ffn = norm_ffn_g[l][None, :]
        wr = jnp.pad(w_router[l], ((0, 0), (0, V7X_LANES - n_exp)))
        wr_hi = wr.astype(BF16)
        wr_lo = (wr - wr_hi.astype(F32)).astype(BF16)
        br = jnp.pad(b_router[l], (0, V7X_LANES - n_exp), constant_values=NEG)[None, :]

        zeros_state = (jnp.zeros((bp, ML_HEADS, ML_DV, ML_DK), F32), jnp.zeros((bp, ML_HEADS, ML_DK), F32),
                       jnp.zeros((bp, ML_HEADS), F32))
        cat_a_p, cat_m_p, st_p = _mixer(yp, jnp.zeros((bp, CONV_W - 1, QK_W), F32), None, zeros_state, w,
                                        min(PROJ_ROWS, sp), min(FOX_QBLOCK, sp), min(FOX_BLOCK, sp), CHUNK)
        p_st.append(st_p)
        past = cache_fox_k.shape[2]
        tk_s = -(-(past + ss) // V7X_LANES) * V7X_LANES
        cat_a_s, cat_m_s, st_s = _mixer(
            ys, state_mlstm_conv[l], (cache_fox_k[l], cache_fox_v[l], cache_fox_logf[l]),
            (state_mlstm_C[l].astype(F32), state_mlstm_n[l].astype(F32), state_mlstm_m[l].astype(F32)),
            w, ss, ss, tk_s, ss)
        s_st.append(st_s)

        tp, ts = bp * sp, bs * ss
        y1_p, xn_p, ridx_p, rgate_p, cnt_p = _merge(
            cat_a_p.reshape(tp, FOX_W), cat_m_p.reshape(tp, ML_W), yp.reshape(tp, d), w_a, w_m, g_ffn,
            wr_hi, wr_lo, br, min(MERGE_ROWS, tp))
        y1_s, xn_s, ridx_s, rgate_s, cnt_s = _merge(
            cat_a_s.reshape(ts, FOX_W), cat_m_s.reshape(ts, ML_W), ys.reshape(ts, d), w_a, w_m, g_ffn,
            wr_hi, wr_lo, br, min(MERGE_ROWS, ts))

        cnt_p = cnt_p[0, :n_exp].astype(jnp.int32)
        cnt_s = cnt_s[0, :n_exp].astype(jnp.int32)
        seg_rows = -(-(cnt_p + cnt_s) // FFN_ROWS) * FFN_ROWS
        seg_end = jnp.cumsum(seg_rows)
        seg_start = seg_end - seg_rows
        n_rows = -(-((tp + ts) * TOP_K + n_exp * (FFN_ROWS - 1)) // FFN_ROWS) * FFN_ROWS
        n_tiles = n_rows // FFN_ROWS
        tile_row0 = jnp.arange(n_tiles, dtype=jnp.int32) * FFN_ROWS
        tile_expert = jnp.minimum(jnp.sum((seg_end[None, :] <= tile_row0[:, None]).astype(jnp.int32), axis=1),
                                  n_exp - 1)
        n_used = (seg_end[-1:] // FFN_ROWS).astype(jnp.int32)
        e_p, rank_p = ridx_p[:, :TOP_K], ridx_p[:, TOP_K:2 * TOP_K]
        e_s, rank_s = ridx_s[:, :TOP_K], ridx_s[:, TOP_K:2 * TOP_K]
        pos_p = seg_start[e_p] + rank_p
        pos_s = seg_start[e_s] + cnt_p[e_s] + rank_s
        tm_p, tm_s = min(MOE_ROWS, tp), min(MOE_ROWS, ts)
        pos3_p = pos_p.reshape(tp // tm_p, 1, TOP_K * tm_p)
        pos3_s = pos_s.reshape(ts // tm_s, 1, TOP_K * tm_s)

        cnt = cnt_p + cnt_s
        fill_start = jnp.concatenate([seg_start + cnt, seg_end[-1:]]).astype(jnp.int32)
        fill_len = jnp.concatenate([seg_rows - cnt, n_rows - seg_end[-1:]]).astype(jnp.int32)
        xs = _dispatch(fill_start, fill_len, pos3_p, xn_p, pos3_s, xn_s, n_rows, tm_p, tm_s)
        ysort = _ffn(tile_expert, n_used, xs, w_gate_up[l], b_gate_up[l].reshape(n_exp, 1, 2 * d_ff),
                     w_down[l], b_down[l].reshape(n_exp, 1, d), FFN_ROWS)

        g_fin = norm_final_g[None, :]
        yp = _combine(pos3_p, y1_p, rgate_p, g_fin, ysort, tm_p).reshape(bp, sp, d)
        ys = _combine(pos3_s, y1_s, rgate_s, g_fin, ysort, tm_s).reshape(bs, ss, d)

    p_out = tuple(jnp.stack(a) for a in zip(*p_st))
    s_out = tuple(jnp.stack(a) for a in zip(*s_st))
    return (yp, ys) + p_out + s_out
```

```python
import functools
import math

import numpy as np

import jax
import jax.numpy as jnp
from jax import lax
from jax.experimental import pallas as pl
from jax.experimental.pallas import tpu as pltpu

F32 = jnp.float32
BF16 = jnp.bfloat16

FOX_HEADS = 8
FOX_HD = 64
FOX_W = FOX_HEADS * FOX_HD
ML_HEADS = 4
ML_DK = 128
ML_DV = 128
ML_W = ML_HEADS * ML_DV
QK_W = 2 * ML_HEADS * ML_DK
CONV_W = 4
CHUNK = 64
TOP_K = 4
SWIGLU_LIMIT = 7.0
SWIGLU_ALPHA = 1.702
EPS = 1e-6
NEG = -1e30
LOG2E = math.log2(math.e)

O_FQ = 0
O_FK = O_FQ + FOX_W
O_FV = O_FK + FOX_W
O_FF = O_FV + FOX_W
O_MQ = O_FF + FOX_HEADS
O_MK = O_MQ + ML_HEADS * ML_DK
O_MV = O_MK + ML_HEADS * ML_DK
O_MO = O_MV + ML_W
O_MI = O_MO + ML_W
O_MF = O_MI + ML_HEADS
P_IN = O_MF + ML_HEADS

B_FQ, B_FK, B_FV, B_QK, B_MV, B_MO, B_END = 0, 512, 1024, 1536, 2560, 3072, 3584
S_FF, S_MI, S_MF, S_END = 0, 8, 12, 16
BIAS_TERMS = 3

V7X_LANES = 128
V7X_SUBLANES = 8
V7X_VMEM_LIMIT_BYTES = 56 * 1024 * 1024

PROJ_ROWS = 512
FOX_BLOCK = 512
FOX_QBLOCK = 1024
FOX_SUB = 512
FOX_UNROLLS = (4, 2, 1)
MERGE_ROWS = 512
MLSTM_CHUNKS_PER_STEP = 2
MOE_ROWS = 512
FFN_ROWS = 512
DMA_UNROLL = 8


def _params(*semantics):
    return pltpu.CompilerParams(dimension_semantics=semantics, vmem_limit_bytes=V7X_VMEM_LIMIT_BYTES)


def _dot(a, b):
    return jnp.dot(a, b, preferred_element_type=F32)


def _dot_nt(a, b):
    return lax.dot_general(a, b, (((1,), (1,)), ((), ())), preferred_element_type=F32)


def _mxu_transpose(eye, x):
    return _dot_nt(eye, x)


def _sigmoid(x):
    return 1.0 / (1.0 + jnp.exp(-x))


def _log_sigmoid(x):
    return jnp.minimum(x, 0.0) - jnp.log1p(jnp.exp(-jnp.abs(x)))


def _split3(x):
    t1 = x.astype(BF16)
    r1 = x - t1.astype(F32)
    t2 = r1.astype(BF16)
    t3 = (r1 - t2.astype(F32)).astype(BF16)
    return t1, t2, t3


def _lower_tri(n):
    r = lax.broadcasted_iota(jnp.int32, (n, n), 0)
    c = lax.broadcasted_iota(jnp.int32, (n, n), 1)
    return c <= r


def _proj_kernel(*refs, aug):
    x_ref, g_ref, wbig_ref, wsm_ref, bsm_ref, cw_ref, cb_ref, hist_ref = refs[:8]
    rest = refs[8:]
    if aug:
        place_ref, rest = rest[0], rest[1:]
        (fq_ref, fk32_ref, fv32_ref, ka0_ref, ka1_ref, va0_ref, va1_ref, sm_ref, flogf_ref, mq_ref, mk_ref,
         mv_ref, mo_ref, cnew_ref, halo_ref, carry_ref) = rest
    else:
        (fq_ref, fk32_ref, fv32_ref, fk16_ref, fv16_ref, sm_ref, flogf_ref, mq_ref, mk_ref, mv_ref, mo_ref,
         cnew_ref, halo_ref) = rest
    tm = x_ref.shape[1]

    @pl.when(pl.program_id(1) == 0)
    def _():
        halo_ref[...] = hist_ref[0]
        if aug:
            carry_ref[...] = jnp.zeros_like(carry_ref)

    x = x_ref[0]
    xn = (x * lax.rsqrt(jnp.mean(x * x, axis=-1, keepdims=True) + EPS)) * g_ref[...]
    xb = xn.astype(BF16)
    zs = _dot(xb, wsm_ref[...]) + bsm_ref[...]
    z = _dot(xb, wbig_ref[...])

    fq_ref[0] = (z[:, B_FQ:B_FQ + FOX_W] * (FOX_HD ** -0.5 * LOG2E)).astype(BF16)
    fk = z[:, B_FK:B_FK + FOX_W]
    fv = z[:, B_FV:B_FV + FOX_W]
    fk32_ref[0] = fk
    fv32_ref[0] = fv
    mv_ref[0] = z[:, B_MV:B_MV + ML_W].astype(BF16)
    mo_ref[0] = _sigmoid(z[:, B_MO:B_MO + ML_W]).astype(BF16)

    lane = lax.broadcasted_iota(jnp.int32, zs.shape, 1)
    is_forget = (lane < S_MI) | ((lane >= S_MF) & (lane < S_END))
    sm = jnp.where(is_forget, _log_sigmoid(zs), zs)
    sm_ref[0] = sm
    flogf_ref[0] = sm[:, S_FF:S_MI]

    if aug:
        rows = lax.broadcasted_iota(jnp.int32, sm.shape, 0)
        c = sm
        sh = 1
        while sh < tm:
            c = c + jnp.where(rows >= sh, pltpu.roll(c, sh, axis=0), 0.0)
            sh *= 2
        c = c + carry_ref[...]
        carry_ref[...] = c[tm - 1:tm, :]
        b1, b2, b3 = _split3(c * (-LOG2E))
        kb = _dot(b1, place_ref[0]) + (_dot(b2, place_ref[1]) + _dot(b3, place_ref[2]))
        first_head = lax.broadcasted_iota(jnp.int32, fk.shape, 1) % V7X_LANES < FOX_HD
        ka0_ref[0] = jnp.where(first_head, fk, kb).astype(BF16)
        ka1_ref[0] = jnp.where(first_head, kb, fk).astype(BF16)
        va0_ref[0] = jnp.where(first_head, fv, 1.0).astype(BF16)
        va1_ref[0] = jnp.where(first_head, 1.0, fv).astype(BF16)
    else:
        fk16_ref[0] = fk.astype(BF16)
        fv16_ref[0] = fv.astype(BF16)

    u = z[:, B_QK:B_QK + QK_W]
    up = jnp.concatenate([halo_ref[...], u], axis=0)
    n = tm + V7X_SUBLANES
    first = V7X_SUBLANES - (CONV_W - 1)
    y = cb_ref[...] + cw_ref[0:1, :] * pltpu.roll(up, n - first, axis=0)[:tm]
    for j in range(1, CONV_W - 1):
        y = y + cw_ref[j:j + 1, :] * pltpu.roll(up, n - (first + j), axis=0)[:tm]
    y = y + cw_ref[CONV_W - 1:CONV_W, :] * u
    qk = y * _sigmoid(y)
    mq_ref[0] = qk[:, :QK_W // 2].astype(BF16)
    mk_ref[0] = (qk[:, QK_W // 2:] * (ML_DK ** -0.5)).astype(BF16)

    halo_ref[...] = u[tm - V7X_SUBLANES:, :]
    cnew_ref[0] = halo_ref[first:, :]


def _project(x, hist8, g, wbig, wsm, bsm, cw, cb, tm, aug):
    bsz, seq, d = x.shape
    grid = (bsz, seq // tm)
    row = lambda c: pl.BlockSpec((1, tm, c), lambda b, s: (b, s, 0))
    const = lambda shape: pl.BlockSpec(shape, lambda b, s: (0,) * len(shape))
    n_kv = 4 if aug else 2
    outs = ([(FOX_W, BF16), (FOX_W, F32), (FOX_W, F32)] + [(FOX_W, BF16)] * n_kv
            + [(V7X_LANES, F32), (FOX_HEADS, F32), (ML_W, BF16), (ML_W, BF16), (ML_W, BF16), (ML_W, BF16)])
    out_shape = [jax.ShapeDtypeStruct((bsz, seq, c), dt) for c, dt in outs]
    out_specs = [row(c) for c, _ in outs]
    out_shape.append(jax.ShapeDtypeStruct((bsz, CONV_W - 1, QK_W), F32))
    out_specs.append(pl.BlockSpec((1, CONV_W - 1, QK_W), lambda b, s: (b, 0, 0)))
    in_specs = [row(d), const((1, d)), const(wbig.shape), const(wsm.shape), const((1, V7X_LANES)),
                const((CONV_W, QK_W)), const((1, QK_W)),
                pl.BlockSpec((1, V7X_SUBLANES, QK_W), lambda b, s: (b, 0, 0))]
    args = [x, g, wbig, wsm, bsm, cw, cb, hist8]
    scratch = [pltpu.VMEM((V7X_SUBLANES, QK_W), F32)]
    if aug:
        place = _bias_placement()
        in_specs.append(const(place.shape))
        args.append(place)
        scratch.append(pltpu.VMEM((1, V7X_LANES), F32))
    return pl.pallas_call(
        functools.partial(_proj_kernel, aug=aug),
        grid=grid,
        in_specs=in_specs,
        out_specs=out_specs,
        out_shape=out_shape,
        scratch_shapes=scratch,
        compiler_params=_params("arbitrary", "arbitrary"),
        name="proj",
    )(*args)


def _bias_lane(head, term):
    return (head // 2) * V7X_LANES + (FOX_HD if head % 2 == 0 else 0) + term


def _bias_placement():
    place = np.zeros((BIAS_TERMS, V7X_LANES, FOX_W), np.float32)
    for h in range(FOX_HEADS):
        for t in range(BIAS_TERMS):
            place[t, S_FF + h, _bias_lane(h, t)] = 1.0
    return jnp.asarray(place, BF16)


def _fox_prep_kernel(lf_ref, *refs, n_parts):
    k_refs, v_refs = refs[:n_parts], refs[n_parts:2 * n_parts]
    place_ref, ka0_ref, ka1_ref, va0_ref, va1_ref, carry_ref = refs[2 * n_parts:]
    tm = lf_ref.shape[1]

    def rows(part_refs):
        parts = [r[0].astype(F32) for r in part_refs]
        missing = tm - sum(p.shape[0] for p in parts)
        if missing:
            parts.append(jnp.zeros((missing, FOX_W), F32))
        return parts[0] if len(parts) == 1 else jnp.concatenate(parts, axis=0)

    @pl.when(pl.program_id(1) == 0)
    def _():
        carry_ref[...] = jnp.zeros_like(carry_ref)

    tri = jnp.where(_lower_tri(tm), 1.0, 0.0).astype(BF16)
    t1, t2, t3 = _split3(lf_ref[0])
    c = carry_ref[...] + (_dot(tri, t1) + (_dot(tri, t2) + _dot(tri, t3)))
    carry_ref[...] = c[tm - 1:tm, :]
    b1, b2, b3 = _split3(c * (-LOG2E))
    kb = _dot(b1, place_ref[0]) + (_dot(b2, place_ref[1]) + _dot(b3, place_ref[2]))

    k = rows(k_refs)
    v = rows(v_refs)
    first_head = lax.broadcasted_iota(jnp.int32, k.shape, 1) % V7X_LANES < FOX_HD
    ka0_ref[0] = jnp.where(first_head, k, kb).astype(BF16)
    ka1_ref[0] = jnp.where(first_head, kb, k).astype(BF16)
    va0_ref[0] = jnp.where(first_head, v, 1.0).astype(BF16)
    va1_ref[0] = jnp.where(first_head, 1.0, v).astype(BF16)


def _fox_prep(lf, k_parts, v_parts, tm):
    bsz, lk, _ = lf.shape
    place = _bias_placement()
    wide = pl.BlockSpec((1, tm, FOX_W), lambda b, s: (b, s, 0))
    if len(k_parts) == 1:
        part_specs = [wide, wide]
    else:
        assert lk == tm, "several key/value parts are only stitched inside a single tile"
        part_specs = [pl.BlockSpec((1,) + p.shape[1:], lambda b, s: (b, 0, 0)) for p in k_parts + v_parts]
    sds = jax.ShapeDtypeStruct((bsz, lk, FOX_W), BF16)
    return pl.pallas_call(
        functools.partial(_fox_prep_kernel, n_parts=len(k_parts)),
        grid=(bsz, lk // tm),
        in_specs=[pl.BlockSpec((1, tm, V7X_LANES), lambda b, s: (b, s, 0))] + part_specs
        + [pl.BlockSpec(place.shape, lambda b, s: (0, 0, 0))],
        out_specs=[wide, wide, wide, wide],
        out_shape=[sds, sds, sds, sds],
        scratch_shapes=[pltpu.VMEM((1, V7X_LANES), F32)],
        compiler_params=_params("arbitrary", "arbitrary"),
        name="fox_prep",
    )(lf, *k_parts, *v_parts, place)


def _fox_kernel(q_ref, ka0_ref, ka1_ref, va0_ref, va1_ref, g_ref, o_ref, *, tq, tk, sub, q_offset):
    ts = min(tq, sub)
    n_sub = tq // ts
    diag_blocks = max(ts // tk, 1)
    k_refs = (ka0_ref, ka1_ref)
    v_refs = (va0_ref, va1_ref)
    lane = lax.broadcasted_iota(jnp.int32, (ts, V7X_LANES), 1)
    first_head = lane < FOX_HD
    q_lo = q_offset + pl.program_id(2) * tq
    n_common = (q_lo + 1) // tk
    single_block = ka0_ref.shape[1] == tk
    even_common = q_offset % (2 * tk) == 0 and tq % (2 * tk) == 0

    def chain_q(r, hh):
        q = q_ref[0, r * ts:(r + 1) * ts, :].astype(F32)
        if hh == 0:
            return jnp.where(first_head, q, jnp.where(lane < FOX_HD + BIAS_TERMS, 1.0, 0.0)).astype(BF16)
        return jnp.where(first_head, jnp.where(lane < BIAS_TERMS, 1.0, 0.0), q).astype(BF16)

    chains = [(r, hh) for r in range(n_sub) for hh in range(2)]
    qs = [chain_q(r, hh) for r, hh in chains]

    def scores(c, j):
        return _dot_nt(qs[c], k_refs[chains[c][1]][0, pl.ds(pl.multiple_of(j * tk, tk), tk), :])

    def update(state, c, j, s, masked):
        r, hh = chains[c]
        m, acc = state
        start = pl.multiple_of(j * tk, tk)
        if masked:
            qpos = q_lo + r * ts + lax.broadcasted_iota(jnp.int32, s.shape, 0)
            kpos = start + lax.broadcasted_iota(jnp.int32, s.shape, 1)
            s = jnp.where(kpos <= qpos, s, NEG)
        m_new = jnp.maximum(m, jnp.max(s, axis=-1, keepdims=True))
        p = jnp.exp2(s - m_new).astype(BF16)
        acc = jnp.exp2(m - m_new) * acc + _dot(p, v_refs[hh][0, pl.ds(start, tk), :])
        return m_new, acc

    def step(state, c, j, masked):
        return update(state, c, j, scores(c, j), masked)

    def common_run(t, states, nblk):
        states = list(states)
        cur = [scores(c, nblk * t) for c in range(len(chains))]
        for e in range(nblk):
            nxt = []
            for c in range(len(chains)):
                states[c] = update(states[c], c, nblk * t + e, cur[c], False)
                if e + 1 < nblk:
                    nxt.append(scores(c, nblk * t + e + 1))
            cur = nxt
        return tuple(states)

    init = (jnp.full((ts, 1), NEG, F32), jnp.zeros((ts, V7X_LANES), F32))
    states = (init,) * len(chains)
    if not single_block:
        done = 0
        for nblk in FOX_UNROLLS:
            if nblk == 1 and even_common:
                continue
            trips = n_common // nblk
            states = lax.fori_loop(done // nblk, trips, functools.partial(common_run, nblk=nblk), states)
            done = trips * nblk
    states = list(states)
    for c, (r, _) in enumerate(chains):
        for e in range(r * diag_blocks):
            states[c] = step(states[c], c, n_common + e, False)
        for e in range(diag_blocks):
            states[c] = step(states[c], c, n_common + r * diag_blocks + e, True)

    for r in range(n_sub):
        out = jnp.zeros((ts, V7X_LANES), F32)
        for hh in range(2):
            acc = states[chains.index((r, hh))][1]
            own = first_head if hh == 0 else jnp.logical_not(first_head)
            denom_lane = FOX_HD if hh == 0 else 0
            o = jnp.where(own, acc / acc[:, denom_lane:denom_lane + 1], 0.0)
            out = out + o * lax.rsqrt(jnp.sum(o * o, axis=-1, keepdims=True) * (1.0 / FOX_HD) + EPS)
        o_ref[0, r * ts:(r + 1) * ts, :] = (out * g_ref[...]).astype(BF16)


def _fox(q, ka0, ka1, va0, va1, g_fox, q_offset, tq, tk):
    bsz, lq, _ = q.shape
    lk = ka0.shape[1]
    assert (q_offset % tk == 0 and tq % tk == 0) or (lq == tq <= tk and lk == tk), (q_offset, tq, tk, lq, lk)
    pairs = FOX_HEADS // 2
    grid = (bsz, pairs, lq // tq)
    kv_spec = pl.BlockSpec((1, lk, V7X_LANES), lambda b, p, i: (b, 0, p))
    q_spec = pl.BlockSpec((1, tq, V7X_LANES), lambda b, p, i: (b, i, p))
    return pl.pallas_call(
        functools.partial(_fox_kernel, tq=tq, tk=tk, sub=FOX_SUB, q_offset=q_offset),
        grid=grid,
        in_specs=[q_spec, kv_spec, kv_spec, kv_spec, kv_spec,
                  pl.BlockSpec((1, V7X_LANES), lambda b, p, i: (0, p))],
        out_specs=q_spec,
        out_shape=jax.ShapeDtypeStruct((bsz, lq, FOX_W), BF16),
        compiler_params=_params("arbitrary", "arbitrary", "arbitrary"),
        name="fox_attention",
    )(q, ka0, ka1, va0, va1, g_fox)


def _mlstm_kernel(q_ref, k_ref, v_ref, mo_ref, sm_ref, c0_ref, n0_ref, m0_ref, g_ref,
                  o_ref, c_ref, n_ref, m_ref, *, L):
    bsz = q_ref.shape[0]
    chunk_rows = [slice(c * L, (c + 1) * L) for c in range(q_ref.shape[1] // L)]

    @pl.when(pl.program_id(0) == 0)
    def _():
        c_ref[...] = c0_ref[...]
        n_ref[...] = n0_ref[...]
        m_ref[...] = m0_ref[...]

    causal = _lower_tri(L)
    tri = jnp.where(causal, 1.0, 0.0).astype(BF16)
    eye = jnp.where(lax.broadcasted_iota(jnp.int32, (V7X_LANES, V7X_LANES), 0)
                    == lax.broadcasted_iota(jnp.int32, (V7X_LANES, V7X_LANES), 1), 1.0, 0.0).astype(BF16)
    rows = lax.broadcasted_iota(jnp.int32, (L, V7X_LANES), 0)
    lanes = lax.broadcasted_iota(jnp.int32, (L, V7X_LANES), 1)
    gate_lanes = (lanes >= S_MI) & (lanes < S_END)

    heads = [(b, h) for b in range(bsz) for h in range(ML_HEADS)]
    cols = lambda h: slice(h * ML_DK, (h + 1) * ML_DK)

    ones = jnp.ones((V7X_LANES, V7X_LANES), BF16)

    sms, bcums, qk = {}, {}, {}
    for ci, rs in enumerate(chunk_rows):
        for b in range(bsz):
            sms[ci, b] = jnp.where(gate_lanes, sm_ref[b, rs, :], 0.0)
            t1, t2, t3 = _split3(sms[ci, b])
            bcums[ci, b] = _dot(tri, t1) + (_dot(tri, t2) + _dot(tri, t3))
    for ci, rs in enumerate(chunk_rows):
        for b, h in heads:
            qk[ci, b, h] = _dot_nt(q_ref[b, rs, cols(h)], k_ref[b, rs, cols(h)])

    for ci, rs in enumerate(chunk_rows):
        qc, qn = {}, {}
        for b, h in heads:
            qh = q_ref[b, rs, cols(h)]
            qc[b, h] = _dot_nt(qh, c_ref[b, h].astype(BF16))
            qn[b, h] = _dot((qh.astype(F32) * n_ref[b, h:h + 1, :]).astype(BF16), ones)

        gates = {}
        for b in range(bsz):
            sm = sms[ci, b]
            bcum = pltpu.roll(bcums[ci, b], V7X_LANES - (S_MF - S_MI), axis=1)
            g = sm - bcum
            gmax = g
            sh = 1
            while sh < L:
                gmax = jnp.maximum(gmax, jnp.where(rows >= sh, pltpu.roll(gmax, sh, axis=0), -jnp.inf))
                sh *= 2
            m = m_ref[pl.ds(b, 1), :]
            u = jnp.maximum(m, gmax)
            a = jnp.exp(m - u)
            mt = bcum + u
            m_ref[pl.ds(b, 1), :] = mt[L - 1:L, :]
            g1, g2, g3 = _split3(g)
            g_rows = _mxu_transpose(eye, g1) + (_mxu_transpose(eye, g2) + _mxu_transpose(eye, g3))
            gates[b] = dict(u=u, a=a, em=jnp.exp(-mt), a_last=a[L - 1:L, :],
                            wcol=jnp.exp(g - u[L - 1:L, :]), g_rows=g_rows)

        for b, h in heads:
            gl = S_MI + h
            kh = k_ref[b, rs, cols(h)]
            w_h = gates[b]["wcol"][:, gl:gl + 1]
            a_l = gates[b]["a_last"][:, gl:gl + 1]
            vw = (v_ref[b, rs, cols(h)].astype(F32) * w_h).astype(BF16)
            c_ref[b, h] = a_l * c_ref[b, h] + _dot(_mxu_transpose(eye, vw).astype(BF16), kh)
            n_ref[b, h:h + 1, :] = (a_l * n_ref[b, h:h + 1, :]
                                    + jnp.sum(kh.astype(F32) * w_h, axis=0, keepdims=True))

        for b, h in heads:
            gl = S_MI + h
            gb = gates[b]
            decay = jnp.exp(jnp.where(causal, gb["g_rows"][gl:gl + 1, :] - gb["u"][:, gl:gl + 1], NEG))
            s = qk[ci, b, h] * decay
            a_h = gb["a"][:, gl:gl + 1]
            sb = s.astype(BF16)
            num = a_h * qc[b, h] + _dot(sb, v_ref[b, rs, cols(h)])
            den = a_h * qn[b, h] + _dot(sb, ones[:L, :])
            hv = num / jnp.maximum(jnp.abs(den), gb["em"][:, gl:gl + 1])
            hn = hv * lax.rsqrt(_dot((hv * hv).astype(BF16), ones) * (1.0 / ML_DV) + EPS)
            o_ref[b, rs, cols(h)] = (hn * g_ref[:, cols(h)] * mo_ref[b, rs, cols(h)].astype(F32)).astype(BF16)


def _mlstm(mq, mk, mv, mo, small, c0, n0, m0, g_ml, L):
    bsz, seq, _ = mq.shape
    rows = L * math.gcd(seq // L, MLSTM_CHUNKS_PER_STEP)
    chunk = pl.BlockSpec((bsz, rows, ML_W), lambda c: (0, c, 0))
    full = lambda shape: pl.BlockSpec(shape, lambda c: (0,) * len(shape))
    return pl.pallas_call(
        functools.partial(_mlstm_kernel, L=L),
        grid=(seq // rows,),
        in_specs=[chunk, chunk, chunk, chunk,
                  pl.BlockSpec((bsz, rows, V7X_LANES), lambda c: (0, c, 0)),
                  full(c0.shape), full(n0.shape), full(m0.shape), full((1, ML_W))],
        out_specs=[chunk, full(c0.shape), full(n0.shape), full(m0.shape)],
        out_shape=[jax.ShapeDtypeStruct((bsz, seq, ML_W), BF16),
                   jax.ShapeDtypeStruct(c0.shape, F32),
                   jax.ShapeDtypeStruct(n0.shape, F32),
                   jax.ShapeDtypeStruct(m0.shape, F32)],
        compiler_params=_params("arbitrary"),
        name="mlstm",
    )(mq, mk, mv, mo, small, c0, n0, m0, g_ml)


def _merge_kernel(ca_ref, cm_ref, x_ref, wa_ref, wm_ref, g_ref, wrh_ref, wrl_ref, br_ref,
                  y1_ref, xn_ref, ridx_ref, rgate_ref, cnt_ref):
    tm = x_ref.shape[0]

    @pl.when(pl.program_id(0) == 0)
    def _():
        cnt_ref[...] = jnp.zeros_like(cnt_ref)

    y1 = x_ref[...] + (_dot(ca_ref[...], wa_ref[...]) + _dot(cm_ref[...], wm_ref[...]))
    y1_ref[...] = y1
    xn = (y1 * lax.rsqrt(jnp.mean(y1 * y1, axis=-1, keepdims=True) + EPS)) * g_ref[...]
    xn_ref[...] = xn.reshape(xn_ref.shape)

    xh = xn.astype(BF16)
    xl = (xn - xh.astype(F32)).astype(BF16)
    logits = (_dot(xh, wrh_ref[...]) + (_dot(xl, wrh_ref[...]) + _dot(xh, wrl_ref[...]))) + br_ref[...]

    lane = lax.broadcasted_iota(jnp.int32, logits.shape, 1)
    vals, sel = [], []
    work = logits
    for _ in range(TOP_K):
        mx = jnp.max(work, axis=-1, keepdims=True)
        idx = jnp.min(jnp.where(work == mx, lane, V7X_LANES), axis=-1, keepdims=True)
        vals.append(mx)
        sel.append(idx)
        work = jnp.where(lane == idx, -jnp.inf, work)
    ex = [jnp.exp(v - vals[0]) for v in vals]
    tot = ex[0] + ex[1] + ex[2] + ex[3]

    onehot = [(lane == idx) for idx in sel]
    picked = jnp.where(onehot[0] | onehot[1] | onehot[2] | onehot[3], 1.0, 0.0)
    earlier = (lax.broadcasted_iota(jnp.int32, (tm, tm), 1) < lax.broadcasted_iota(jnp.int32, (tm, tm), 0))
    before = _dot(jnp.where(earlier, 1.0, 0.0).astype(BF16), picked.astype(BF16)) + cnt_ref[...]
    cnt_ref[...] = cnt_ref[...] + jnp.sum(picked, axis=0, keepdims=True)

    ridx = jnp.zeros(logits.shape, jnp.int32)
    rgate = jnp.zeros(logits.shape, F32)
    for k in range(TOP_K):
        rank = jnp.sum(jnp.where(onehot[k], before, 0.0), axis=-1, keepdims=True).astype(jnp.int32)
        ridx = jnp.where(lane == k, sel[k], ridx)
        ridx = jnp.where(lane == TOP_K + k, rank, ridx)
        rgate = jnp.where(lane == k, ex[k] / tot, rgate)
    ridx_ref[...] = ridx
    rgate_ref[...] = rgate


def _merge(cat_a, cat_m, x, w_a, w_m, g_ffn, wr_hi, wr_lo, br, tm):
    t, d = x.shape
    row = lambda c: pl.BlockSpec((tm, c), lambda i: (i, 0))
    const = lambda shape: pl.BlockSpec(shape, lambda i: (0,) * len(shape))
    return pl.pallas_call(
        _merge_kernel,
        grid=(t // tm,),
        in_specs=[row(FOX_W), row(ML_W), row(d), const(w_a.shape), const(w_m.shape), const((1, d)),
                  const(wr_hi.shape), const(wr_lo.shape), const((1, V7X_LANES))],
        out_specs=[row(d), pl.BlockSpec((tm,) + _row_tile(d), lambda i: (i, 0, 0)), row(V7X_LANES),
                   row(V7X_LANES), const((1, V7X_LANES))],
        out_shape=[jax.ShapeDtypeStruct((t, d), F32), jax.ShapeDtypeStruct((t,) + _row_tile(d), F32),
                   jax.ShapeDtypeStruct((t, V7X_LANES), jnp.int32),
                   jax.ShapeDtypeStruct((t, V7X_LANES), F32),
                   jax.ShapeDtypeStruct((1, V7X_LANES), F32)],
        compiler_params=_params("arbitrary"),
        name="merge_router",
    )(cat_a, cat_m, x, w_a, w_m, g_ffn, wr_hi, wr_lo, br)


def _row_tile(d):
    return (d // V7X_LANES, V7X_LANES)


def _row_copy(src_ref, src_row, dst_ref, dst_row, sem):
    return pltpu.make_async_copy(src_ref.at[pl.ds(src_row, 1)], dst_ref.at[pl.ds(dst_row, 1)], sem)


def _scatter_rows(pos_ref, x_ref, xs_ref, sem):
    tm = x_ref.shape[0]

    def issue(g, _):
        for u in range(DMA_UNROLL):
            r = g * DMA_UNROLL + u
            for k in range(TOP_K):
                _row_copy(x_ref, r, xs_ref, pos_ref[0, 0, TOP_K * r + k], sem).start(priority=k % 2)
        return 0

    def drain(g, _):
        for _ in range(DMA_UNROLL * TOP_K):
            _row_copy(x_ref, 0, xs_ref, 0, sem).wait()
        return 0

    lax.fori_loop(0, tm // DMA_UNROLL, issue, 0)
    lax.fori_loop(0, tm // DMA_UNROLL, drain, 0)


def _dispatch_kernel(fs_ref, fl_ref, pos_p_ref, x_p_ref, pos_s_ref, x_s_ref, xs_ref, zero_ref, sem, *, ntp, nts):
    i = pl.program_id(0)

    @pl.when(i < ntp)
    def _():
        _scatter_rows(pos_p_ref, x_p_ref, xs_ref, sem)

    @pl.when((i >= ntp) & (i < ntp + nts))
    def _():
        _scatter_rows(pos_s_ref, x_s_ref, xs_ref, sem)

    @pl.when(i == ntp + nts)
    def _():
        zero_ref[...] = jnp.zeros_like(zero_ref)
        full = zero_ref.shape[0]
        bits = [1 << p for p in range(full.bit_length() - 2, -1, -1)]

        def zeros_to(start, size):
            return pltpu.make_async_copy(zero_ref.at[pl.ds(0, size)], xs_ref.at[pl.ds(start, size)], sem)

        def per_run(e, _, wait):
            start, n = fs_ref[e], fl_ref[e]
            whole = n // full

            def whole_copy(q, _):
                c = zeros_to(start + q * full, full)
                c.wait() if wait else c.start()
                return 0

            lax.fori_loop(0, whole, whole_copy, 0)
            for p in bits:
                @pl.when((n & p) != 0)
                def _():
                    c = zeros_to(start + (n & ~(2 * p - 1)), p)
                    c.wait() if wait else c.start()
            return 0

        lax.fori_loop(0, fs_ref.shape[0], functools.partial(per_run, wait=False), 0)
        lax.fori_loop(0, fs_ref.shape[0], functools.partial(per_run, wait=True), 0)


def _dispatch(fill_start, fill_len, pos3_p, x_p, pos3_s, x_s, n_rows, tm_p, tm_s):
    tp, sub, lanes = x_p.shape
    ntp, nts = tp // tm_p, x_s.shape[0] // tm_s
    clamp_p = lambda i, fs, fl: (jnp.minimum(i, ntp - 1), 0, 0)
    clamp_s = lambda i, fs, fl: (jnp.clip(i - ntp, 0, nts - 1), 0, 0)
    grid_spec = pltpu.PrefetchScalarGridSpec(
        num_scalar_prefetch=2,
        grid=(ntp + nts + 1,),
        in_specs=[pl.BlockSpec((1, 1, TOP_K * tm_p), clamp_p, memory_space=pltpu.SMEM),
                  pl.BlockSpec((tm_p, sub, lanes), clamp_p),
                  pl.BlockSpec((1, 1, TOP_K * tm_s), clamp_s, memory_space=pltpu.SMEM),
                  pl.BlockSpec((tm_s, sub, lanes), clamp_s)],
        out_specs=pl.BlockSpec(memory_space=pl.ANY),
        scratch_shapes=[pltpu.VMEM((FFN_ROWS, sub, lanes), F32), pltpu.SemaphoreType.DMA(())],
    )
    return pl.pallas_call(
        functools.partial(_dispatch_kernel, ntp=ntp, nts=nts),
        grid_spec=grid_spec,
        out_shape=jax.ShapeDtypeStruct((n_rows, sub, lanes), F32),
        compiler_params=_params("arbitrary"),
        name="moe_dispatch",
    )(fill_start, fill_len, pos3_p, x_p, pos3_s, x_s)


def _ffn_kernel(te_ref, nu_ref, x_ref, wgu_ref, bgu_ref, wd_ref, bd_ref, y_ref, wgu16_ref, wd16_ref):
    i = pl.program_id(0)
    d_ff = wd_ref.shape[1]

    @pl.when((i == 0) | (te_ref[i] != te_ref[jnp.maximum(i - 1, 0)]))
    def _():
        wgu16_ref[...] = wgu_ref[0].astype(BF16)
        wd16_ref[...] = wd_ref[0].astype(BF16)

    @pl.when(i < nu_ref[0])
    def _():
        x = x_ref[...].reshape(x_ref.shape[0], -1)
        gu = _dot(x.astype(BF16), wgu16_ref[...]) + bgu_ref[0]
        gate = jnp.minimum(gu[:, :d_ff], SWIGLU_LIMIT)
        up = jnp.clip(gu[:, d_ff:], -SWIGLU_LIMIT, SWIGLU_LIMIT)
        h = (up + 1.0) * (gate * _sigmoid(SWIGLU_ALPHA * gate))
        y = _dot(h.astype(BF16), wd16_ref[...]) + bd_ref[0]
        y_ref[...] = y.reshape(y_ref.shape)

    @pl.when(i >= nu_ref[0])
    def _():
        y_ref[...] = jnp.zeros_like(y_ref)


def _ffn(tile_expert, n_used, xs, w_gu, b_gu, w_d, b_d, tm):
    r = xs.shape[0]
    d_ff, d = w_d.shape[1:]
    tiles = pl.BlockSpec((tm,) + _row_tile(d), lambda i, te, nu: (i, 0, 0))
    grid_spec = pltpu.PrefetchScalarGridSpec(
        num_scalar_prefetch=2,
        grid=(r // tm,),
        in_specs=[tiles,
                  pl.BlockSpec((1, d, 2 * d_ff), lambda i, te, nu: (te[i], 0, 0)),
                  pl.BlockSpec((1, 1, 2 * d_ff), lambda i, te, nu: (te[i], 0, 0)),
                  pl.BlockSpec((1, d_ff, d), lambda i, te, nu: (te[i], 0, 0)),
                  pl.BlockSpec((1, 1, d), lambda i, te, nu: (te[i], 0, 0))],
        out_specs=tiles,
        scratch_shapes=[pltpu.VMEM((d, 2 * d_ff), BF16), pltpu.VMEM((d_ff, d), BF16)],
    )
    return pl.pallas_call(
        _ffn_kernel,
        grid_spec=grid_spec,
        out_shape=jax.ShapeDtypeStruct(xs.shape, F32),
        compiler_params=_params("arbitrary"),
        name="moe_ffn",
    )(tile_expert, n_used, xs, w_gu, b_gu, w_d, b_d)


def _combine_kernel(pos_ref, pos_next_ref, y1_ref, gate_ref, gfin_ref, ys_ref, o_ref, buf_ref, sem):
    tm = y1_ref.shape[0]
    i = pl.program_id(0)
    slot = i % 2

    def gather(p_ref, s):
        def issue(g, _):
            for u in range(DMA_UNROLL):
                r = g * DMA_UNROLL + u
                for k in range(TOP_K):
                    _row_copy(ys_ref, p_ref[0, 0, TOP_K * r + k], buf_ref.at[s, k], r,
                              sem.at[s]).start(priority=k % 2)
            return 0

        lax.fori_loop(0, tm // DMA_UNROLL, issue, 0)

    @pl.when(i == 0)
    def _():
        gather(pos_ref, 0)

    @pl.when(i + 1 < pl.num_programs(0))
    def _():
        gather(pos_next_ref, 1 - slot)

    def drain(g, _):
        for _ in range(DMA_UNROLL * TOP_K):
            _row_copy(ys_ref, 0, buf_ref.at[slot, 0], 0, sem.at[slot]).wait()
        return 0

    lax.fori_loop(0, tm // DMA_UNROLL, drain, 0)

    gate = gate_ref[...]
    moe = gate[:, 0:1] * buf_ref[slot, 0].reshape(tm, -1)
    for k in range(1, TOP_K):
        moe = moe + gate[:, k:k + 1] * buf_ref[slot, k].reshape(tm, -1)
    y = y1_ref[...] + moe
    o_ref[...] = (y * lax.rsqrt(jnp.mean(y * y, axis=-1, keepdims=True) + EPS)) * gfin_ref[...]


def _combine(pos3, y1, rgate, g_final, ys, tm):
    t, d = y1.shape
    nt = t // tm
    pos_spec = lambda f: pl.BlockSpec((1, 1, TOP_K * tm), f, memory_space=pltpu.SMEM)
    return pl.pallas_call(
        _combine_kernel,
        grid=(nt,),
        in_specs=[pos_spec(lambda i: (i, 0, 0)), pos_spec(lambda i: (jnp.minimum(i + 1, nt - 1), 0, 0)),
                  pl.BlockSpec((tm, d), lambda i: (i, 0)),
                  pl.BlockSpec((tm, V7X_LANES), lambda i: (i, 0)),
                  pl.BlockSpec((1, d), lambda i: (0, 0)),
                  pl.BlockSpec(memory_space=pl.ANY)],
        out_specs=pl.BlockSpec((tm, d), lambda i: (i, 0)),
        out_shape=jax.ShapeDtypeStruct((t, d), F32),
        scratch_shapes=[pltpu.VMEM((2, TOP_K, tm) + _row_tile(d), F32), pltpu.SemaphoreType.DMA((2,))],
        compiler_params=_params("arbitrary"),
        name="moe_combine",
    )(pos3, pos3, y1, rgate, g_final, ys)


def _mixer(x, hist, fox_cache, ml_state, w, tm, tq, tk, chunk):
    bsz, seq, _ = x.shape
    hist8 = jnp.pad(hist, ((0, 0), (V7X_SUBLANES - (CONV_W - 1), 0), (0, 0)))
    proj = _project(x, hist8, w["g_mix"], w["wbig"], w["wsm"], w["bsm"], w["conv_w"], w["conv_b"], tm,
                    aug=fox_cache is None)

    if fox_cache is None:
        q_offset = 0
        assert seq % tk == 0
        (fq, fk32, fv32, ka0, ka1, va0, va1, small, flogf, mq, mk, mv, mo, conv_new) = proj
    else:
        (fq, fk32, fv32, fk16, fv16, small, flogf, mq, mk, mv, mo, conv_new) = proj
        ck_c, cv_c, clogf_c = fox_cache
        q_offset = ck_c.shape[1]
        assert q_offset + seq <= tk
        clogf_c = jnp.pad(clogf_c, ((0, 0), (0, 0), (S_FF, V7X_LANES - S_FF - FOX_HEADS)))
        lf_all = jnp.pad(jnp.concatenate([clogf_c, small], axis=1), ((0, 0), (0, tk - q_offset - seq), (0, 0)))
        ka0, ka1, va0, va1 = _fox_prep(lf_all, [ck_c.reshape(bsz, q_offset, FOX_W), fk16],
                                       [cv_c.reshape(bsz, q_offset, FOX_W), fv16], tk)
    cat_a = _fox(fq, ka0, ka1, va0, va1, w["g_fox"], q_offset, tq, tk)

    c0, n0, m0 = ml_state
    m0 = jnp.pad(m0, ((0, 0), (S_MI, V7X_LANES - S_MI - ML_HEADS)))
    cat_m, c_new, n_new, m_new = _mlstm(mq, mk, mv, mo, small, c0, n0, m0, w["g_ml"], chunk)
    m_new = m_new[:, S_MI:S_MI + ML_HEADS]

    states = (fk32.reshape(bsz, seq, FOX_HEADS, FOX_HD), fv32.reshape(bsz, seq, FOX_HEADS, FOX_HD), flogf,
              c_new, n_new, m_new, conv_new)
    return cat_a, cat_m, states


def kernel(x_prompt, x_sample, cache_fox_k, cache_fox_v, cache_fox_logf, state_mlstm_C, state_mlstm_n,
           state_mlstm_m, state_mlstm_conv, norm_mix_g, w_in, b_fox_f, conv_w, conv_b, b_ml_i, b_ml_f,
           g_fox, g_ml, w_out, norm_ffn_g, w_router, b_router, w_gate_up, b_gate_up, w_down, b_down,
           norm_final_g):
    depth = w_in.shape[0]
    assert depth == 1, "the final norm is fused into the last layer's combine; only depth 1 is wired up"
    bp, sp, d = x_prompt.shape
    bs, ss, _ = x_sample.shape
    n_exp = w_router.shape[-1]
    d_ff = w_down.shape[2]
    yp, ys = x_prompt, x_sample
    p_st, s_st = [], []

    for l in range(depth):
        wl = w_in[l]
        w = {
            "g_mix": norm_mix_g[l][None, :],
            "wbig": jnp.concatenate([wl[:, O_FQ:O_FF], wl[:, O_MQ:O_MI]], axis=1).astype(BF16),
            "wsm": jnp.pad(jnp.concatenate([wl[:, O_FF:O_MQ], wl[:, O_MI:P_IN]], axis=1),
                           ((0, 0), (0, V7X_LANES - S_END))).astype(BF16),
            "bsm": jnp.pad(jnp.concatenate([b_fox_f[l], b_ml_i[l], b_ml_f[l]]), (0, V7X_LANES - S_END))[None, :],
            "conv_w": conv_w[l],
            "conv_b": conv_b[l][None, :],
            "g_fox": g_fox[l][None, :],
            "g_ml": g_ml[l][None, :],
        }
        w_a = w_out[l][:FOX_W].astype(BF16)
        w_m = w_out[l][FOX_W:].astype(BF16)
        g_ffn = norm_ffn_g[l][None, :]
        wr = jnp.pad(w_router[l], ((0, 0), (0, V7X_LANES - n_exp)))
        wr_hi = wr.astype(BF16)
        wr_lo = (wr - wr_hi.astype(F32)).astype(BF16)
        br = jnp.pad(b_router[l], (0, V7X_LANES - n_exp), constant_values=NEG)[None, :]

        zeros_state = (jnp.zeros((bp, ML_HEADS, ML_DV, ML_DK), F32), jnp.zeros((bp, ML_HEADS, ML_DK), F32),
                       jnp.zeros((bp, ML_HEADS), F32))
        cat_a_p, cat_m_p, st_p = _mixer(yp, jnp.zeros((bp, CONV_W - 1, QK_W), F32), None, zeros_state, w,
                                        min(PROJ_ROWS, sp), min(FOX_QBLOCK, sp), min(FOX_BLOCK, sp), CHUNK)
        p_st.append(st_p)
        past = cache_fox_k.shape[2]
        tk_s = -(-(past + ss) // V7X_LANES) * V7X_LANES
        cat_a_s, cat_m_s, st_s = _mixer(
            ys, state_mlstm_conv[l], (cache_fox_k[l], cache_fox_v[l], cache_fox_logf[l]),
            (state_mlstm_C[l].astype(F32), state_mlstm_n[l].astype(F32), state_mlstm_m[l].astype(F32)),
            w, ss, ss, tk_s, ss)
        s_st.append(st_s)

        tp, ts = bp * sp, bs * ss
        y1_p, xn_p, ridx_p, rgate_p, cnt_p = _merge(
            cat_a_p.reshape(tp, FOX_W), cat_m_p.reshape(tp, ML_W), yp.reshape(tp, d), w_a, w_m, g_ffn,
            wr_hi, wr_lo, br, min(MERGE_ROWS, tp))
        y1_s, xn_s, ridx_s, rgate_s, cnt_s = _merge(
            cat_a_s.reshape(ts, FOX_W), cat_m_s.reshape(ts, ML_W), ys.reshape(ts, d), w_a, w_m, g_ffn,
            wr_hi, wr_lo, br, min(MERGE_ROWS, ts))

        cnt_p = cnt_p[0, :n_exp].astype(jnp.int32)
        cnt_s = cnt_s[0, :n_exp].astype(jnp.int32)
        seg_rows = -(-(cnt_p + cnt_s) // FFN_ROWS) * FFN_ROWS
        seg_end = jnp.cumsum(seg_rows)
        seg_start = seg_end - seg_rows
        n_rows = -(-((tp + ts) * TOP_K + n_exp * (FFN_ROWS - 1)) // FFN_ROWS) * FFN_ROWS
        n_tiles = n_rows // FFN_ROWS
        tile_row0 = jnp.arange(n_tiles, dtype=jnp.int32) * FFN_ROWS
        tile_expert = jnp.minimum(jnp.sum((seg_end[None, :] <= tile_row0[:, None]).astype(jnp.int32), axis=1),
                                  n_exp - 1)
        n_used = (seg_end[-1:] // FFN_ROWS).astype(jnp.int32)
        e_p, rank_p = ridx_p[:, :TOP_K], ridx_p[:, TOP_K:2 * TOP_K]
        e_s, rank_s = ridx_s[:, :TOP_K], ridx_s[:, TOP_K:2 * TOP_K]
        pos_p = seg_start[e_p] + rank_p
        pos_s = seg_start[e_s] + cnt_p[e_s] + rank_s
        tm_p, tm_s = min(MOE_ROWS, tp), min(MOE_ROWS, ts)
        pos3_p = pos_p.reshape(tp // tm_p, 1, TOP_K * tm_p)
        pos3_s = pos_s.reshape(ts // tm_s, 1, TOP_K * tm_s)

        cnt = cnt_p + cnt_s
        fill_start = jnp.concatenate([seg_start + cnt, seg_end[-1:]]).astype(jnp.int32)
        fill_len = jnp.concatenate([seg_rows - cnt, n_rows - seg_end[-1:]]).astype(jnp.int32)
        xs = _dispatch(fill_start, fill_len, pos3_p, xn_p, pos3_s, xn_s, n_rows, tm_p, tm_s)
        ysort = _ffn(tile_expert, n_used, xs, w_gate_up[l], b_gate_up[l].reshape(n_exp, 1, 2 * d_ff),
                     w_down[l], b_down[l].reshape(n_exp, 1, d), FFN_ROWS)

        g_fin = norm_final_g[None, :]
        yp = _combine(pos3_p, y1_p, rgate_p, g_fin, ysort, tm_p).reshape(bp, sp, d)
        ys = _combine(pos3_s, y1_s, rgate_s, g_fin, ysort, tm_s).reshape(bs, ss, d)

    p_out = tuple(jnp.stack(a) for a in zip(*p_st))
    s_out = tuple(jnp.stack(a) for a in zip(*s_st))
    return (yp, ys) + p_out + s_out
```
